```python
import math
import jax, jax.numpy as jnp
from jax import lax
import numpy as np

D_MODEL = 2048
BATCH = 4
SEQ = 8192
DEPTH = 2

N_A_LAYERS = DEPTH // 2
N_B_LAYERS = DEPTH - N_A_LAYERS

GMLP_CHUNK = 128
GMLP_HALF = 3 * D_MODEL
GMLP_GROUPS = 8
GMLP_GROUP_DIM = GMLP_HALF // GMLP_GROUPS

NSA_HEAD_DIM = 128
NSA_HEADS = D_MODEL // NSA_HEAD_DIM
NSA_KV_GROUPS = 4
NSA_HPG = NSA_HEADS // NSA_KV_GROUPS
NSA_WIDTH = NSA_HEADS * NSA_HEAD_DIM
NSA_BRANCHES = 3
CMP_BLOCK = 32
CMP_STRIDE = 16
CMP_HIDDEN = 4 * NSA_HEAD_DIM
SLC_BLOCK = 64
N_SEL = 16
WINDOW = 512
NSA_QBLOCK = 64
FORCE_BONUS = 1000.0

PEER_HEADS = 8
PEER_NKEYS = 128
PEER_EXPERTS = PEER_NKEYS * PEER_NKEYS
PEER_TOPK = 16
PEER_KEY_DIM = 128
PEER_CHUNK = 128

RMS_EPS = 1e-6

kernel_name = 'yoco_gmlp_nsa_peer_trunk'


def rms_norm(x, g):
    x32 = x.astype(jnp.float32)
    y = x32 * lax.rsqrt(jnp.mean(x32 * x32, axis=-1, keepdims=True) + RMS_EPS)
    return (y * g.astype(jnp.float32)).astype(x.dtype)


def masked_softmax(s, mask):
    s = jnp.where(mask, s.astype(jnp.float32), -1e30)
    e = jnp.where(mask, jnp.exp(s - jnp.max(s, axis=-1, keepdims=True)), 0.0)
    return e / jnp.maximum(jnp.sum(e, axis=-1, keepdims=True), jnp.finfo(jnp.float32).tiny)


def alibi_slopes():
    h = jnp.arange(1, NSA_HEADS + 1, dtype=jnp.float32)
    return 2.0 ** (-8.0 * h / NSA_HEADS)


def chunked_gmlp(xn, w_in, v_norm, w_s, b_s, w_out):
    B, S, _ = xn.shape
    z = jax.nn.gelu(xn @ w_in)
    u, v = z[..., :GMLP_HALF], z[..., GMLP_HALF:]
    v = rms_norm(v, v_norm).reshape(B, S // GMLP_CHUNK, GMLP_CHUNK, GMLP_GROUPS, GMLP_GROUP_DIM)
    w_causal = jnp.tril(w_s)
    sv = jnp.einsum('gts,bcsgd->bctgd', w_causal, v) + b_s.T[:, :, None]
    return (u * sv.reshape(B, S, GMLP_HALF)) @ w_out


def compress_blocks(blocks, pe, w1, b1, w2, b2):
    flat = (blocks + pe).reshape(*blocks.shape[:-2], CMP_BLOCK * NSA_HEAD_DIM)
    return jax.nn.gelu(flat @ w1 + b1) @ w2 + b2


def nsa_shared_kv(h, kv_norm, kv_w, kv_k_norm, phi_pe, phi_w1, phi_b1, phi_w2, phi_b2):
    B, S, _ = h.shape
    kv = (rms_norm(h, kv_norm) @ kv_w).reshape(B, S, 6, NSA_KV_GROUPS, NSA_HEAD_DIM).transpose(2, 0, 3, 1, 4)
    k_raw, v_raw, k_s, v_s, k_w, v_w = kv[0], kv[1], kv[2], kv[3], kv[4], kv[5]
    n_cmp = (S - CMP_BLOCK) // CMP_STRIDE + 1
    idx = jnp.arange(n_cmp)[:, None] * CMP_STRIDE + jnp.arange(CMP_BLOCK)[None, :]
    k_c = compress_blocks(k_raw[:, :, idx], phi_pe[0], phi_w1[0], phi_b1[0], phi_w2[0], phi_b2[0])
    v_c = compress_blocks(v_raw[:, :, idx], phi_pe[1], phi_w1[1], phi_b1[1], phi_w2[1], phi_b2[1])
    k_c = rms_norm(k_c, kv_k_norm[0])
    k_s = rms_norm(k_s, kv_k_norm[1])
    k_w = rms_norm(k_w, kv_k_norm[2])
    n_slc = S // SLC_BLOCK
    k_s_blk = k_s.reshape(B, NSA_KV_GROUPS, n_slc, SLC_BLOCK, NSA_HEAD_DIM)
    v_s_blk = v_s.reshape(B, NSA_KV_GROUPS, n_slc, SLC_BLOCK, NSA_HEAD_DIM)
    pad = ((0, 0), (0, 0), (WINDOW, 0), (0, 0))
    return k_c, v_c, k_s_blk, v_s_blk, jnp.pad(k_w, pad), jnp.pad(v_w, pad)


def nsa_mixer(xn, w_in, q_norm, w_out, k_c, v_c, k_s_blk, v_s_blk, k_w_pad, v_w_pad):
    B, S, _ = xn.shape
    G, HPG, HD, QB = NSA_KV_GROUPS, NSA_HPG, NSA_HEAD_DIM, NSA_QBLOCK
    n_qb = S // QB
    n_cmp = k_c.shape[2]
    n_slc = k_s_blk.shape[2]
    n_sel = min(N_SEL, n_slc)
    proj = xn @ w_in
    q = rms_norm(proj[..., :NSA_WIDTH].reshape(B, S, G, HPG, HD), q_norm)
    gate = jax.nn.sigmoid(proj[..., NSA_WIDTH:]).reshape(B, S, G, HPG, NSA_BRANCHES)
    q_blk = q.reshape(B, n_qb, QB, G, HPG, HD).transpose(1, 0, 3, 4, 2, 5)
    gate_blk = gate.reshape(B, n_qb, QB, G, HPG, NSA_BRANCHES).transpose(1, 0, 3, 4, 2, 5)
    slopes = alibi_slopes().reshape(G, HPG)[:, :, None, None]
    scale = HD ** -0.5
    c_pos = jnp.arange(n_cmp) * CMP_STRIDE + CMP_BLOCK - 1
    c_start = c_pos - (CMP_BLOCK - 1)
    s_start = jnp.arange(n_slc) * SLC_BLOCK
    overlap = ((c_start[:, None] < s_start[None, :] + SLC_BLOCK)
               & (c_pos[:, None] >= s_start[None, :])).astype(jnp.float32)
    bi = jnp.arange(B)[:, None, None, None]
    gi = jnp.arange(G)[None, :, None, None]
    j_blk = jnp.arange(n_slc)

    def block(args):
        qi, qb, gb = args
        t = qi * QB + jnp.arange(QB)
        dist_c = t[:, None] - c_pos[None, :]
        s_c = jnp.einsum('bghqd,bgnd->bghqn', qb, k_c).astype(jnp.float32) * scale - slopes * dist_c
        p_c = masked_softmax(s_c, dist_c >= 0)
        o_c = jnp.einsum('bghqn,bgnd->bghqd', p_c.astype(v_c.dtype), v_c)
        imp = jnp.einsum('bghqn,nj->bgqj', p_c, overlap)
        cur = (t // SLC_BLOCK)[:, None]
        forced = (j_blk == 0) | (j_blk == cur) | (j_blk == cur - 1)
        score = jnp.where(s_start[None, :] <= t[:, None], imp + FORCE_BONUS * forced, -jnp.inf)
        _, sel = lax.top_k(score, n_sel)
        k_sel = k_s_blk[bi, gi, sel].reshape(B, G, QB, n_sel * SLC_BLOCK, HD)
        v_sel = v_s_blk[bi, gi, sel].reshape(B, G, QB, n_sel * SLC_BLOCK, HD)
        key_pos = (sel[..., None] * SLC_BLOCK + jnp.arange(SLC_BLOCK)).reshape(B, G, QB, n_sel * SLC_BLOCK)
        dist_s = (t[:, None] - key_pos)[:, :, None]
        s_s = jnp.einsum('bghqd,bgqkd->bghqk', qb, k_sel).astype(jnp.float32) * scale - slopes * dist_s
        p_s = masked_softmax(s_s, dist_s >= 0)
        o_s = jnp.einsum('bghqk,bgqkd->bghqd', p_s.astype(v_sel.dtype), v_sel)
        k_win = lax.dynamic_slice_in_dim(k_w_pad, qi * QB, WINDOW + QB, axis=2)
        v_win = lax.dynamic_slice_in_dim(v_w_pad, qi * QB, WINDOW + QB, axis=2)
        w_pos = qi * QB - WINDOW + jnp.arange(WINDOW + QB)
        dist_w = t[:, None] - w_pos[None, :]
        mask_w = (dist_w >= 0) & (dist_w < WINDOW) & (w_pos[None, :] >= 0)
        s_w = jnp.einsum('bghqd,bgkd->bghqk', qb, k_win).astype(jnp.float32) * scale - slopes * dist_w
        p_w = masked_softmax(s_w, mask_w)
        o_w = jnp.einsum('bghqk,bgkd->bghqd', p_w.astype(v_win.dtype), v_win)
        return gb[..., 0:1] * o_c + gb[..., 1:2] * o_s + gb[..., 2:3] * o_w

    out = lax.map(block, (jnp.arange(n_qb), q_blk, gate_blk))
    o = out.transpose(1, 0, 4, 2, 3, 5).reshape(B, S, NSA_WIDTH)
    return o @ w_out


def peer(xn, w_q, sub_keys, u_tab, v_tab):
    B, S, D = xn.shape
    xt = xn.reshape((B * S) // PEER_CHUNK, PEER_CHUNK, D)

    def chunk(xc):
        C = xc.shape[0]
        q = (xc @ w_q).reshape(C, PEER_HEADS, 2, PEER_KEY_DIM)
        s = jnp.einsum('chpd,hpnd->chpn', q, sub_keys).astype(jnp.float32)
        val, idx = lax.top_k(s, PEER_TOPK)
        cand = (val[:, :, 0, :, None] + val[:, :, 1, None, :]).reshape(C, PEER_HEADS, PEER_TOPK * PEER_TOPK)
        cand_idx = (idx[:, :, 0, :, None] * PEER_NKEYS + idx[:, :, 1, None, :]).reshape(C, PEER_HEADS, PEER_TOPK * PEER_TOPK)
        top_s, top_pos = lax.top_k(cand, PEER_TOPK)
        e = jnp.take_along_axis(cand_idx, top_pos, axis=-1)
        g = jax.nn.softmax(top_s, axis=-1)
        hid = jax.nn.gelu(jnp.einsum('cd,chkd->chk', xc, u_tab[e]))
        w = (g * hid.astype(jnp.float32)).astype(xc.dtype)
        return jnp.einsum('chk,chkd->cd', w, v_tab[e])

    return lax.map(chunk, xt).reshape(B, S, D)


def setup_inputs(seed: int = 0) -> dict:
    key = jax.random.key(seed)
    ks = jax.random.split(key, 24)
    D = D_MODEL
    nA, nB = N_A_LAYERS, N_B_LAYERS

    def nrm(k, shape, scale):
        return scale * jax.random.normal(k, shape, jnp.float32)

    def gain(k, shape):
        return 1.0 + 0.05 * jax.random.normal(k, shape, jnp.float32)

    return {
        'x': nrm(ks[0], (BATCH, SEQ, D), 1.0),
        'a_norm': gain(ks[1], (nA, D)),
        'a_w_in': nrm(ks[2], (nA, D, 2 * GMLP_HALF), D ** -0.5),
        'a_v_norm': gain(ks[3], (nA, GMLP_HALF)),
        'a_w_s': nrm(ks[4], (nA, GMLP_GROUPS, GMLP_CHUNK, GMLP_CHUNK), GMLP_CHUNK ** -0.5),
        'a_b_s': 1.0 + nrm(ks[5], (nA, GMLP_GROUPS, GMLP_CHUNK), 0.1),
        'a_w_out': nrm(ks[6], (nA, GMLP_HALF, D), GMLP_HALF ** -0.5),
        'kv_norm': gain(ks[7], (D,)),
        'kv_w': nrm(ks[8], (D, 6 * NSA_KV_GROUPS * NSA_HEAD_DIM), D ** -0.5),
        'kv_k_norm': gain(ks[9], (NSA_BRANCHES, NSA_HEAD_DIM)),
        'phi_pe': nrm(ks[10], (2, CMP_BLOCK, NSA_HEAD_DIM), 0.1),
        'phi_w1': nrm(ks[11], (2, CMP_BLOCK * NSA_HEAD_DIM, CMP_HIDDEN), (CMP_BLOCK * NSA_HEAD_DIM) ** -0.5),
        'phi_b1': nrm(ks[12], (2, CMP_HIDDEN), 0.01),
        'phi_w2': nrm(ks[13], (2, CMP_HIDDEN, NSA_HEAD_DIM), CMP_HIDDEN ** -0.5),
        'phi_b2': nrm(ks[14], (2, NSA_HEAD_DIM), 0.01),
        'b_norm': gain(ks[15], (nB, D)),
        'b_w_in': nrm(ks[16], (nB, D, NSA_WIDTH + NSA_HEADS * NSA_BRANCHES), D ** -0.5),
        'b_q_norm': gain(ks[17], (nB, NSA_HEAD_DIM)),
        'b_w_out': nrm(ks[18], (nB, NSA_WIDTH, D), NSA_WIDTH ** -0.5),
        'ffn_norm': gain(ks[19], (DEPTH, D)),
        'peer_w_q': nrm(ks[20], (DEPTH, D, PEER_HEADS * 2 * PEER_KEY_DIM), D ** -0.5),
        'peer_keys': nrm(ks[21], (DEPTH, PEER_HEADS, 2, PEER_NKEYS, PEER_KEY_DIM), PEER_KEY_DIM ** -0.5),
        'peer_u': nrm(ks[22], (DEPTH, PEER_EXPERTS, D), D ** -0.5),
        'peer_v': nrm(ks[23], (DEPTH, PEER_EXPERTS, D), PEER_HEADS ** -0.5),
    }


def reference(x, a_norm, a_w_in, a_v_norm, a_w_s, a_b_s, a_w_out, kv_norm, kv_w, kv_k_norm,
              phi_pe, phi_w1, phi_b1, phi_w2, phi_b2, b_norm, b_w_in, b_q_norm, b_w_out,
              ffn_norm, peer_w_q, peer_keys, peer_u, peer_v):
    h = x
    shared = None
    for layer in range(DEPTH):
        if layer < N_A_LAYERS:
            a = layer
            h = h + chunked_gmlp(rms_norm(h, a_norm[a]), a_w_in[a], a_v_norm[a], a_w_s[a], a_b_s[a], a_w_out[a])
        else:
            if shared is None:
                shared = nsa_shared_kv(h, kv_norm, kv_w, kv_k_norm, phi_pe, phi_w1, phi_b1, phi_w2, phi_b2)
            b = layer - N_A_LAYERS
            h = h + nsa_mixer(rms_norm(h, b_norm[b]), b_w_in[b], b_q_norm[b], b_w_out[b], *shared)
        h = h + peer(rms_norm(h, ffn_norm[layer]), peer_w_q[layer], peer_keys[layer], peer_u[layer], peer_v[layer])
    return h
```

```python
import functools
import math

import jax
import jax.numpy as jnp
from jax import lax
from jax.experimental import pallas as pl
from jax.experimental.pallas import tpu as pltpu

F32 = jnp.float32
BF16 = jnp.bfloat16

GMLP_CHUNK = 128
GMLP_GROUPS = 8
NSA_HEAD_DIM = 128
NSA_KV_GROUPS = 4
NSA_BRANCHES = 3
CMP_BLOCK = 32
CMP_STRIDE = 16
SLC_BLOCK = 64
N_SEL = 16
WINDOW = 512
FORCE_BONUS = 1000.0
PEER_HEADS = 8
PEER_NKEYS = 128
PEER_TOPK = 16
PEER_KEY_DIM = 128
RMS_EPS = 1e-6
MASK_FILL = -1e30
TINY = float(jnp.finfo(jnp.float32).tiny)

V7X_VMEM_LIMIT_BYTES = 56 * 1024 * 1024
LANES = 128
SUBLANES = 8

_NT = (((1,), (1,)), ((), ()))


def _cparams(sem):
    return pltpu.CompilerParams(dimension_semantics=sem, vmem_limit_bytes=V7X_VMEM_LIMIT_BYTES)


def _gelu(x):
    c = math.sqrt(2.0 / math.pi)
    return x * (0.5 * (1.0 + jnp.tanh(c * (x + 0.044715 * (x * x * x)))))


def _rms_rows(x, gain):
    r = lax.rsqrt(jnp.mean(x * x, axis=-1, keepdims=True) + RMS_EPS)
    return (x * r) * gain


def _gmlp_in_kernel(h_ref, g_ref, w_ref, z_ref, ss_ref, xn_ref, *, n_u_tiles):
    j = pl.program_id(1)

    @pl.when(j == 0)
    def _():
        xn_ref[...] = _rms_rows(h_ref[...], g_ref[...]).astype(BF16)
        ss_ref[...] = jnp.zeros_like(ss_ref)

    z = _gelu(jnp.dot(xn_ref[...], w_ref[...], preferred_element_type=F32))
    z_ref[...] = z.astype(BF16)

    @pl.when(j >= n_u_tiles)
    def _():
        ss_ref[...] += jnp.sum(z * z, axis=-1, keepdims=True)


def _gmlp_out_kernel(u_ref, v_ref, ss_ref, vg_ref, ws_ref, bs_ref, wo_ref, h_ref, o_ref, prod_ref, *,
                     n_chunks, half):
    j = pl.program_id(1)
    gdim = half // GMLP_GROUPS

    @pl.when(j == 0)
    def _():
        row = lax.broadcasted_iota(jnp.int32, (GMLP_CHUNK, GMLP_CHUNK), 0)
        col = lax.broadcasted_iota(jnp.int32, (GMLP_CHUNK, GMLP_CHUNK), 1)
        tril = row >= col
        for c in range(n_chunks):
            rows = slice(c * GMLP_CHUNK, (c + 1) * GMLP_CHUNK)
            rinv = lax.rsqrt(ss_ref[c] * (1.0 / half) + RMS_EPS)
            for g in range(GMLP_GROUPS):
                cols = slice(g * gdim, (g + 1) * gdim)
                wc = (jnp.where(tril, ws_ref[g], 0.0) * rinv).astype(BF16)
                sv = jnp.dot(wc, v_ref[rows, cols], preferred_element_type=F32)
                sv = sv * vg_ref[:, cols] + bs_ref[:, g:g + 1]
                prod_ref[rows, cols] = (u_ref[rows, cols].astype(F32) * sv).astype(BF16)

    o_ref[...] = jnp.dot(prod_ref[...], wo_ref[...], preferred_element_type=F32) + h_ref[...]


def _gmlp_layer(h, norm_g, w_in, v_norm, w_s, b_s, w_out):
    T, D = h.shape
    half = w_out.shape[0]
    tm, tn = 512, 1024
    n_u = half // tn
    z, ss = pl.pallas_call(
        functools.partial(_gmlp_in_kernel, n_u_tiles=n_u),
        grid=(T // tm, 2 * half // tn),
        in_specs=[pl.BlockSpec((tm, D), lambda i, j: (i, 0)),
                  pl.BlockSpec((1, D), lambda i, j: (0, 0)),
                  pl.BlockSpec((D, tn), lambda i, j: (0, j))],
        out_specs=[pl.BlockSpec((tm, tn), lambda i, j: (i, j)),
                   pl.BlockSpec((tm, 1), lambda i, j: (i, 0))],
        out_shape=[jax.ShapeDtypeStruct((T, 2 * half), BF16), jax.ShapeDtypeStruct((T, 1), F32)],
        scratch_shapes=[pltpu.VMEM((tm, D), BF16)],
        compiler_params=_cparams(("parallel", "arbitrary")),
        name="gmlp_in",
    )(h, norm_g.reshape(1, D), w_in.astype(BF16))

    tn2 = 512
    n_chunks = tm // GMLP_CHUNK
    return pl.pallas_call(
        functools.partial(_gmlp_out_kernel, n_chunks=n_chunks, half=half),
        grid=(T // tm, D // tn2),
        in_specs=[pl.BlockSpec((tm, half), lambda i, j: (i, 0)),
                  pl.BlockSpec((tm, half), lambda i, j: (i, 1)),
                  pl.BlockSpec((n_chunks, 1, GMLP_CHUNK), lambda i, j: (i, 0, 0)),
                  pl.BlockSpec((1, half), lambda i, j: (0, 0)),
                  pl.BlockSpec((GMLP_GROUPS, GMLP_CHUNK, GMLP_CHUNK), lambda i, j: (0, 0, 0)),
                  pl.BlockSpec((GMLP_CHUNK, GMLP_GROUPS), lambda i, j: (0, 0)),
                  pl.BlockSpec((half, tn2), lambda i, j: (0, j)),
                  pl.BlockSpec((tm, tn2), lambda i, j: (i, j))],
        out_specs=pl.BlockSpec((tm, tn2), lambda i, j: (i, j)),
        out_shape=jax.ShapeDtypeStruct((T, D), F32),
        scratch_shapes=[pltpu.VMEM((tm, half), BF16)],
        compiler_params=_cparams(("parallel", "arbitrary")),
        name="gmlp_out",
    )(z, z, ss.reshape(T // GMLP_CHUNK, 1, GMLP_CHUNK), v_norm.reshape(1, half), w_s, b_s.T,
      w_out.astype(BF16), h)


def _cmpx(v, i, l, desc):
    hi, lo = jnp.maximum(v[i], v[l]), jnp.minimum(v[i], v[l])
    v[i], v[l] = (hi, lo) if desc else (lo, hi)


def _bitonic_sort_desc(v):
    n = len(v)
    k = 2
    while k <= n:
        j = k // 2
        while j >= 1:
            for i in range(n):
                l = i ^ j
                if l > i:
                    _cmpx(v, i, l, (i & k) == 0)
            j //= 2
        k *= 2


def _bitonic_merge_desc(v):
    n = len(v)
    j = n // 2
    while j >= 1:
        for i in range(n):
            l = i ^ j
            if l > i:
                _cmpx(v, i, l, True)
        j //= 2


def _top16_sorted(rows):
    v = list(rows)
    _bitonic_sort_desc(v)
    for shift in (4, 2, 1):
        w = [pltpu.roll(x, shift, axis=0) for x in v]
        v = [jnp.maximum(v[i], w[PEER_TOPK - 1 - i]) for i in range(PEER_TOPK)]
        _bitonic_merge_desc(v)
    return v


def _peer_score_kernel(h_ref, g_ref, wq_ref, keys_ref, xn_ref, s_ref, st_ref):
    tm = h_ref.shape[0]
    xn = _rms_rows(h_ref[...], g_ref[...]).astype(BF16)
    xn_ref[...] = xn
    q = jnp.dot(xn, wq_ref[...], preferred_element_type=F32).astype(BF16)
    sub = lax.broadcasted_iota(jnp.int32, (SUBLANES, tm), 0)

    def pack(lst):
        out = lst[0]
        for s in range(1, SUBLANES):
            out = jnp.where(sub == s, lst[s], out)
        return out

    for h in range(PEER_HEADS):
        tops = []
        for p in range(2):
            hp = 2 * h + p
            qs = q[:, hp * PEER_KEY_DIM:(hp + 1) * PEER_KEY_DIM]
            s_t = lax.dot_general(keys_ref[hp], qs, _NT, preferred_element_type=F32)
            s_ref[hp] = s_t
            tops.append(_top16_sorted([s_t[SUBLANES * r:SUBLANES * (r + 1), :] for r in range(16)]))
        t1, t2 = tops
        t2lo, t2hi, t1hi = pack(t2[0:8]), pack(t2[8:16]), pack(t1[8:16])
        cands = [t1[0] + t2lo, t1[0] + t2hi] + [t1[a] + t2lo for a in range(1, 8)] + [t1hi + t2[0]]
        cands += [jnp.full((SUBLANES, tm), -jnp.inf, F32)] * (16 - len(cands))
        top = _top16_sorted(cands)
        z = jnp.ones((SUBLANES, tm), F32)
        for k in range(1, PEER_TOPK):
            z = z + jnp.exp(top[k] - top[0])
        for k, val in enumerate((top[PEER_TOPK - 1], t1[0], t2[0], z)):
            st_ref[pl.ds(k * PEER_HEADS + h, 1), :] = val[0:1, :]


def _peer_main_kernel(xn_ref, u_ref, vt_ref, s_ref, st_ref, h_ref, o_ref, acc_ref, e_ref, *, rows_per_tile):
    j = pl.program_id(1)
    tm = xn_ref.shape[0]

    @pl.when(j == 0)
    def _():
        acc_ref[...] = jnp.zeros_like(acc_ref)
        for h in range(PEER_HEADS):
            m1 = st_ref[PEER_HEADS + h:PEER_HEADS + h + 1, :]
            m2 = st_ref[2 * PEER_HEADS + h:2 * PEER_HEADS + h + 1, :]
            zz = st_ref[3 * PEER_HEADS + h:3 * PEER_HEADS + h + 1, :]
            e_ref[2 * h] = jnp.exp(s_ref[2 * h] - m1)
            e_ref[2 * h + 1] = jnp.exp(s_ref[2 * h + 1] - m2) / zz

    act = _gelu(lax.dot_general(u_ref[...], xn_ref[...], _NT, preferred_element_type=F32))
    ws = []
    for r in range(rows_per_tile):
        i = j * rows_per_tile + r
        gate = jnp.zeros((PEER_NKEYS, tm), F32)
        for h in range(PEER_HEADS):
            thr = st_ref[h:h + 1, :]
            ssum = s_ref[2 * h, pl.ds(i, 1), :] + s_ref[2 * h + 1]
            val = e_ref[2 * h, pl.ds(i, 1), :] * e_ref[2 * h + 1]
            gate = gate + jnp.where(ssum >= thr, val, 0.0)
        ws.append((act[r * PEER_NKEYS:(r + 1) * PEER_NKEYS, :] * gate).astype(BF16))
    w_t = jnp.concatenate(ws, axis=0)
    acc_ref[...] += jnp.dot(vt_ref[...], w_t, preferred_element_type=F32)

    @pl.when(j == pl.num_programs(1) - 1)
    def _():
        o_ref[...] = acc_ref[...].T + h_ref[...]


def _peer_layer(h, norm_g, w_q, keys, u_tab, v_tab):
    T, D = h.shape
    n_exp = u_tab.shape[0]
    n_hp = 2 * PEER_HEADS
    tm1 = 256
    xn, s_t, stats = pl.pallas_call(
        _peer_score_kernel,
        grid=(T // tm1,),
        in_specs=[pl.BlockSpec((tm1, D), lambda i: (i, 0)),
                  pl.BlockSpec((1, D), lambda i: (0, 0)),
                  pl.BlockSpec((D, n_hp * PEER_KEY_DIM), lambda i: (0, 0)),
                  pl.BlockSpec((n_hp, PEER_NKEYS, PEER_KEY_DIM), lambda i: (0, 0, 0))],
        out_specs=[pl.BlockSpec((tm1, D), lambda i: (i, 0)),
                   pl.BlockSpec((n_hp, PEER_NKEYS, tm1), lambda i: (0, 0, i)),
                   pl.BlockSpec((4 * PEER_HEADS, tm1), lambda i: (0, i))],
        out_shape=[jax.ShapeDtypeStruct((T, D), BF16),
                   jax.ShapeDtypeStruct((n_hp, PEER_NKEYS, T), F32),
                   jax.ShapeDtypeStruct((4 * PEER_HEADS, T), F32)],
        compiler_params=_cparams(("parallel",)),
        name="peer_score",
    )(h, norm_g.reshape(1, D), w_q.astype(BF16),
      keys.reshape(n_hp, PEER_NKEYS, PEER_KEY_DIM).astype(BF16))

    tm, te = 512, 512
    return pl.pallas_call(
        functools.partial(_peer_main_kernel, rows_per_tile=te // PEER_NKEYS),
        grid=(T // tm, n_exp // te),
        in_specs=[pl.BlockSpec((tm, D), lambda i, j: (i, 0)),
                  pl.BlockSpec((te, D), lambda i, j: (j, 0)),
                  pl.BlockSpec((D, te), lambda i, j: (0, j)),
                  pl.BlockSpec((n_hp, PEER_NKEYS, tm), lambda i, j: (0, 0, i)),
                  pl.BlockSpec((4 * PEER_HEADS, tm), lambda i, j: (0, i)),
                  pl.BlockSpec((tm, D), lambda i, j: (i, 0))],
        out_specs=pl.BlockSpec((tm, D), lambda i, j: (i, 0)),
        out_shape=jax.ShapeDtypeStruct((T, D), F32),
        scratch_shapes=[pltpu.VMEM((D, tm), F32), pltpu.VMEM((n_hp, PEER_NKEYS, tm), F32)],
        compiler_params=_cparams(("parallel", "arbitrary")),
        name="peer_main",
    )(xn, u_tab.astype(BF16), v_tab.T.astype(BF16), s_t, stats, h)


def _kv_proj_kernel(h_ref, g_ref, wk_ref, wvt_ref, kg_ref, raw_ref, kn_ref, vt_ref):
    G, HD = NSA_KV_GROUPS, NSA_HEAD_DIM
    xn = _rms_rows(h_ref[...], g_ref[...]).astype(BF16)
    y = jnp.dot(xn, wk_ref[...], preferred_element_type=F32)
    for c in range(2 * G):
        raw_ref[c] = y[:, c * HD:(c + 1) * HD]
    for c in range(2 * G):
        gain = kg_ref[1:2, :] if c < G else kg_ref[2:3, :]
        kn_ref[c] = _rms_rows(y[:, (2 * G + c) * HD:(2 * G + c + 1) * HD], gain).astype(BF16)
    y_t = lax.dot_general(wvt_ref[...], xn, _NT, preferred_element_type=F32)
    for c in range(2 * G):
        vt_ref[c, 0] = y_t[c * HD:(c + 1) * HD, :].astype(BF16)


def _compress_kernel(x_ref, pe_ref, w1_ref, b1_ref, w2_ref, b2_ref, gain_ref, o_ref, *, is_key):
    x = x_ref[0]
    n, half_k = x.shape
    xa = (x + pe_ref[0, 0:1, :]).astype(BF16)
    xb = (x + pe_ref[0, 1:2, :]).astype(BF16)
    a = jnp.dot(xa, w1_ref[0, :half_k, :], preferred_element_type=F32)
    b = jnp.dot(xb, w1_ref[0, half_k:, :], preferred_element_type=F32)
    hid = _gelu(a + pltpu.roll(b, n - 1, axis=0) + b1_ref[0])
    y = jnp.dot(hid.astype(BF16), w2_ref[0], preferred_element_type=F32) + b2_ref[0]
    if is_key:
        o_ref[0] = _rms_rows(y, gain_ref[...]).astype(BF16)
    else:
        o_ref[0] = y.T.astype(BF16)


def _nsa_shared_kv(h, B, S, kv_norm, kv_w, kv_k_norm, phi_pe, phi_w1, phi_b1, phi_w2, phi_b2, tk):
    T, D = h.shape
    G, HD = NSA_KV_GROUPS, NSA_HEAD_DIM
    GW = G * HD
    w = kv_w.astype(BF16)
    sl = lambda six: w[:, six * GW:(six + 1) * GW]
    wk = jnp.concatenate([sl(0), sl(1), sl(2), sl(4)], axis=1)
    wvt = jnp.concatenate([sl(3), sl(5)], axis=1).T
    raw, kn, vt = pl.pallas_call(
        _kv_proj_kernel,
        grid=(T // tk,),
        in_specs=[pl.BlockSpec((tk, D), lambda i: (i, 0)),
                  pl.BlockSpec((1, D), lambda i: (0, 0)),
                  pl.BlockSpec((D, 4 * GW), lambda i: (0, 0)),
                  pl.BlockSpec((2 * GW, D), lambda i: (0, 0)),
                  pl.BlockSpec((NSA_BRANCHES, HD), lambda i: (0, 0))],
        out_specs=[pl.BlockSpec((2 * G, tk, HD), lambda i: (0, i, 0)),
                   pl.BlockSpec((2 * G, tk, HD), lambda i: (0, i, 0)),
                   pl.BlockSpec((2 * G, 1, HD, tk), lambda i: (0, i, 0, 0))],
        out_shape=[jax.ShapeDtypeStruct((2 * G, T, HD), F32),
                   jax.ShapeDtypeStruct((2 * G, T, HD), BF16),
                   jax.ShapeDtypeStruct((2 * G, T // tk, HD, tk), BF16)],
        compiler_params=_cparams(("parallel",)),
        name="nsa_kv_proj",
    )(h, kv_norm.reshape(1, D), wk, wvt, kv_k_norm)

    ncb = S // CMP_STRIDE
    half_k = CMP_STRIDE * HD
    hidden = phi_w1.shape[-1]
    raw16 = raw.reshape(2 * G, T // CMP_STRIDE, half_k)
    pe = phi_pe.reshape(2, CMP_BLOCK // CMP_STRIDE, half_k)
    outs = []
    for kind in range(2):
        is_key = kind == 0
        o_block = (1, ncb, HD) if is_key else (1, HD, ncb)
        outs.append(pl.pallas_call(
            functools.partial(_compress_kernel, is_key=is_key),
            grid=(B, G),
            in_specs=[pl.BlockSpec((1, ncb, half_k), lambda b, g, kind=kind: (kind * G + g, b, 0)),
                      pl.BlockSpec((1, 2, half_k), lambda b, g, kind=kind: (kind, 0, 0)),
                      pl.BlockSpec((1, 2 * half_k, hidden), lambda b, g, kind=kind: (kind, 0, 0)),
                      pl.BlockSpec((1, 1, hidden), lambda b, g, kind=kind: (kind, 0, 0)),
                      pl.BlockSpec((1, hidden, HD), lambda b, g, kind=kind: (kind, 0, 0)),
                      pl.BlockSpec((1, 1, HD), lambda b, g, kind=kind: (kind, 0, 0)),
                      pl.BlockSpec((1, HD), lambda b, g: (0, 0))],
            out_specs=pl.BlockSpec(o_block, lambda b, g: (b * G + g, 0, 0)),
            out_shape=jax.ShapeDtypeStruct((B * G,) + o_block[1:], BF16),
            compiler_params=_cparams(("parallel", "parallel")),
            name="nsa_compress_k" if is_key else "nsa_compress_v",
        )(raw16, pe, phi_w1.astype(BF16), phi_b1.reshape(2, 1, hidden), phi_w2.astype(BF16),
          phi_b2.reshape(2, 1, HD), kv_k_norm[0:1]))
    return outs[0], outs[1], kn, vt


def _nsa_in_kernel(h_ref, g_ref, wq_ref, wgt_ref, qg_ref, q_ref, gate_ref):
    HD = NSA_HEAD_DIM
    xn = _rms_rows(h_ref[...], g_ref[...]).astype(BF16)
    y = jnp.dot(xn, wq_ref[...], preferred_element_type=F32)
    for hd in range(q_ref.shape[0]):
        q_ref[hd] = _rms_rows(y[:, hd * HD:(hd + 1) * HD], qg_ref[...]).astype(BF16)
    g_t = lax.dot_general(wgt_ref[...], xn, _NT, preferred_element_type=F32)
    gate_ref[...] = jax.nn.sigmoid(g_t)


def _nsa_attn_kernel(slopes_ref, q_ref, gate_ref, kc_ref, vct_ref, ks_ref, vst_ref, kw_ref, vwt_ref, ovt_ref,
                     o_ref, sel_ref, m_ref, l_ref, acc_ref, oc_ref, os_ref, *, tq, hpg):
    HD = NSA_HEAD_DIM
    tk = tq
    g = pl.program_id(1)
    qi = pl.program_id(2)
    t0 = qi * tq
    ncb = kc_ref.shape[1]
    nsb = sel_ref.shape[0]
    scale = HD ** -0.5
    slopes = [slopes_ref[g * hpg + hh] for hh in range(hpg)]

    tpos_c = lax.broadcasted_iota(jnp.int32, (ncb, tq), 1) + t0
    cpos = lax.broadcasted_iota(jnp.int32, (ncb, tq), 0) * CMP_STRIDE + (CMP_BLOCK - 1)
    dist_c = tpos_c - cpos
    mask_c = dist_c >= 0
    dist_cf = dist_c.astype(F32)
    p_sum = jnp.zeros((ncb, tq), F32)
    for hh in range(hpg):
        s = lax.dot_general(kc_ref[0], q_ref[hh], _NT, preferred_element_type=F32) * scale - slopes[hh] * dist_cf
        s = jnp.where(mask_c, s, MASK_FILL)
        e = jnp.where(mask_c, jnp.exp(s - jnp.max(s, axis=0, keepdims=True)), 0.0)
        p = e / jnp.maximum(jnp.sum(e, axis=0, keepdims=True), TINY)
        oc_ref[hh] = jnp.dot(vct_ref[0], p.astype(BF16), preferred_element_type=F32)
        p_sum = p_sum + p

    imp = jnp.zeros((nsb, tq), F32)
    rem = p_sum
    for _ in range(3):
        part = rem.astype(BF16)
        imp = imp + jnp.dot(ovt_ref[...], part, preferred_element_type=F32)
        rem = rem - part.astype(F32)
    jb = lax.broadcasted_iota(jnp.int32, (nsb, tq), 0)
    cur = (lax.broadcasted_iota(jnp.int32, (nsb, tq), 1) + t0) // SLC_BLOCK
    forced = jnp.where(jb == 0, 1.0, jnp.where(jb == cur, 1.0, jnp.where(jb == cur - 1, 1.0, 0.0)))
    score = jnp.where(jb <= cur, imp + FORCE_BONUS * forced, -jnp.inf)
    sel_ref[...] = score

    def rank_body(jp, cnt):
        row = sel_ref[pl.ds(jp, 1), :]
        tie = jnp.where(jb > jp, 1.0, 0.0)
        return cnt + jnp.where(row > score, 1.0, jnp.where(row == score, tie, 0.0))

    n_valid = (t0 + tq) // SLC_BLOCK
    cnt = lax.fori_loop(0, n_valid, rank_body, jnp.zeros((nsb, tq), F32))
    sel_ref[...] = jnp.where(cnt < float(N_SEL), 1.0, 0.0)

    kiota = lax.broadcasted_iota(jnp.int32, (tk, tq), 0)
    tpos = lax.broadcasted_iota(jnp.int32, (tk, tq), 1) + t0

    def flash(k_ref, vt_ref, lo, hi, mask_fn, out_ref):
        m_ref[...] = jnp.full(m_ref.shape, MASK_FILL, F32)
        l_ref[...] = jnp.zeros_like(l_ref)
        acc_ref[...] = jnp.zeros_like(acc_ref)

        def body(ki, carry):
            k0 = pl.multiple_of(ki * tk, tk)
            k = k_ref[0, pl.ds(k0, tk), :]
            v_t = vt_ref[0, ki]
            dist = tpos - (kiota + k0)
            mask = mask_fn(ki, dist)
            dist_f = dist.astype(F32)
            for hh in range(hpg):
                s = lax.dot_general(k, q_ref[hh], _NT, preferred_element_type=F32) * scale - slopes[hh] * dist_f
                s = jnp.where(mask, s, MASK_FILL)
                m_old = m_ref[hh]
                m_new = jnp.maximum(m_old, jnp.max(s, axis=0, keepdims=True))
                alpha = jnp.exp(m_old - m_new)
                p = jnp.where(mask, jnp.exp(s - m_new), 0.0)
                l_ref[hh] = alpha * l_ref[hh] + jnp.sum(p, axis=0, keepdims=True)
                acc_ref[hh] = alpha * acc_ref[hh] + jnp.dot(v_t, p.astype(BF16), preferred_element_type=F32)
                m_ref[hh] = m_new
            return carry

        lax.fori_loop(lo, hi, body, 0)
        for hh in range(hpg):
            out_ref[hh] = acc_ref[hh] / jnp.maximum(l_ref[hh], TINY)

    bpt = tk // SLC_BLOCK

    def sel_mask(ki, dist):
        rows = [jnp.broadcast_to(sel_ref[pl.ds(ki * bpt + r, 1), :], (SLC_BLOCK, tq)) for r in range(bpt)]
        return jnp.where(dist >= 0, jnp.concatenate(rows, axis=0), 0.0) > 0.5

    def win_mask(ki, dist):
        return jnp.where(dist >= 0, dist, WINDOW) < WINDOW

    flash(ks_ref, vst_ref, 0, qi + 1, sel_mask, os_ref)
    flash(kw_ref, vwt_ref, jnp.maximum(qi - WINDOW // tk, 0), qi + 1, win_mask, acc_ref)

    for hh in range(hpg):
        gates = [gate_ref[0, hh * NSA_BRANCHES + br:hh * NSA_BRANCHES + br + 1, :] for br in range(NSA_BRANCHES)]
        o_t = gates[0] * oc_ref[hh] + gates[1] * os_ref[hh] + gates[2] * acc_ref[hh]
        o_ref[:, hh * HD:(hh + 1) * HD] = o_t.T.astype(BF16)


def _mm_res_kernel(a_ref, w_ref, h_ref, o_ref):
    o_ref[...] = jnp.dot(a_ref[...], w_ref[...], preferred_element_type=F32) + h_ref[...]


def _nsa_layer(h, B, S, norm_g, w_in, q_norm, w_out, k_c, v_ct, kn, vt, tq):
    T, D = h.shape
    G, HD = NSA_KV_GROUPS, NSA_HEAD_DIM
    width = w_out.shape[0]
    heads = width // HD
    hpg = heads // G
    n_gate = heads * NSA_BRANCHES
    w = w_in.astype(BF16)
    tm = 512
    q, gate_t = pl.pallas_call(
        _nsa_in_kernel,
        grid=(T // tm,),
        in_specs=[pl.BlockSpec((tm, D), lambda i: (i, 0)),
                  pl.BlockSpec((1, D), lambda i: (0, 0)),
                  pl.BlockSpec((D, width), lambda i: (0, 0)),
                  pl.BlockSpec((n_gate, D), lambda i: (0, 0)),
                  pl.BlockSpec((1, HD), lambda i: (0, 0))],
        out_specs=[pl.BlockSpec((heads, tm, HD), lambda i: (0, i, 0)),
                   pl.BlockSpec((n_gate, tm), lambda i: (0, i))],
        out_shape=[jax.ShapeDtypeStruct((heads, T, HD), BF16), jax.ShapeDtypeStruct((n_gate, T), F32)],
        compiler_params=_cparams(("parallel",)),
        name="nsa_in",
    )(h, norm_g.reshape(1, D), w[:, :width], w[:, width:].T, q_norm.reshape(1, HD))

    ncb = S // CMP_STRIDE
    nsb = S // SLC_BLOCK
    n_cmp = (S - CMP_BLOCK) // CMP_STRIDE + 1
    c_start = jnp.arange(ncb) * CMP_STRIDE
    c_pos = c_start + CMP_BLOCK - 1
    s_start = jnp.arange(nsb) * SLC_BLOCK
    ov_t = ((c_start[None, :] < s_start[:, None] + SLC_BLOCK) & (c_pos[None, :] >= s_start[:, None])
            & (jnp.arange(ncb)[None, :] < n_cmp)).astype(BF16)
    h_idx = jnp.arange(1, heads + 1, dtype=F32)
    slopes = 2.0 ** (-8.0 * h_idx / heads)
    nq = S // tq
    o = pl.pallas_call(
        functools.partial(_nsa_attn_kernel, tq=tq, hpg=hpg),
        grid_spec=pltpu.PrefetchScalarGridSpec(
            num_scalar_prefetch=1,
            grid=(B, G, nq),
            in_specs=[pl.BlockSpec((hpg, tq, HD), lambda b, g, qi, sl: (g, b * nq + qi, 0)),
                      pl.BlockSpec((1, hpg * NSA_BRANCHES, tq), lambda b, g, qi, sl: (g, 0, b * nq + qi)),
                      pl.BlockSpec((1, ncb, HD), lambda b, g, qi, sl: (b * G + g, 0, 0)),
                      pl.BlockSpec((1, HD, ncb), lambda b, g, qi, sl: (b * G + g, 0, 0)),
                      pl.BlockSpec((1, S, HD), lambda b, g, qi, sl: (g, b, 0)),
                      pl.BlockSpec((1, nq, HD, tq), lambda b, g, qi, sl: (g, b, 0, 0)),
                      pl.BlockSpec((1, S, HD), lambda b, g, qi, sl: (G + g, b, 0)),
                      pl.BlockSpec((1, nq, HD, tq), lambda b, g, qi, sl: (G + g, b, 0, 0)),
                      pl.BlockSpec((nsb, ncb), lambda b, g, qi, sl: (0, 0))],
            out_specs=pl.BlockSpec((tq, hpg * HD), lambda b, g, qi, sl: (b * nq + qi, g)),
            scratch_shapes=[pltpu.VMEM((nsb, tq), F32),
                            pltpu.VMEM((hpg, 1, tq), F32),
                            pltpu.VMEM((hpg, 1, tq), F32),
                            pltpu.VMEM((hpg, HD, tq), F32),
                            pltpu.VMEM((hpg, HD, tq), F32),
                            pltpu.VMEM((hpg, HD, tq), F32)]),
        out_shape=jax.ShapeDtypeStruct((T, width), BF16),
        compiler_params=_cparams(("parallel", "parallel", "arbitrary")),
        name="nsa_attn",
    )(slopes, q, gate_t.reshape(G, hpg * NSA_BRANCHES, T), k_c, v_ct, kn, vt, kn, vt, ov_t)

    return pl.pallas_call(
        _mm_res_kernel,
        grid=(T // tm,),
        in_specs=[pl.BlockSpec((tm, width), lambda i: (i, 0)),
                  pl.BlockSpec((width, D), lambda i: (0, 0)),
                  pl.BlockSpec((tm, D), lambda i: (i, 0))],
        out_specs=pl.BlockSpec((tm, D), lambda i: (i, 0)),
        out_shape=jax.ShapeDtypeStruct((T, D), F32),
        compiler_params=_cparams(("parallel",)),
        name="nsa_out",
    )(o, w_out.astype(BF16), h)


def kernel(x, a_norm, a_w_in, a_v_norm, a_w_s, a_b_s, a_w_out, kv_norm, kv_w, kv_k_norm, phi_pe, phi_w1, phi_b1, phi_w2, phi_b2, b_norm, b_w_in, b_q_norm, b_w_out, ffn_norm, peer_w_q, peer_keys, peer_u, peer_v):
    B, S, D = x.shape
    depth = ffn_norm.shape[0]
    n_a = a_norm.shape[0]
    tq = 256
    h = x.reshape(B * S, D)
    shared = None
    for layer in range(depth):
        if layer < n_a:
            a = layer
            h = _gmlp_layer(h, a_norm[a], a_w_in[a], a_v_norm[a], a_w_s[a], a_b_s[a], a_w_out[a])
        else:
            if shared is None:
                shared = _nsa_shared_kv(h, B, S, kv_norm, kv_w, kv_k_norm, phi_pe, phi_w1, phi_b1, phi_w2,
                                        phi_b2, tq)
            b = layer - n_a
            h = _nsa_layer(h, B, S, b_norm[b], b_w_in[b], b_q_norm[b], b_w_out[b], *shared, tq)
        h = _peer_layer(h, ffn_norm[layer], peer_w_q[layer], peer_keys[layer], peer_u[layer], peer_v[layer])
    return h.reshape(B, S, D)
```

```python
import functools
import math

import jax
import jax.numpy as jnp
from jax import lax
from jax.experimental import pallas as pl
from jax.experimental.pallas import tpu as pltpu

F32 = jnp.float32
BF16 = jnp.bfloat16

GMLP_CHUNK = 128
GMLP_GROUPS = 8
NSA_HEAD_DIM = 128
NSA_KV_GROUPS = 4
NSA_BRANCHES = 3
CMP_BLOCK = 32
CMP_STRIDE = 16
SLC_BLOCK = 64
N_SEL = 16
WINDOW = 512
FORCE_BONUS = 1000.0
PEER_HEADS = 8
PEER_NKEYS = 128
PEER_TOPK = 16
PEER_KEY_DIM = 128
RMS_EPS = 1e-6
MASK_FILL = -1e30
TINY = float(jnp.finfo(jnp.float32).tiny)

V7X_VMEM_LIMIT_BYTES = 56 * 1024 * 1024
LANES = 128
SUBLANES = 8
ALIBI_TERMS = 3
LOG2E = 1.4426950408889634

_NT = (((1,), (1,)), ((), ()))


def _cparams(sem):
    return pltpu.CompilerParams(dimension_semantics=sem, vmem_limit_bytes=V7X_VMEM_LIMIT_BYTES)


def _gelu(x):
    c = math.sqrt(2.0 / math.pi)
    return x * (0.5 * (1.0 + jnp.tanh(c * (x + 0.044715 * (x * x * x)))))


def _rms_rows(x, gain):
    r = lax.rsqrt(jnp.mean(x * x, axis=-1, keepdims=True) + RMS_EPS)
    return (x * r) * gain


def _gmlp_in_kernel(h_ref, g_ref, w_ref, z_ref, ss_ref, xn_ref, *, n_u_tiles):
    j = pl.program_id(1)

    @pl.when(j == 0)
    def _():
        xn_ref[...] = _rms_rows(h_ref[...], g_ref[...]).astype(BF16)
        ss_ref[...] = jnp.zeros_like(ss_ref)

    z = _gelu(jnp.dot(xn_ref[...], w_ref[...], preferred_element_type=F32))
    z_ref[...] = z.astype(BF16)

    @pl.when(j >= n_u_tiles)
    def _():
        ss_ref[...] += jnp.sum(z * z, axis=-1, keepdims=True)


def _gmlp_out_kernel(u_ref, v_ref, ss_ref, vg_ref, ws_ref, bs_ref, wo_ref, h_ref, o_ref, prod_ref, *,
                     n_chunks, half):
    j = pl.program_id(1)
    gdim = half // GMLP_GROUPS

    @pl.when(j == 0)
    def _():
        row = lax.broadcasted_iota(jnp.int32, (GMLP_CHUNK, GMLP_CHUNK), 0)
        col = lax.broadcasted_iota(jnp.int32, (GMLP_CHUNK, GMLP_CHUNK), 1)
        tril = row >= col
        for c in range(n_chunks):
            rows = slice(c * GMLP_CHUNK, (c + 1) * GMLP_CHUNK)
            rinv = lax.rsqrt(ss_ref[c] * (1.0 / half) + RMS_EPS)
            for g in range(GMLP_GROUPS):
                cols = slice(g * gdim, (g + 1) * gdim)
                wc = (jnp.where(tril, ws_ref[g], 0.0) * rinv).astype(BF16)
                sv = jnp.dot(wc, v_ref[rows, cols], preferred_element_type=F32)
                sv = sv * vg_ref[:, cols] + bs_ref[:, g:g + 1]
                prod_ref[rows, cols] = (u_ref[rows, cols].astype(F32) * sv).astype(BF16)

    o_ref[...] = jnp.dot(prod_ref[...], wo_ref[...], preferred_element_type=F32) + h_ref[...]


def _gmlp_layer(h, norm_g, w_in, v_norm, w_s, b_s, w_out):
    T, D = h.shape
    half = w_out.shape[0]
    tm, tn = 512, 1024
    n_u = half // tn
    z, ss = pl.pallas_call(
        functools.partial(_gmlp_in_kernel, n_u_tiles=n_u),
        grid=(T // tm, 2 * half // tn),
        in_specs=[pl.BlockSpec((tm, D), lambda i, j: (i, 0)),
                  pl.BlockSpec((1, D), lambda i, j: (0, 0)),
                  pl.BlockSpec((D, tn), lambda i, j: (0, j))],
        out_specs=[pl.BlockSpec((tm, tn), lambda i, j: (i, j)),
                   pl.BlockSpec((tm, 1), lambda i, j: (i, 0))],
        out_shape=[jax.ShapeDtypeStruct((T, 2 * half), BF16), jax.ShapeDtypeStruct((T, 1), F32)],
        scratch_shapes=[pltpu.VMEM((tm, D), BF16)],
        compiler_params=_cparams(("parallel", "arbitrary")),
        name="gmlp_in",
    )(h, norm_g.reshape(1, D), w_in.astype(BF16))

    tn2 = 512
    n_chunks = tm // GMLP_CHUNK
    return pl.pallas_call(
        functools.partial(_gmlp_out_kernel, n_chunks=n_chunks, half=half),
        grid=(T // tm, D // tn2),
        in_specs=[pl.BlockSpec((tm, half), lambda i, j: (i, 0)),
                  pl.BlockSpec((tm, half), lambda i, j: (i, 1)),
                  pl.BlockSpec((n_chunks, 1, GMLP_CHUNK), lambda i, j: (i, 0, 0)),
                  pl.BlockSpec((1, half), lambda i, j: (0, 0)),
                  pl.BlockSpec((GMLP_GROUPS, GMLP_CHUNK, GMLP_CHUNK), lambda i, j: (0, 0, 0)),
                  pl.BlockSpec((GMLP_CHUNK, GMLP_GROUPS), lambda i, j: (0, 0)),
                  pl.BlockSpec((half, tn2), lambda i, j: (0, j)),
                  pl.BlockSpec((tm, tn2), lambda i, j: (i, j))],
        out_specs=pl.BlockSpec((tm, tn2), lambda i, j: (i, j)),
        out_shape=jax.ShapeDtypeStruct((T, D), F32),
        scratch_shapes=[pltpu.VMEM((tm, half), BF16)],
        compiler_params=_cparams(("parallel", "arbitrary")),
        name="gmlp_out",
    )(z, z, ss.reshape(T // GMLP_CHUNK, 1, GMLP_CHUNK), v_norm.reshape(1, half), w_s, b_s.T,
      w_out.astype(BF16), h)


def _cmpx(v, i, l, desc):
    hi, lo = jnp.maximum(v[i], v[l]), jnp.minimum(v[i], v[l])
    v[i], v[l] = (hi, lo) if desc else (lo, hi)


def _bitonic_sort_desc(v):
    n = len(v)
    k = 2
    while k <= n:
        j = k // 2
        while j >= 1:
            for i in range(n):
                l = i ^ j
                if l > i:
                    _cmpx(v, i, l, (i & k) == 0)
            j //= 2
        k *= 2


def _bitonic_merge_desc(v):
    n = len(v)
    j = n // 2
    while j >= 1:
        for i in range(n):
            l = i ^ j
            if l > i:
                _cmpx(v, i, l, True)
        j //= 2


def _top16_sorted(rows):
    v = list(rows)
    _bitonic_sort_desc(v)
    for shift in (4, 2, 1):
        w = [pltpu.roll(x, shift, axis=0) for x in v]
        v = [jnp.maximum(v[i], w[PEER_TOPK - 1 - i]) for i in range(PEER_TOPK)]
        _bitonic_merge_desc(v)
    return v


def _peer_score_kernel(h_ref, g_ref, wq_ref, keys_ref, xn_ref, s_ref, st_ref):
    tm = h_ref.shape[0]
    xn = _rms_rows(h_ref[...], g_ref[...]).astype(BF16)
    xn_ref[...] = xn
    q = jnp.dot(xn, wq_ref[...], preferred_element_type=F32).astype(BF16)
    sub = lax.broadcasted_iota(jnp.int32, (SUBLANES, tm), 0)

    def pack(lst):
        out = lst[0]
        for s in range(1, SUBLANES):
            out = jnp.where(sub == s, lst[s], out)
        return out

    for h in range(PEER_HEADS):
        tops = []
        for p in range(2):
            hp = 2 * h + p
            qs = q[:, hp * PEER_KEY_DIM:(hp + 1) * PEER_KEY_DIM]
            s_t = lax.dot_general(keys_ref[hp], qs, _NT, preferred_element_type=F32)
            s_ref[hp] = s_t
            tops.append(_top16_sorted([s_t[SUBLANES * r:SUBLANES * (r + 1), :] for r in range(16)]))
        t1, t2 = tops
        t2lo, t2hi, t1hi = pack(t2[0:8]), pack(t2[8:16]), pack(t1[8:16])
        cands = [t1[0] + t2lo, t1[0] + t2hi] + [t1[a] + t2lo for a in range(1, 8)] + [t1hi + t2[0]]
        cands += [jnp.full((SUBLANES, tm), -jnp.inf, F32)] * (16 - len(cands))
        top = _top16_sorted(cands)
        z = jnp.ones((SUBLANES, tm), F32)
        for k in range(1, PEER_TOPK):
            z = z + jnp.exp(top[k] - top[0])
        for k, val in enumerate((top[PEER_TOPK - 1], t1[0], t2[0], z)):
            st_ref[pl.ds(k * PEER_HEADS + h, 1), :] = val[0:1, :]


def _peer_main_kernel(xn_ref, u_ref, vt_ref, s_ref, st_ref, h_ref, o_ref, acc_ref, e_ref, *, rows_per_tile):
    j = pl.program_id(1)
    tm = xn_ref.shape[0]

    @pl.when(j == 0)
    def _():
        acc_ref[...] = jnp.zeros_like(acc_ref)
        for h in range(PEER_HEADS):
            m1 = st_ref[PEER_HEADS + h:PEER_HEADS + h + 1, :]
            m2 = st_ref[2 * PEER_HEADS + h:2 * PEER_HEADS + h + 1, :]
            zz = st_ref[3 * PEER_HEADS + h:3 * PEER_HEADS + h + 1, :]
            e_ref[2 * h] = jnp.exp(s_ref[2 * h] - m1)
            e_ref[2 * h + 1] = jnp.exp(s_ref[2 * h + 1] - m2) / zz

    act = _gelu(lax.dot_general(u_ref[...], xn_ref[...], _NT, preferred_element_type=F32))
    ws = []
    for r in range(rows_per_tile):
        i = j * rows_per_tile + r
        gate = jnp.zeros((PEER_NKEYS, tm), F32)
        for h in range(PEER_HEADS):
            thr = st_ref[h:h + 1, :]
            ssum = s_ref[2 * h, pl.ds(i, 1), :] + s_ref[2 * h + 1]
            val = e_ref[2 * h, pl.ds(i, 1), :] * e_ref[2 * h + 1]
            gate = gate + jnp.where(ssum >= thr, val, 0.0)
        ws.append((act[r * PEER_NKEYS:(r + 1) * PEER_NKEYS, :] * gate).astype(BF16))
    w_t = jnp.concatenate(ws, axis=0)
    acc_ref[...] += jnp.dot(vt_ref[...], w_t, preferred_element_type=F32)

    @pl.when(j == pl.num_programs(1) - 1)
    def _():
        o_ref[...] = acc_ref[...].T + h_ref[...]


def _peer_layer(h, norm_g, w_q, keys, u_tab, v_tab):
    T, D = h.shape
    n_exp = u_tab.shape[0]
    n_hp = 2 * PEER_HEADS
    tm1 = 256
    xn, s_t, stats = pl.pallas_call(
        _peer_score_kernel,
        grid=(T // tm1,),
        in_specs=[pl.BlockSpec((tm1, D), lambda i: (i, 0)),
                  pl.BlockSpec((1, D), lambda i: (0, 0)),
                  pl.BlockSpec((D, n_hp * PEER_KEY_DIM), lambda i: (0, 0)),
                  pl.BlockSpec((n_hp, PEER_NKEYS, PEER_KEY_DIM), lambda i: (0, 0, 0))],
        out_specs=[pl.BlockSpec((tm1, D), lambda i: (i, 0)),
                   pl.BlockSpec((n_hp, PEER_NKEYS, tm1), lambda i: (0, 0, i)),
                   pl.BlockSpec((4 * PEER_HEADS, tm1), lambda i: (0, i))],
        out_shape=[jax.ShapeDtypeStruct((T, D), BF16),
                   jax.ShapeDtypeStruct((n_hp, PEER_NKEYS, T), F32),
                   jax.ShapeDtypeStruct((4 * PEER_HEADS, T), F32)],
        compiler_params=_cparams(("parallel",)),
        name="peer_score",
    )(h, norm_g.reshape(1, D), w_q.astype(BF16),
      keys.reshape(n_hp, PEER_NKEYS, PEER_KEY_DIM).astype(BF16))

    tm, te = 512, 512
    return pl.pallas_call(
        functools.partial(_peer_main_kernel, rows_per_tile=te // PEER_NKEYS),
        grid=(T // tm, n_exp // te),
        in_specs=[pl.BlockSpec((tm, D), lambda i, j: (i, 0)),
                  pl.BlockSpec((te, D), lambda i, j: (j, 0)),
                  pl.BlockSpec((D, te), lambda i, j: (0, j)),
                  pl.BlockSpec((n_hp, PEER_NKEYS, tm), lambda i, j: (0, 0, i)),
                  pl.BlockSpec((4 * PEER_HEADS, tm), lambda i, j: (0, i)),
                  pl.BlockSpec((tm, D), lambda i, j: (i, 0))],
        out_specs=pl.BlockSpec((tm, D), lambda i, j: (i, 0)),
        out_shape=jax.ShapeDtypeStruct((T, D), F32),
        scratch_shapes=[pltpu.VMEM((D, tm), F32), pltpu.VMEM((n_hp, PEER_NKEYS, tm), F32)],
        compiler_params=_cparams(("parallel", "arbitrary")),
        name="peer_main",
    )(xn, u_tab.astype(BF16), v_tab.T.astype(BF16), s_t, stats, h)


def _kv_proj_kernel(h_ref, g_ref, wk_ref, wvt_ref, kg_ref, raw_ref, ks_ref, kw_ref, vt_ref, *, tiles_per_seq):
    G, HD = NSA_KV_GROUPS, NSA_HEAD_DIM
    tk = h_ref.shape[0]
    xn = _rms_rows(h_ref[...], g_ref[...]).astype(BF16)
    y = jnp.dot(xn, wk_ref[...], preferred_element_type=F32)
    for c in range(2 * G):
        raw_ref[c] = y[:, c * HD:(c + 1) * HD]
    row = lax.broadcasted_iota(jnp.int32, (tk, HD), 0)
    lane = lax.broadcasted_iota(jnp.int32, (tk, HD), 1)
    pos_cols = jnp.where(lane < ALIBI_TERMS, row, 0).astype(F32).astype(BF16)
    nsbp = ks_ref.shape[2] - 2 * HD
    blk = lax.rem(pl.program_id(0), tiles_per_seq) * (tk // SLC_BLOCK) \
        + lax.broadcasted_iota(jnp.int32, (tk, nsbp), 0) // SLC_BLOCK
    onehot = jnp.where(lax.broadcasted_iota(jnp.int32, (tk, nsbp), 1) == blk, 1.0, 0.0).astype(BF16)
    for c in range(G):
        ks_ref[c, :, 0:HD] = _rms_rows(y[:, (2 * G + c) * HD:(2 * G + c + 1) * HD], kg_ref[1:2, :]).astype(BF16)
        ks_ref[c, :, HD:2 * HD] = pos_cols
        ks_ref[c, :, 2 * HD:] = onehot
        kw_ref[c, :, 0:HD] = _rms_rows(y[:, (3 * G + c) * HD:(3 * G + c + 1) * HD], kg_ref[2:3, :]).astype(BF16)
        kw_ref[c, :, HD:2 * HD] = pos_cols
    y_t = lax.dot_general(wvt_ref[...], xn, _NT, preferred_element_type=F32)
    for c in range(2 * G):
        vt_ref[c, 0] = y_t[c * HD:(c + 1) * HD, :].astype(BF16)


def _compress_kernel(x_ref, pe_ref, w1_ref, b1_ref, w2_ref, b2_ref, gain_ref, o_ref, *, is_key):
    x = x_ref[0]
    n, half_k = x.shape
    xa = (x + pe_ref[0, 0:1, :]).astype(BF16)
    xb = (x + pe_ref[0, 1:2, :]).astype(BF16)
    a = jnp.dot(xa, w1_ref[0, :half_k, :], preferred_element_type=F32)
    b = jnp.dot(xb, w1_ref[0, half_k:, :], preferred_element_type=F32)
    hid = _gelu(a + pltpu.roll(b, n - 1, axis=0) + b1_ref[0])
    y = jnp.dot(hid.astype(BF16), w2_ref[0], preferred_element_type=F32) + b2_ref[0]
    if is_key:
        o_ref[0] = _rms_rows(y, gain_ref[...]).astype(BF16)
    else:
        o_ref[0] = y.T.astype(BF16)


def _nsa_shared_kv(h, B, S, kv_norm, kv_w, kv_k_norm, phi_pe, phi_w1, phi_b1, phi_w2, phi_b2, tk):
    T, D = h.shape
    G, HD = NSA_KV_GROUPS, NSA_HEAD_DIM
    GW = G * HD
    w = kv_w.astype(BF16)
    sl = lambda six: w[:, six * GW:(six + 1) * GW]
    wk = jnp.concatenate([sl(0), sl(1), sl(2), sl(4)], axis=1)
    wvt = jnp.concatenate([sl(3), sl(5)], axis=1).T
    nsbp = -(-(S // SLC_BLOCK) // LANES) * LANES
    raw, ks, kw, vt = pl.pallas_call(
        functools.partial(_kv_proj_kernel, tiles_per_seq=S // tk),
        grid=(T // tk,),
        in_specs=[pl.BlockSpec((tk, D), lambda i: (i, 0)),
                  pl.BlockSpec((1, D), lambda i: (0, 0)),
                  pl.BlockSpec((D, 4 * GW), lambda i: (0, 0)),
                  pl.BlockSpec((2 * GW, D), lambda i: (0, 0)),
                  pl.BlockSpec((NSA_BRANCHES, HD), lambda i: (0, 0))],
        out_specs=[pl.BlockSpec((2 * G, tk, HD), lambda i: (0, i, 0)),
                   pl.BlockSpec((G, tk, 2 * HD + nsbp), lambda i: (0, i, 0)),
                   pl.BlockSpec((G, tk, 2 * HD), lambda i: (0, i, 0)),
                   pl.BlockSpec((2 * G, 1, HD, tk), lambda i: (0, i, 0, 0))],
        out_shape=[jax.ShapeDtypeStruct((2 * G, T, HD), F32),
                   jax.ShapeDtypeStruct((G, T, 2 * HD + nsbp), BF16),
                   jax.ShapeDtypeStruct((G, T, 2 * HD), BF16),
                   jax.ShapeDtypeStruct((2 * G, T // tk, HD, tk), BF16)],
        compiler_params=_cparams(("parallel",)),
        name="nsa_kv_proj",
    )(h, kv_norm.reshape(1, D), wk, wvt, kv_k_norm)

    ncb = S // CMP_STRIDE
    half_k = CMP_STRIDE * HD
    hidden = phi_w1.shape[-1]
    raw16 = raw.reshape(2 * G, T // CMP_STRIDE, half_k)
    pe = phi_pe.reshape(2, CMP_BLOCK // CMP_STRIDE, half_k)
    outs = []
    for kind in range(2):
        is_key = kind == 0
        o_block = (1, ncb, HD) if is_key else (1, HD, ncb)
        outs.append(pl.pallas_call(
            functools.partial(_compress_kernel, is_key=is_key),
            grid=(B, G),
            in_specs=[pl.BlockSpec((1, ncb, half_k), lambda b, g, kind=kind: (kind * G + g, b, 0)),
                      pl.BlockSpec((1, 2, half_k), lambda b, g, kind=kind: (kind, 0, 0)),
                      pl.BlockSpec((1, 2 * half_k, hidden), lambda b, g, kind=kind: (kind, 0, 0)),
                      pl.BlockSpec((1, 1, hidden), lambda b, g, kind=kind: (kind, 0, 0)),
                      pl.BlockSpec((1, hidden, HD), lambda b, g, kind=kind: (kind, 0, 0)),
                      pl.BlockSpec((1, 1, HD), lambda b, g, kind=kind: (kind, 0, 0)),
                      pl.BlockSpec((1, HD), lambda b, g: (0, 0))],
            out_specs=pl.BlockSpec(o_block, lambda b, g: (b * G + g, 0, 0)),
            out_shape=jax.ShapeDtypeStruct((B * G,) + o_block[1:], BF16),
            compiler_params=_cparams(("parallel", "parallel")),
            name="nsa_compress_k" if is_key else "nsa_compress_v",
        )(raw16, pe, phi_w1.astype(BF16), phi_b1.reshape(2, 1, hidden), phi_w2.astype(BF16),
          phi_b2.reshape(2, 1, HD), kv_k_norm[0:1]))
    return outs[0], outs[1], ks, kw, vt


def _nsa_in_kernel(h_ref, g_ref, wq_ref, wgt_ref, qg_ref, q_ref, gate_ref):
    HD = NSA_HEAD_DIM
    xn = _rms_rows(h_ref[...], g_ref[...]).astype(BF16)
    y = jnp.dot(xn, wq_ref[...], preferred_element_type=F32)
    qscale = HD ** -0.5 * LOG2E
    for hd in range(q_ref.shape[0]):
        q_ref[hd] = (_rms_rows(y[:, hd * HD:(hd + 1) * HD], qg_ref[...]) * qscale).astype(BF16)
    g_t = lax.dot_general(wgt_ref[...], xn, _NT, preferred_element_type=F32)
    gate_ref[...] = jax.nn.sigmoid(g_t)


def _nsa_attn_kernel(slopes_ref, q_ref, gate_ref, kc_ref, vct_ref, ks_ref, vst_ref, kw_ref, vwt_ref, ovt_ref,
                     o_ref, sel_ref, qa_ref, s_ref, p_ref, m_ref, l_ref, acc_ref, oc_ref, os_ref, *, tq, hpg):
    HD = NSA_HEAD_DIM
    tk = tq
    g = pl.program_id(1)
    qi = pl.program_id(2)
    t0 = qi * tq
    ncb = kc_ref.shape[1]
    nsb = sel_ref.shape[0]
    nsbp = qa_ref.shape[2] - 2 * HD
    slopes = [slopes_ref[g * hpg + hh] for hh in range(hpg)]

    tpos_c = lax.broadcasted_iota(jnp.int32, (ncb, tq), 1) + t0
    cpos = lax.broadcasted_iota(jnp.int32, (ncb, tq), 0) * CMP_STRIDE + (CMP_BLOCK - 1)
    dist_c = tpos_c - cpos
    mask_c = dist_c >= 0
    dist_cf = dist_c.astype(F32)
    p_sum = jnp.zeros((ncb, tq), F32)
    for hh in range(hpg):
        s = lax.dot_general(kc_ref[0], q_ref[hh], _NT, preferred_element_type=F32) - slopes[hh] * dist_cf
        s = jnp.where(mask_c, s, MASK_FILL)
        e = jnp.where(mask_c, jnp.exp2(s - jnp.max(s, axis=0, keepdims=True)), 0.0)
        p = e / jnp.maximum(jnp.sum(e, axis=0, keepdims=True), TINY)
        oc_ref[hh] = jnp.dot(vct_ref[0], p.astype(BF16), preferred_element_type=F32)
        p_sum = p_sum + p

    imp = jnp.zeros((nsb, tq), F32)
    rem = p_sum
    for _ in range(3):
        part = rem.astype(BF16)
        imp = imp + jnp.dot(ovt_ref[...], part, preferred_element_type=F32)
        rem = rem - part.astype(F32)
    jb = lax.broadcasted_iota(jnp.int32, (nsb, tq), 0)
    cur = (lax.broadcasted_iota(jnp.int32, (nsb, tq), 1) + t0) // SLC_BLOCK
    forced = jnp.where(jb == 0, 1.0, jnp.where(jb == cur, 1.0, jnp.where(jb == cur - 1, 1.0, 0.0)))
    score = jnp.where(jb <= cur, imp + FORCE_BONUS * forced, -jnp.inf)
    sel_ref[...] = score

    def rank_body(jp, cnt):
        row = sel_ref[pl.ds(jp, 1), :]
        tie = jnp.where(jb > jp, 1.0, 0.0)
        return cnt + jnp.where(row > score, 1.0, jnp.where(row == score, tie, 0.0))

    n_valid = (t0 + tq) // SLC_BLOCK
    cnt = lax.fori_loop(0, n_valid, rank_body, jnp.zeros((nsb, tq), F32))
    neg_sel = jnp.where(cnt < float(N_SEL), 0.0, MASK_FILL)
    if nsbp > nsb:
        neg_sel = jnp.concatenate([neg_sel, jnp.zeros((nsbp - nsb, tq), F32)], axis=0)
    neg_sel_t = neg_sel.T.astype(BF16)

    lane = lax.broadcasted_iota(jnp.int32, (tq, HD), 1)
    for hh in range(hpg):
        rem = jnp.full((tq, HD), slopes[hh], F32)
        cols = jnp.zeros((tq, HD), F32)
        for term in range(ALIBI_TERMS):
            part = rem.astype(BF16).astype(F32)
            cols = jnp.where(lane == term, part, cols)
            rem = rem - part
        qa_ref[hh, :, 0:HD] = q_ref[hh]
        qa_ref[hh, :, HD:2 * HD] = cols.astype(BF16)
        qa_ref[hh, :, 2 * HD:] = neg_sel_t

    kio = lax.broadcasted_iota(jnp.int32, (tk, tq), 0)
    tio = lax.broadcasted_iota(jnp.int32, (tk, tq), 1)
    causal = jnp.where(kio > tio, MASK_FILL, 0.0)
    beyond = jnp.where(kio <= tio, MASK_FILL, 0.0)

    def scores(k_ref, ki, width, hh):
        k0 = pl.multiple_of(ki * tk, tk)
        return lax.dot_general(k_ref[0, pl.ds(k0, tk), :], qa_ref[hh, :, 0:width], _NT,
                               preferred_element_type=F32)

    rc = 32
    chunks = [slice(r * rc, (r + 1) * rc) for r in range(tk // rc)]

    def col_max(hh):
        acc = s_ref[hh, chunks[0], :]
        for ch in chunks[1:]:
            acc = jnp.maximum(acc, s_ref[hh, ch, :])
        return jnp.max(acc, axis=0, keepdims=True)

    def exp_pass(hh, shift):
        lsum = jnp.zeros((rc, tq), F32)
        for ch in chunks:
            p = jnp.exp2(s_ref[hh, ch, :] + shift)
            p_ref[hh, ch, :] = p.astype(BF16)
            lsum = lsum + p
        return jnp.sum(lsum, axis=0, keepdims=True)

    def first_tile(s, v_t, hh):
        s_ref[hh] = s
        m = col_max(hh)
        m_ref[hh] = m
        l_ref[hh] = exp_pass(hh, -m)
        acc_ref[hh] = jnp.dot(v_t, p_ref[hh], preferred_element_type=F32)

    def next_tile(s, c, v_t, hh):
        s_ref[hh] = s
        m_old = m_ref[hh]
        m_new = jnp.maximum(m_old, col_max(hh) + c)
        alpha = jnp.exp2(m_old - m_new)
        l_ref[hh] = alpha * l_ref[hh] + exp_pass(hh, c - m_new)
        acc_ref[hh] = alpha * acc_ref[hh] + jnp.dot(v_t, p_ref[hh], preferred_element_type=F32)
        m_ref[hh] = m_new

    def finish(out_ref):
        for hh in range(hpg):
            out_ref[hh] = acc_ref[hh] / jnp.maximum(l_ref[hh], TINY)

    w_sel = 2 * HD + nsbp
    for hh in range(hpg):
        first_tile(scores(ks_ref, qi, w_sel, hh) + causal, vst_ref[0, qi], hh)

    def sel_body(step, carry):
        ki = qi - 1 - step
        v_t = vst_ref[0, ki]
        back = ((step + 1) * tk).astype(F32)
        for hh in range(hpg):
            next_tile(scores(ks_ref, ki, w_sel, hh), -slopes[hh] * back, v_t, hh)
        return carry

    lax.fori_loop(0, qi, sel_body, 0)
    finish(os_ref)

    w_win = 2 * HD
    for hh in range(hpg):
        first_tile(scores(kw_ref, qi, w_win, hh) + causal, vwt_ref[0, qi], hh)
    for back_tiles in range(1, WINDOW // tk + 1):
        @pl.when(qi >= back_tiles)
        def _(back_tiles=back_tiles):
            ki = qi - back_tiles
            v_t = vwt_ref[0, ki]
            for hh in range(hpg):
                s = scores(kw_ref, ki, w_win, hh)
                if back_tiles == WINDOW // tk:
                    s = s + beyond
                next_tile(s, -slopes[hh] * float(back_tiles * tk), v_t, hh)
    finish(acc_ref)

    for hh in range(hpg):
        gates = [gate_ref[0, hh * NSA_BRANCHES + br:hh * NSA_BRANCHES + br + 1, :] for br in range(NSA_BRANCHES)]
        o_t = gates[0] * oc_ref[hh] + gates[1] * os_ref[hh] + gates[2] * acc_ref[hh]
        o_ref[:, hh * HD:(hh + 1) * HD] = o_t.T.astype(BF16)


def _mm_res_kernel(a_ref, w_ref, h_ref, o_ref):
    o_ref[...] = jnp.dot(a_ref[...], w_ref[...], preferred_element_type=F32) + h_ref[...]


def _nsa_layer(h, B, S, norm_g, w_in, q_norm, w_out, k_c, v_ct, ks, kw, vt, tq):
    T, D = h.shape
    G, HD = NSA_KV_GROUPS, NSA_HEAD_DIM
    width = w_out.shape[0]
    heads = width // HD
    hpg = heads // G
    n_gate = heads * NSA_BRANCHES
    w = w_in.astype(BF16)
    tm = 512
    q, gate_t = pl.pallas_call(
        _nsa_in_kernel,
        grid=(T // tm,),
        in_specs=[pl.BlockSpec((tm, D), lambda i: (i, 0)),
                  pl.BlockSpec((1, D), lambda i: (0, 0)),
                  pl.BlockSpec((D, width), lambda i: (0, 0)),
                  pl.BlockSpec((n_gate, D), lambda i: (0, 0)),
                  pl.BlockSpec((1, HD), lambda i: (0, 0))],
        out_specs=[pl.BlockSpec((heads, tm, HD), lambda i: (0, i, 0)),
                   pl.BlockSpec((n_gate, tm), lambda i: (0, i))],
        out_shape=[jax.ShapeDtypeStruct((heads, T, HD), BF16), jax.ShapeDtypeStruct((n_gate, T), F32)],
        compiler_params=_cparams(("parallel",)),
        name="nsa_in",
    )(h, norm_g.reshape(1, D), w[:, :width], w[:, width:].T, q_norm.reshape(1, HD))

    ncb = S // CMP_STRIDE
    nsb = S // SLC_BLOCK
    n_cmp = (S - CMP_BLOCK) // CMP_STRIDE + 1
    c_start = jnp.arange(ncb) * CMP_STRIDE
    c_pos = c_start + CMP_BLOCK - 1
    s_start = jnp.arange(nsb) * SLC_BLOCK
    ov_t = ((c_start[None, :] < s_start[:, None] + SLC_BLOCK) & (c_pos[None, :] >= s_start[:, None])
            & (jnp.arange(ncb)[None, :] < n_cmp)).astype(BF16)
    h_idx = jnp.arange(1, heads + 1, dtype=F32)
    slopes = 2.0 ** (-8.0 * h_idx / heads) * LOG2E
    nq = S // tq
    w_sel = ks.shape[2]
    w_win = kw.shape[2]
    o = pl.pallas_call(
        functools.partial(_nsa_attn_kernel, tq=tq, hpg=hpg),
        grid_spec=pltpu.PrefetchScalarGridSpec(
            num_scalar_prefetch=1,
            grid=(B, G, nq),
            in_specs=[pl.BlockSpec((hpg, tq, HD), lambda b, g, qi, sl: (g, b * nq + qi, 0)),
                      pl.BlockSpec((1, hpg * NSA_BRANCHES, tq), lambda b, g, qi, sl: (g, 0, b * nq + qi)),
                      pl.BlockSpec((1, ncb, HD), lambda b, g, qi, sl: (b * G + g, 0, 0)),
                      pl.BlockSpec((1, HD, ncb), lambda b, g, qi, sl: (b * G + g, 0, 0)),
                      pl.BlockSpec((1, S, w_sel), lambda b, g, qi, sl: (g, b, 0)),
                      pl.BlockSpec((1, nq, HD, tq), lambda b, g, qi, sl: (g, b, 0, 0)),
                      pl.BlockSpec((1, S, w_win), lambda b, g, qi, sl: (g, b, 0)),
                      pl.BlockSpec((1, nq, HD, tq), lambda b, g, qi, sl: (G + g, b, 0, 0)),
                      pl.BlockSpec((nsb, ncb), lambda b, g, qi, sl: (0, 0))],
            out_specs=pl.BlockSpec((tq, hpg * HD), lambda b, g, qi, sl: (b * nq + qi, g)),
            scratch_shapes=[pltpu.VMEM((nsb, tq), F32),
                            pltpu.VMEM((hpg, tq, w_sel), BF16),
                            pltpu.VMEM((hpg, tq, tq), F32),
                            pltpu.VMEM((hpg, tq, tq), BF16),
                            pltpu.VMEM((hpg, 1, tq), F32),
                            pltpu.VMEM((hpg, 1, tq), F32),
                            pltpu.VMEM((hpg, HD, tq), F32),
                            pltpu.VMEM((hpg, HD, tq), F32),
                            pltpu.VMEM((hpg, HD, tq), F32)]),
        out_shape=jax.ShapeDtypeStruct((T, width), BF16),
        compiler_params=_cparams(("parallel", "parallel", "arbitrary")),
        name="nsa_attn",
    )(slopes, q, gate_t.reshape(G, hpg * NSA_BRANCHES, T), k_c, v_ct, ks, vt, kw, vt, ov_t)

    return pl.pallas_call(
        _mm_res_kernel,
        grid=(T // tm,),
        in_specs=[pl.BlockSpec((tm, width), lambda i: (i, 0)),
                  pl.BlockSpec((width, D), lambda i: (0, 0)),
                  pl.BlockSpec((tm, D), lambda i: (i, 0))],
        out_specs=pl.BlockSpec((tm, D), lambda i: (i, 0)),
        out_shape=jax.ShapeDtypeStruct((T, D), F32),
        compiler_params=_cparams(("parallel",)),
        name="nsa_out",
    )(o, w_out.astype(BF16), h)


def _kv_proj2_kernel(h_ref, g_ref, w_ref, kg_ref, raw_ref, ks_ref, kw_ref, v_ref, *, tiles_per_seq):
    G, HD = NSA_KV_GROUPS, NSA_HEAD_DIM
    tk = h_ref.shape[0]
    xn = _rms_rows(h_ref[...], g_ref[...]).astype(BF16)
    y = jnp.dot(xn, w_ref[...], preferred_element_type=F32)
    col = lambda six, c: y[:, (six * G + c) * HD:(six * G + c + 1) * HD]
    nsbp = ks_ref.shape[2] - HD
    blk = lax.rem(pl.program_id(0), tiles_per_seq) * (tk // SLC_BLOCK) \
        + lax.broadcasted_iota(jnp.int32, (tk, nsbp), 0) // SLC_BLOCK
    onehot = jnp.where(lax.broadcasted_iota(jnp.int32, (tk, nsbp), 1) == blk, 1.0, 0.0).astype(BF16)
    for c in range(G):
        raw_ref[c] = col(0, c)
        raw_ref[G + c] = col(1, c)
        ks_ref[c, :, 0:HD] = _rms_rows(col(2, c), kg_ref[1:2, :]).astype(BF16)
        ks_ref[c, :, HD:] = onehot
        kw_ref[c] = _rms_rows(col(4, c), kg_ref[2:3, :]).astype(BF16)
        v_ref[c] = col(3, c).astype(BF16)
        v_ref[G + c] = col(5, c).astype(BF16)


def _nsa_shared_kv2(h, B, S, kv_norm, kv_w, kv_k_norm, phi_pe, phi_w1, phi_b1, phi_w2, phi_b2, tk):
    T, D = h.shape
    G, HD = NSA_KV_GROUPS, NSA_HEAD_DIM
    nsbp = -(-(S // SLC_BLOCK) // LANES) * LANES
    raw, ks, kw, v = pl.pallas_call(
        functools.partial(_kv_proj2_kernel, tiles_per_seq=S // tk),
        grid=(T // tk,),
        in_specs=[pl.BlockSpec((tk, D), lambda i: (i, 0)),
                  pl.BlockSpec((1, D), lambda i: (0, 0)),
                  pl.BlockSpec((D, 6 * G * HD), lambda i: (0, 0)),
                  pl.BlockSpec((NSA_BRANCHES, HD), lambda i: (0, 0))],
        out_specs=[pl.BlockSpec((2 * G, tk, HD), lambda i: (0, i, 0)),
                   pl.BlockSpec((G, tk, HD + nsbp), lambda i: (0, i, 0)),
                   pl.BlockSpec((G, tk, HD), lambda i: (0, i, 0)),
                   pl.BlockSpec((2 * G, tk, HD), lambda i: (0, i, 0))],
        out_shape=[jax.ShapeDtypeStruct((2 * G, T, HD), F32),
                   jax.ShapeDtypeStruct((G, T, HD + nsbp), BF16),
                   jax.ShapeDtypeStruct((G, T, HD), BF16),
                   jax.ShapeDtypeStruct((2 * G, T, HD), BF16)],
        compiler_params=_cparams(("parallel",)),
        name="nsa_kv_proj",
    )(h, kv_norm.reshape(1, D), kv_w.astype(BF16), kv_k_norm)

    ncb = S // CMP_STRIDE
    half_k = CMP_STRIDE * HD
    hidden = phi_w1.shape[-1]
    raw16 = raw.reshape(2 * G, T // CMP_STRIDE, half_k)
    pe = phi_pe.reshape(2, CMP_BLOCK // CMP_STRIDE, half_k)
    outs = []
    for kind in range(2):
        is_key = kind == 0
        o_block = (1, ncb, HD) if is_key else (1, HD, ncb)
        outs.append(pl.pallas_call(
            functools.partial(_compress_kernel, is_key=is_key),
            grid=(B, G),
            in_specs=[pl.BlockSpec((1, ncb, half_k), lambda b, g, kind=kind: (kind * G + g, b, 0)),
                      pl.BlockSpec((1, 2, half_k), lambda b, g, kind=kind: (kind, 0, 0)),
                      pl.BlockSpec((1, 2 * half_k, hidden), lambda b, g, kind=kind: (kind, 0, 0)),
                      pl.BlockSpec((1, 1, hidden), lambda b, g, kind=kind: (kind, 0, 0)),
                      pl.BlockSpec((1, hidden, HD), lambda b, g, kind=kind: (kind, 0, 0)),
                      pl.BlockSpec((1, 1, HD), lambda b, g, kind=kind: (kind, 0, 0)),
                      pl.BlockSpec((1, HD), lambda b, g: (0, 0))],
            out_specs=pl.BlockSpec(o_block, lambda b, g: (b * G + g, 0, 0)),
            out_shape=jax.ShapeDtypeStruct((B * G,) + o_block[1:], BF16),
            compiler_params=_cparams(("parallel", "parallel")),
            name="nsa_compress_k" if is_key else "nsa_compress_v",
        )(raw16, pe, phi_w1.astype(BF16), phi_b1.reshape(2, 1, hidden), phi_w2.astype(BF16),
          phi_b2.reshape(2, 1, HD), kv_k_norm[0:1]))
    return outs[0], outs[1], ks, kw, v


def _nsa_in2_kernel(h_ref, g_ref, wq_ref, wg_ref, qg_ref, q_ref, gate_ref):
    HD = NSA_HEAD_DIM
    xn = _rms_rows(h_ref[...], g_ref[...]).astype(BF16)
    y = jnp.dot(xn, wq_ref[...], preferred_element_type=F32)
    qscale = HD ** -0.5 * LOG2E
    for hd in range(q_ref.shape[0]):
        q_ref[hd] = (_rms_rows(y[:, hd * HD:(hd + 1) * HD], qg_ref[...]) * qscale).astype(BF16)
    gate_ref[...] = jax.nn.sigmoid(jnp.dot(xn, wg_ref[...], preferred_element_type=F32))


def _nsa_attn2_kernel(slopes_ref, q_ref, gate_ref, kc_ref, vct_ref, ks_ref, vs_ref, kw_ref, vw_ref, ovt_ref,
                      o_ref, sel_ref, qa_ref, p_ref, m_ref, al_ref, lp_ref, acc_ref, oc_ref, os_ref, *, tq, hpg):
    HD = NSA_HEAD_DIM
    tk = tq
    g = pl.program_id(1)
    qi = pl.program_id(2)
    t0 = qi * tq
    ncb = kc_ref.shape[1]
    nsb = sel_ref.shape[0]
    nsbp = qa_ref.shape[1] - HD
    slopes = [slopes_ref[g * hpg + hh] for hh in range(hpg)]

    tpos_c = lax.broadcasted_iota(jnp.int32, (ncb, tq), 1) + t0
    cpos = lax.broadcasted_iota(jnp.int32, (ncb, tq), 0) * CMP_STRIDE + (CMP_BLOCK - 1)
    dist_c = tpos_c - cpos
    mask_c = dist_c >= 0
    dist_cf = dist_c.astype(F32)
    p_sum = jnp.zeros((ncb, tq), F32)
    for hh in range(hpg):
        s = lax.dot_general(kc_ref[0], q_ref[hh], _NT, preferred_element_type=F32) - slopes[hh] * dist_cf
        s = jnp.where(mask_c, s, MASK_FILL)
        e = jnp.where(mask_c, jnp.exp2(s - jnp.max(s, axis=0, keepdims=True)), 0.0)
        p = e / jnp.maximum(jnp.sum(e, axis=0, keepdims=True), TINY)
        oc_ref[hh] = jnp.dot(vct_ref[0], p.astype(BF16), preferred_element_type=F32)
        p_sum = p_sum + p

    imp = jnp.zeros((nsb, tq), F32)
    rem = p_sum
    for _ in range(3):
        part = rem.astype(BF16)
        imp = imp + jnp.dot(ovt_ref[...], part, preferred_element_type=F32)
        rem = rem - part.astype(F32)
    jb = lax.broadcasted_iota(jnp.int32, (nsb, tq), 0)
    cur = (lax.broadcasted_iota(jnp.int32, (nsb, tq), 1) + t0) // SLC_BLOCK
    forced = jnp.where(jb == 0, 1.0, jnp.where(jb == cur, 1.0, jnp.where(jb == cur - 1, 1.0, 0.0)))
    score = jnp.where(jb <= cur, imp + FORCE_BONUS * forced, -jnp.inf)
    sel_ref[...] = score

    def rank_body(jp, cnt):
        row = sel_ref[pl.ds(jp, 1), :]
        tie = jnp.where(jb > jp, 1.0, 0.0)
        return cnt + jnp.where(row > score, 1.0, jnp.where(row == score, tie, 0.0))

    n_valid = (t0 + tq) // SLC_BLOCK
    cnt = lax.fori_loop(0, n_valid, rank_body, jnp.zeros((nsb, tq), F32))
    neg_sel = jnp.where(cnt < float(N_SEL), 0.0, MASK_FILL)
    if nsbp > nsb:
        neg_sel = jnp.concatenate([neg_sel, jnp.zeros((nsbp - nsb, tq), F32)], axis=0)
    neg_sel_t = neg_sel.T.astype(BF16)
    for hh in range(hpg):
        qa_ref[hh * tq:(hh + 1) * tq, 0:HD] = q_ref[hh]
        qa_ref[hh * tq:(hh + 1) * tq, HD:] = neg_sel_t

    row_i = lax.broadcasted_iota(jnp.int32, (tq, tk), 0)
    col_i = lax.broadcasted_iota(jnp.int32, (tq, tk), 1)
    causal = jnp.where(col_i > row_i, MASK_FILL, 0.0)
    beyond = jnp.where(col_i <= row_i, MASK_FILL, 0.0)
    key_off = lax.broadcasted_iota(jnp.int32, (1, tk), 1).astype(F32)

    def tile(s, v_tile, back, extra, first):
        for hh in range(hpg):
            rows = slice(hh * tq, (hh + 1) * tq)
            sh = s[rows] + slopes[hh] * (key_off - back)
            if extra is not None:
                sh = sh + extra
            mx = jnp.max(sh, axis=1, keepdims=True)
            if first:
                m_new = mx
            else:
                m_old = m_ref[rows]
                m_new = jnp.maximum(m_old, mx)
                al_ref[rows] = jnp.exp2(m_old - m_new)
            m_ref[rows] = m_new
            p = jnp.exp2(sh - m_new)
            p_ref[rows] = p.astype(BF16)
            part = p[:, 0:LANES]
            for c in range(1, tk // LANES):
                part = part + p[:, c * LANES:(c + 1) * LANES]
            if first:
                lp_ref[rows] = part
            else:
                lp_ref[rows] = al_ref[rows] * lp_ref[rows] + part
        pv = jnp.dot(p_ref[...], v_tile, preferred_element_type=F32)
        if first:
            acc_ref[...] = pv
        else:
            acc_ref[...] = al_ref[...] * acc_ref[...] + pv

    def key_tile(k_ref, ki):
        return k_ref[0, pl.ds(pl.multiple_of(ki * tk, tk), tk), :]

    def finish(out_ref):
        l = jnp.sum(lp_ref[...], axis=1, keepdims=True)
        out_ref[...] = acc_ref[...] / jnp.maximum(l, TINY)

    s0 = lax.dot_general(qa_ref[...], key_tile(ks_ref, qi), _NT, preferred_element_type=F32)
    tile(s0, key_tile(vs_ref, qi), 0.0, causal, True)

    def sel_body(step, carry):
        ki = qi - 1 - step
        s = lax.dot_general(qa_ref[...], key_tile(ks_ref, ki), _NT, preferred_element_type=F32)
        tile(s, key_tile(vs_ref, ki), ((step + 1) * tk).astype(F32), None, False)
        return carry

    lax.fori_loop(0, qi, sel_body, 0)
    finish(os_ref)

    q_all = q_ref[...].reshape(hpg * tq, HD)
    s0 = lax.dot_general(q_all, key_tile(kw_ref, qi), _NT, preferred_element_type=F32)
    tile(s0, key_tile(vw_ref, qi), 0.0, causal, True)
    for back_tiles in range(1, WINDOW // tk + 1):
        @pl.when(qi >= back_tiles)
        def _(back_tiles=back_tiles):
            ki = qi - back_tiles
            s = lax.dot_general(q_all, key_tile(kw_ref, ki), _NT, preferred_element_type=F32)
            tile(s, key_tile(vw_ref, ki), float(back_tiles * tk),
                 beyond if back_tiles == WINDOW // tk else None, False)
    l_w = jnp.sum(lp_ref[...], axis=1, keepdims=True)

    for hh in range(hpg):
        rows = slice(hh * tq, (hh + 1) * tq)
        gates = [gate_ref[0, :, hh * NSA_BRANCHES + br:hh * NSA_BRANCHES + br + 1] for br in range(NSA_BRANCHES)]
        o_w = acc_ref[rows] / jnp.maximum(l_w[rows], TINY)
        o = gates[0] * oc_ref[hh].T + gates[1] * os_ref[rows] + gates[2] * o_w
        o_ref[:, hh * HD:(hh + 1) * HD] = o.astype(BF16)


def _nsa_layer2(h, B, S, norm_g, w_in, q_norm, w_out, k_c, v_ct, ks, kw, v, tq):
    T, D = h.shape
    G, HD = NSA_KV_GROUPS, NSA_HEAD_DIM
    width = w_out.shape[0]
    heads = width // HD
    hpg = heads // G
    n_gate = heads * NSA_BRANCHES
    w = w_in.astype(BF16)
    w_gate = jnp.pad(w[:, width:], ((0, 0), (0, LANES - n_gate)))
    tm = 512
    q, gate = pl.pallas_call(
        _nsa_in2_kernel,
        grid=(T // tm,),
        in_specs=[pl.BlockSpec((tm, D), lambda i: (i, 0)),
                  pl.BlockSpec((1, D), lambda i: (0, 0)),
                  pl.BlockSpec((D, width), lambda i: (0, 0)),
                  pl.BlockSpec((D, LANES), lambda i: (0, 0)),
                  pl.BlockSpec((1, HD), lambda i: (0, 0))],
        out_specs=[pl.BlockSpec((heads, tm, HD), lambda i: (0, i, 0)),
                   pl.BlockSpec((tm, LANES), lambda i: (i, 0))],
        out_shape=[jax.ShapeDtypeStruct((heads, T, HD), BF16), jax.ShapeDtypeStruct((T, LANES), F32)],
        compiler_params=_cparams(("parallel",)),
        name="nsa_in",
    )(h, norm_g.reshape(1, D), w[:, :width], w_gate, q_norm.reshape(1, HD))
    gate = gate[:, :n_gate].reshape(T, G, hpg * NSA_BRANCHES).transpose(1, 0, 2)

    ncb = S // CMP_STRIDE
    nsb = S // SLC_BLOCK
    n_cmp = (S - CMP_BLOCK) // CMP_STRIDE + 1
    c_start = jnp.arange(ncb) * CMP_STRIDE
    c_pos = c_start + CMP_BLOCK - 1
    s_start = jnp.arange(nsb) * SLC_BLOCK
    ov_t = ((c_start[None, :] < s_start[:, None] + SLC_BLOCK) & (c_pos[None, :] >= s_start[:, None])
            & (jnp.arange(ncb)[None, :] < n_cmp)).astype(BF16)
    h_idx = jnp.arange(1, heads + 1, dtype=F32)
    slopes = 2.0 ** (-8.0 * h_idx / heads) * LOG2E
    nq = S // tq
    w_sel = ks.shape[2]
    o = pl.pallas_call(
        functools.partial(_nsa_attn2_kernel, tq=tq, hpg=hpg),
        grid_spec=pltpu.PrefetchScalarGridSpec(
            num_scalar_prefetch=1,
            grid=(B, G, nq),
            in_specs=[pl.BlockSpec((hpg, tq, HD), lambda b, g, qi, sl: (g, b * nq + qi, 0)),
                      pl.BlockSpec((1, tq, hpg * NSA_BRANCHES), lambda b, g, qi, sl: (g, b * nq + qi, 0)),
                      pl.BlockSpec((1, ncb, HD), lambda b, g, qi, sl: (b * G + g, 0, 0)),
                      pl.BlockSpec((1, HD, ncb), lambda b, g, qi, sl: (b * G + g, 0, 0)),
                      pl.BlockSpec((1, S, w_sel), lambda b, g, qi, sl: (g, b, 0)),
                      pl.BlockSpec((1, S, HD), lambda b, g, qi, sl: (g, b, 0)),
                      pl.BlockSpec((1, S, HD), lambda b, g, qi, sl: (g, b, 0)),
                      pl.BlockSpec((1, S, HD), lambda b, g, qi, sl: (G + g, b, 0)),
                      pl.BlockSpec((nsb, ncb), lambda b, g, qi, sl: (0, 0))],
            out_specs=pl.BlockSpec((tq, hpg * HD), lambda b, g, qi, sl: (b * nq + qi, g)),
            scratch_shapes=[pltpu.VMEM((nsb, tq), F32),
                            pltpu.VMEM((hpg * tq, w_sel), BF16),
                            pltpu.VMEM((hpg * tq, tq), BF16),
                            pltpu.VMEM((hpg * tq, 1), F32),
                            pltpu.VMEM((hpg * tq, 1), F32),
                            pltpu.VMEM((hpg * tq, LANES), F32),
                            pltpu.VMEM((hpg * tq, HD), F32),
                            pltpu.VMEM((hpg, HD, tq), F32),
                            pltpu.VMEM((hpg * tq, HD), F32)]),
        out_shape=jax.ShapeDtypeStruct((T, width), BF16),
        compiler_params=_cparams(("parallel", "parallel", "arbitrary")),
        name="nsa_attn",
    )(slopes, q, gate, k_c, v_ct, ks, v, kw, v, ov_t)

    return pl.pallas_call(
        _mm_res_kernel,
        grid=(T // tm,),
        in_specs=[pl.BlockSpec((tm, width), lambda i: (i, 0)),
                  pl.BlockSpec((width, D), lambda i: (0, 0)),
                  pl.BlockSpec((tm, D), lambda i: (i, 0))],
        out_specs=pl.BlockSpec((tm, D), lambda i: (i, 0)),
        out_shape=jax.ShapeDtypeStruct((T, D), F32),
        compiler_params=_cparams(("parallel",)),
        name="nsa_out",
    )(o, w_out.astype(BF16), h)


def _kv_proj3_kernel(h_ref, g_ref, w_ref, wvt_ref, kg_ref, raw_ref, ks_ref, kw_ref, vt_ref, *, tiles_per_seq):
    G, HD = NSA_KV_GROUPS, NSA_HEAD_DIM
    tk = h_ref.shape[0]
    xn = _rms_rows(h_ref[...], g_ref[...]).astype(BF16)
    y = jnp.dot(xn, w_ref[...], preferred_element_type=F32)
    col = lambda six, c: y[:, (six * G + c) * HD:(six * G + c + 1) * HD]
    nsbp = ks_ref.shape[2] - HD
    blk = lax.rem(pl.program_id(0), tiles_per_seq) * (tk // SLC_BLOCK) \
        + lax.broadcasted_iota(jnp.int32, (tk, nsbp), 0) // SLC_BLOCK
    onehot = jnp.where(lax.broadcasted_iota(jnp.int32, (tk, nsbp), 1) == blk, 1.0, 0.0).astype(BF16)
    for c in range(G):
        raw_ref[c] = col(0, c)
        raw_ref[G + c] = col(1, c)
        ks_ref[c, :, 0:HD] = _rms_rows(col(2, c), kg_ref[1:2, :]).astype(BF16)
        ks_ref[c, :, HD:] = onehot
        kw_ref[c] = _rms_rows(col(3, c), kg_ref[2:3, :]).astype(BF16)
    y_t = lax.dot_general(wvt_ref[...], xn, _NT, preferred_element_type=F32)
    for c in range(2 * G):
        vt_ref[c, 0] = y_t[c * HD:(c + 1) * HD, :].astype(BF16)


def _nsa_shared_kv3(h, B, S, kv_norm, kv_w, kv_k_norm, phi_pe, phi_w1, phi_b1, phi_w2, phi_b2, tk):
    T, D = h.shape
    G, HD = NSA_KV_GROUPS, NSA_HEAD_DIM
    GW = G * HD
    w = kv_w.astype(BF16)
    sl = lambda six: w[:, six * GW:(six + 1) * GW]
    wk = jnp.concatenate([sl(0), sl(1), sl(2), sl(4)], axis=1)
    wvt = jnp.concatenate([sl(3), sl(5)], axis=1).T
    nsbp = -(-(S // SLC_BLOCK) // LANES) * LANES
    raw, ks, kw, vt = pl.pallas_call(
        functools.partial(_kv_proj3_kernel, tiles_per_seq=S // tk),
        grid=(T // tk,),
        in_specs=[pl.BlockSpec((tk, D), lambda i: (i, 0)),
                  pl.BlockSpec((1, D), lambda i: (0, 0)),
                  pl.BlockSpec((D, 4 * GW), lambda i: (0, 0)),
                  pl.BlockSpec((2 * GW, D), lambda i: (0, 0)),
                  pl.BlockSpec((NSA_BRANCHES, HD), lambda i: (0, 0))],
        out_specs=[pl.BlockSpec((2 * G, tk, HD), lambda i: (0, i, 0)),
                   pl.BlockSpec((G, tk, HD + nsbp), lambda i: (0, i, 0)),
                   pl.BlockSpec((G, tk, HD), lambda i: (0, i, 0)),
                   pl.BlockSpec((2 * G, 1, HD, tk), lambda i: (0, i, 0, 0))],
        out_shape=[jax.ShapeDtypeStruct((2 * G, T, HD), F32),
                   jax.ShapeDtypeStruct((G, T, HD + nsbp), BF16),
                   jax.ShapeDtypeStruct((G, T, HD), BF16),
                   jax.ShapeDtypeStruct((2 * G, T // tk, HD, tk), BF16)],
        compiler_params=_cparams(("parallel",)),
        name="nsa_kv_proj",
    )(h, kv_norm.reshape(1, D), wk, wvt, kv_k_norm)

    ncb = S // CMP_STRIDE
    half_k = CMP_STRIDE * HD
    hidden = phi_w1.shape[-1]
    raw16 = raw.reshape(2 * G, T // CMP_STRIDE, half_k)
    pe = phi_pe.reshape(2, CMP_BLOCK // CMP_STRIDE, half_k)
    outs = []
    for kind in range(2):
        is_key = kind == 0
        o_block = (1, ncb, HD) if is_key else (1, HD, ncb)
        outs.append(pl.pallas_call(
            functools.partial(_compress_kernel, is_key=is_key),
            grid=(B, G),
            in_specs=[pl.BlockSpec((1, ncb, half_k), lambda b, g, kind=kind: (kind * G + g, b, 0)),
                      pl.BlockSpec((1, 2, half_k), lambda b, g, kind=kind: (kind, 0, 0)),
                      pl.BlockSpec((1, 2 * half_k, hidden), lambda b, g, kind=kind: (kind, 0, 0)),
                      pl.BlockSpec((1, 1, hidden), lambda b, g, kind=kind: (kind, 0, 0)),
                      pl.BlockSpec((1, hidden, HD), lambda b, g, kind=kind: (kind, 0, 0)),
                      pl.BlockSpec((1, 1, HD), lambda b, g, kind=kind: (kind, 0, 0)),
                      pl.BlockSpec((1, HD), lambda b, g: (0, 0))],
            out_specs=pl.BlockSpec(o_block, lambda b, g: (b * G + g, 0, 0)),
            out_shape=jax.ShapeDtypeStruct((B * G,) + o_block[1:], BF16),
            compiler_params=_cparams(("parallel", "parallel")),
            name="nsa_compress_k" if is_key else "nsa_compress_v",
        )(raw16, pe, phi_w1.astype(BF16), phi_b1.reshape(2, 1, hidden), phi_w2.astype(BF16),
          phi_b2.reshape(2, 1, HD), kv_k_norm[0:1]))
    return outs[0], outs[1], ks, kw, vt


def _nsa_attn3_kernel(slopes_ref, q_ref, gate_ref, kc_ref, vct_ref, ks_ref, vst_ref, kw_ref, vwt_ref, ovt_ref,
                      o_ref, sel_ref, qat_ref, b_ref, s_ref, p_ref, m_ref, l_ref, acc_ref, oc_ref, os_ref, *,
                      tq, hpg):
    HD = NSA_HEAD_DIM
    tk = tq
    g = pl.program_id(1)
    qi = pl.program_id(2)
    t0 = qi * tq
    ncb = kc_ref.shape[1]
    nsb = sel_ref.shape[0]
    nsbp = qat_ref.shape[0] - HD
    slopes = [slopes_ref[g * hpg + hh] for hh in range(hpg)]

    tpos_c = lax.broadcasted_iota(jnp.int32, (ncb, tq), 1) + t0
    cpos = lax.broadcasted_iota(jnp.int32, (ncb, tq), 0) * CMP_STRIDE + (CMP_BLOCK - 1)
    dist_c = tpos_c - cpos
    mask_c = dist_c >= 0
    dist_cf = dist_c.astype(F32)
    p_sum = jnp.zeros((ncb, tq), F32)
    for hh in range(hpg):
        s = lax.dot_general(kc_ref[0], q_ref[hh], _NT, preferred_element_type=F32) - slopes[hh] * dist_cf
        s = jnp.where(mask_c, s, MASK_FILL)
        e = jnp.where(mask_c, jnp.exp2(s - jnp.max(s, axis=0, keepdims=True)), 0.0)
        p = e / jnp.maximum(jnp.sum(e, axis=0, keepdims=True), TINY)
        oc_ref[hh] = jnp.dot(vct_ref[0], p.astype(BF16), preferred_element_type=F32)
        p_sum = p_sum + p

    imp = jnp.zeros((nsb, tq), F32)
    rem = p_sum
    for _ in range(3):
        part = rem.astype(BF16)
        imp = imp + jnp.dot(ovt_ref[...], part, preferred_element_type=F32)
        rem = rem - part.astype(F32)
    jb = lax.broadcasted_iota(jnp.int32, (nsb, tq), 0)
    cur = (lax.broadcasted_iota(jnp.int32, (nsb, tq), 1) + t0) // SLC_BLOCK
    forced = jnp.where(jb == 0, 1.0, jnp.where(jb == cur, 1.0, jnp.where(jb == cur - 1, 1.0, 0.0)))
    score = jnp.where(jb <= cur, imp + FORCE_BONUS * forced, -jnp.inf)
    sel_ref[...] = score

    def rank_body(jp, cnt):
        row = sel_ref[pl.ds(jp, 1), :]
        tie = jnp.where(jb > jp, 1.0, 0.0)
        return cnt + jnp.where(row > score, 1.0, jnp.where(row == score, tie, 0.0))

    n_valid = (t0 + tq) // SLC_BLOCK
    cnt = lax.fori_loop(0, n_valid, rank_body, jnp.zeros((nsb, tq), F32))
    neg_sel = jnp.where(cnt < float(N_SEL), 0.0, MASK_FILL)
    if nsbp > nsb:
        neg_sel = jnp.concatenate([neg_sel, jnp.zeros((nsbp - nsb, tq), F32)], axis=0)
    neg_sel = neg_sel.astype(BF16)

    wl = hpg * tq
    rc = 16
    kio = lax.broadcasted_iota(jnp.int32, (tk, wl), 0)
    kio_c = lax.broadcasted_iota(jnp.int32, (rc, wl), 0)
    tio_c = lax.rem(lax.broadcasted_iota(jnp.int32, (rc, wl), 1), tq)
    slope_row = jnp.concatenate([jnp.full((1, tq), slopes[hh], F32) for hh in range(hpg)], axis=1)
    for hh in range(hpg):
        qat_ref[0:HD, hh * tq:(hh + 1) * tq] = q_ref[hh].astype(F32).T.astype(BF16)
        qat_ref[HD:, hh * tq:(hh + 1) * tq] = neg_sel
    b_ref[...] = slope_row * kio.astype(F32)

    chunks = [slice(r * rc, (r + 1) * rc) for r in range(tk // rc)]

    def key_tile(k_ref, ki):
        return k_ref[0, pl.ds(pl.multiple_of(ki * tk, tk), tk), :]

    def bias_and_max(mask_kind):
        acc = None
        for r, ch in enumerate(chunks):
            t = s_ref[ch, :] + b_ref[ch, :]
            if mask_kind == "causal":
                t = t + jnp.where(kio_c + r * rc > tio_c, MASK_FILL, 0.0)
            elif mask_kind == "beyond":
                t = t + jnp.where(kio_c + r * rc <= tio_c, MASK_FILL, 0.0)
            s_ref[ch, :] = t
            acc = t if acc is None else jnp.maximum(acc, t)
        return jnp.max(acc, axis=0, keepdims=True)

    def exp_pass(shift):
        lsum = jnp.zeros((rc, wl), F32)
        for ch in chunks:
            p = jnp.exp2(s_ref[ch, :] + shift)
            p_ref[ch, :] = p.astype(BF16)
            lsum = lsum + p
        return jnp.sum(lsum, axis=0, keepdims=True)

    def first_tile(s, v_t):
        s_ref[...] = s
        m = bias_and_max("causal")
        m_ref[...] = m
        l_ref[...] = exp_pass(-m)
        acc_ref[...] = jnp.dot(v_t, p_ref[...], preferred_element_type=F32)

    def next_tile(s, back, v_t, mask_kind):
        s_ref[...] = s
        c = -slope_row * back
        m_old = m_ref[...]
        m_new = jnp.maximum(m_old, bias_and_max(mask_kind) + c)
        alpha = jnp.exp2(m_old - m_new)
        l_ref[...] = alpha * l_ref[...] + exp_pass(c - m_new)
        acc_ref[...] = alpha * acc_ref[...] + jnp.dot(v_t, p_ref[...], preferred_element_type=F32)
        m_ref[...] = m_new

    def sel_scores(ki):
        return jnp.dot(key_tile(ks_ref, ki), qat_ref[...], preferred_element_type=F32)

    def win_scores(ki):
        return jnp.dot(key_tile(kw_ref, ki), qat_ref[0:HD, :], preferred_element_type=F32)

    first_tile(sel_scores(qi), vst_ref[0, qi])

    def sel_body(step, carry):
        ki = qi - 1 - step
        next_tile(sel_scores(ki), ((step + 1) * tk).astype(F32), vst_ref[0, ki], None)
        return carry

    lax.fori_loop(0, qi, sel_body, 0)
    os_ref[...] = acc_ref[...] / jnp.maximum(l_ref[...], TINY)

    first_tile(win_scores(qi), vwt_ref[0, qi])
    for back_tiles in range(1, WINDOW // tk + 1):
        @pl.when(qi >= back_tiles)
        def _(back_tiles=back_tiles):
            ki = qi - back_tiles
            next_tile(win_scores(ki), float(back_tiles * tk), vwt_ref[0, ki],
                      "beyond" if back_tiles == WINDOW // tk else None)
    o_w = acc_ref[...] / jnp.maximum(l_ref[...], TINY)

    for hh in range(hpg):
        cols = slice(hh * tq, (hh + 1) * tq)
        gates = [gate_ref[0, hh * NSA_BRANCHES + br:hh * NSA_BRANCHES + br + 1, :] for br in range(NSA_BRANCHES)]
        o_t = gates[0] * oc_ref[hh] + gates[1] * os_ref[:, cols] + gates[2] * o_w[:, cols]
        o_ref[:, hh * HD:(hh + 1) * HD] = o_t.T.astype(BF16)


def _nsa_layer3(h, B, S, norm_g, w_in, q_norm, w_out, k_c, v_ct, ks, kw, vt, tq):
    T, D = h.shape
    G, HD = NSA_KV_GROUPS, NSA_HEAD_DIM
    width = w_out.shape[0]
    heads = width // HD
    hpg = heads // G
    n_gate = heads * NSA_BRANCHES
    w = w_in.astype(BF16)
    tm = 512
    q, gate_t = pl.pallas_call(
        _nsa_in_kernel,
        grid=(T // tm,),
        in_specs=[pl.BlockSpec((tm, D), lambda i: (i, 0)),
                  pl.BlockSpec((1, D), lambda i: (0, 0)),
                  pl.BlockSpec((D, width), lambda i: (0, 0)),
                  pl.BlockSpec((n_gate, D), lambda i: (0, 0)),
                  pl.BlockSpec((1, HD), lambda i: (0, 0))],
        out_specs=[pl.BlockSpec((heads, tm, HD), lambda i: (0, i, 0)),
                   pl.BlockSpec((n_gate, tm), lambda i: (0, i))],
        out_shape=[jax.ShapeDtypeStruct((heads, T, HD), BF16), jax.ShapeDtypeStruct((n_gate, T), F32)],
        compiler_params=_cparams(("parallel",)),
        name="nsa_in",
    )(h, norm_g.reshape(1, D), w[:, :width], w[:, width:].T, q_norm.reshape(1, HD))

    ncb = S // CMP_STRIDE
    nsb = S // SLC_BLOCK
    n_cmp = (S - CMP_BLOCK) // CMP_STRIDE + 1
    c_start = jnp.arange(ncb) * CMP_STRIDE
    c_pos = c_start + CMP_BLOCK - 1
    s_start = jnp.arange(nsb) * SLC_BLOCK
    ov_t = ((c_start[None, :] < s_start[:, None] + SLC_BLOCK) & (c_pos[None, :] >= s_start[:, None])
            & (jnp.arange(ncb)[None, :] < n_cmp)).astype(BF16)
    h_idx = jnp.arange(1, heads + 1, dtype=F32)
    slopes = 2.0 ** (-8.0 * h_idx / heads) * LOG2E
    nq = S // tq
    w_sel = ks.shape[2]
    o = pl.pallas_call(
        functools.partial(_nsa_attn3_kernel, tq=tq, hpg=hpg),
        grid_spec=pltpu.PrefetchScalarGridSpec(
            num_scalar_prefetch=1,
            grid=(B, G, nq),
            in_specs=[pl.BlockSpec((hpg, tq, HD), lambda b, g, qi, sl: (g, b * nq + qi, 0)),
                      pl.BlockSpec((1, hpg * NSA_BRANCHES, tq), lambda b, g, qi, sl: (g, 0, b * nq + qi)),
                      pl.BlockSpec((1, ncb, HD), lambda b, g, qi, sl: (b * G + g, 0, 0)),
                      pl.BlockSpec((1, HD, ncb), lambda b, g, qi, sl: (b * G + g, 0, 0)),
                      pl.BlockSpec((1, S, w_sel), lambda b, g, qi, sl: (g, b, 0)),
                      pl.BlockSpec((1, nq, HD, tq), lambda b, g, qi, sl: (g, b, 0, 0)),
                      pl.BlockSpec((1, S, HD), lambda b, g, qi, sl: (g, b, 0)),
                      pl.BlockSpec((1, nq, HD, tq), lambda b, g, qi, sl: (G + g, b, 0, 0)),
                      pl.BlockSpec((nsb, ncb), lambda b, g, qi, sl: (0, 0))],
            out_specs=pl.BlockSpec((tq, hpg * HD), lambda b, g, qi, sl: (b * nq + qi, g)),
            scratch_shapes=[pltpu.VMEM((nsb, tq), F32),
                            pltpu.VMEM((w_sel, hpg * tq), BF16),
                            pltpu.VMEM((tq, hpg * tq), F32),
                            pltpu.VMEM((tq, hpg * tq), F32),
                            pltpu.VMEM((tq, hpg * tq), BF16),
                            pltpu.VMEM((1, hpg * tq), F32),
                            pltpu.VMEM((1, hpg * tq), F32),
                            pltpu.VMEM((HD, hpg * tq), F32),
                            pltpu.VMEM((hpg, HD, tq), F32),
                            pltpu.VMEM((HD, hpg * tq), F32)]),
        out_shape=jax.ShapeDtypeStruct((T, width), BF16),
        compiler_params=_cparams(("parallel", "parallel", "arbitrary")),
        name="nsa_attn",
    )(slopes, q, gate_t.reshape(G, hpg * NSA_BRANCHES, T), k_c, v_ct, ks, vt, kw, vt, ov_t)

    return pl.pallas_call(
        _mm_res_kernel,
        grid=(T // tm,),
        in_specs=[pl.BlockSpec((tm, width), lambda i: (i, 0)),
                  pl.BlockSpec((width, D), lambda i: (0, 0)),
                  pl.BlockSpec((tm, D), lambda i: (i, 0))],
        out_specs=pl.BlockSpec((tm, D), lambda i: (i, 0)),
        out_shape=jax.ShapeDtypeStruct((T, D), F32),
        compiler_params=_cparams(("parallel",)),
        name="nsa_out",
    )(o, w_out.astype(BF16), h)


def kernel(x, a_norm, a_w_in, a_v_norm, a_w_s, a_b_s, a_w_out, kv_norm, kv_w, kv_k_norm, phi_pe, phi_w1, phi_b1, phi_w2, phi_b2, b_norm, b_w_in, b_q_norm, b_w_out, ffn_norm, peer_w_q, peer_keys, peer_u, peer_v):
    B, S, D = x.shape
    depth = ffn_norm.shape[0]
    n_a = a_norm.shape[0]
    tq = 256
    h = x.reshape(B * S, D)
    shared = None
    for layer in range(depth):
        if layer < n_a:
            a = layer
            h = _gmlp_layer(h, a_norm[a], a_w_in[a], a_v_norm[a], a_w_s[a], a_b_s[a], a_w_out[a])
        else:
            if shared is None:
                shared = _nsa_shared_kv3(h, B, S, kv_norm, kv_w, kv_k_norm, phi_pe, phi_w1, phi_b1, phi_w2,
                                         phi_b2, tq)
            b = layer - n_a
            h = _nsa_layer3(h, B, S, b_norm[b], b_w_in[b], b_q_norm[b], b_w_out[b], *shared, tq)
        h = _peer_layer(h, ffn_norm[layer], peer_w_q[layer], peer_keys[layer], peer_u[layer], peer_v[layer])
    return h.reshape(B, S, D)
```

```python
import functools
import math

import jax
import jax.numpy as jnp
from jax import lax
from jax.experimental import pallas as pl
from jax.experimental.pallas import tpu as pltpu

F32 = jnp.float32
BF16 = jnp.bfloat16

GMLP_CHUNK = 128
GMLP_GROUPS = 8
NSA_HEAD_DIM = 128
NSA_KV_GROUPS = 4
NSA_BRANCHES = 3
CMP_BLOCK = 32
CMP_STRIDE = 16
SLC_BLOCK = 64
N_SEL = 16
WINDOW = 512
FORCE_BONUS = 1000.0
PEER_HEADS = 8
PEER_NKEYS = 128
PEER_TOPK = 16
PEER_KEY_DIM = 128
RMS_EPS = 1e-6
MASK_FILL = -1e30
TINY = float(jnp.finfo(jnp.float32).tiny)

V7X_VMEM_LIMIT_BYTES = 60 * 1024 * 1024
LANES = 128
SUBLANES = 8
LOG2E = 1.4426950408889634

_NT = (((1,), (1,)), ((), ()))


def _cparams(sem):
    return pltpu.CompilerParams(dimension_semantics=sem, vmem_limit_bytes=V7X_VMEM_LIMIT_BYTES)


def _gelu(x):
    c = math.sqrt(2.0 / math.pi)
    return x * (0.5 * (1.0 + jnp.tanh(c * (x + 0.044715 * (x * x * x)))))


def _rms_rows(x, gain):
    r = lax.rsqrt(jnp.mean(x * x, axis=-1, keepdims=True) + RMS_EPS)
    return (x * r) * gain


def _gmlp_in_kernel(h_ref, g_ref, w_ref, z_ref, ss_ref, xn_ref, *, n_u_tiles):
    j = pl.program_id(1)

    @pl.when(j == 0)
    def _():
        xn_ref[...] = _rms_rows(h_ref[...], g_ref[...]).astype(BF16)
        ss_ref[...] = jnp.zeros_like(ss_ref)

    z = _gelu(jnp.dot(xn_ref[...], w_ref[...], preferred_element_type=F32))
    z_ref[...] = z.astype(BF16)

    @pl.when(j >= n_u_tiles)
    def _():
        ss_ref[...] += jnp.sum(z * z, axis=-1, keepdims=True)


def _gmlp_out_kernel(u_ref, v_ref, ss_ref, vg_ref, ws_ref, bs_ref, wo_ref, h_ref, o_ref, prod_ref, *,
                     n_chunks, half):
    j = pl.program_id(1)
    gdim = half // GMLP_GROUPS

    @pl.when(j == 0)
    def _():
        row = lax.broadcasted_iota(jnp.int32, (GMLP_CHUNK, GMLP_CHUNK), 0)
        col = lax.broadcasted_iota(jnp.int32, (GMLP_CHUNK, GMLP_CHUNK), 1)
        tril = row >= col
        for c in range(n_chunks):
            rows = slice(c * GMLP_CHUNK, (c + 1) * GMLP_CHUNK)
            rinv = lax.rsqrt(ss_ref[c] * (1.0 / half) + RMS_EPS)
            for g in range(GMLP_GROUPS):
                cols = slice(g * gdim, (g + 1) * gdim)
                wc = (jnp.where(tril, ws_ref[g], 0.0) * rinv).astype(BF16)
                sv = jnp.dot(wc, v_ref[rows, cols], preferred_element_type=F32)
                sv = sv * vg_ref[:, cols] + bs_ref[:, g:g + 1]
                prod_ref[rows, cols] = (u_ref[rows, cols].astype(F32) * sv).astype(BF16)

    o_ref[...] = jnp.dot(prod_ref[...], wo_ref[...], preferred_element_type=F32) + h_ref[...]


def _gmlp_layer(h, norm_g, w_in, v_norm, w_s, b_s, w_out):
    T, D = h.shape
    half = w_out.shape[0]
    tm, tn = 512, 1024
    n_u = half // tn
    z, ss = pl.pallas_call(
        functools.partial(_gmlp_in_kernel, n_u_tiles=n_u),
        grid=(T // tm, 2 * half // tn),
        in_specs=[pl.BlockSpec((tm, D), lambda i, j: (i, 0)),
                  pl.BlockSpec((1, D), lambda i, j: (0, 0)),
                  pl.BlockSpec((D, tn), lambda i, j: (0, j))],
        out_specs=[pl.BlockSpec((tm, tn), lambda i, j: (i, j)),
                   pl.BlockSpec((tm, 1), lambda i, j: (i, 0))],
        out_shape=[jax.ShapeDtypeStruct((T, 2 * half), BF16), jax.ShapeDtypeStruct((T, 1), F32)],
        scratch_shapes=[pltpu.VMEM((tm, D), BF16)],
        compiler_params=_cparams(("parallel", "arbitrary")),
        name="gmlp_in",
    )(h, norm_g.reshape(1, D), w_in.astype(BF16))

    tn2 = 512
    n_chunks = tm // GMLP_CHUNK
    return pl.pallas_call(
        functools.partial(_gmlp_out_kernel, n_chunks=n_chunks, half=half),
        grid=(T // tm, D // tn2),
        in_specs=[pl.BlockSpec((tm, half), lambda i, j: (i, 0)),
                  pl.BlockSpec((tm, half), lambda i, j: (i, 1)),
                  pl.BlockSpec((n_chunks, 1, GMLP_CHUNK), lambda i, j: (i, 0, 0)),
                  pl.BlockSpec((1, half), lambda i, j: (0, 0)),
                  pl.BlockSpec((GMLP_GROUPS, GMLP_CHUNK, GMLP_CHUNK), lambda i, j: (0, 0, 0)),
                  pl.BlockSpec((GMLP_CHUNK, GMLP_GROUPS), lambda i, j: (0, 0)),
                  pl.BlockSpec((half, tn2), lambda i, j: (0, j)),
                  pl.BlockSpec((tm, tn2), lambda i, j: (i, j))],
        out_specs=pl.BlockSpec((tm, tn2), lambda i, j: (i, j)),
        out_shape=jax.ShapeDtypeStruct((T, D), F32),
        scratch_shapes=[pltpu.VMEM((tm, half), BF16)],
        compiler_params=_cparams(("parallel", "arbitrary")),
        name="gmlp_out",
    )(z, z, ss.reshape(T // GMLP_CHUNK, 1, GMLP_CHUNK), v_norm.reshape(1, half), w_s, b_s.T,
      w_out.astype(BF16), h)


def _cmpx(v, i, l, desc):
    hi, lo = jnp.maximum(v[i], v[l]), jnp.minimum(v[i], v[l])
    v[i], v[l] = (hi, lo) if desc else (lo, hi)


def _bitonic_sort_desc(v):
    n = len(v)
    k = 2
    while k <= n:
        j = k // 2
        while j >= 1:
            for i in range(n):
                l = i ^ j
                if l > i:
                    _cmpx(v, i, l, (i & k) == 0)
            j //= 2
        k *= 2


def _bitonic_merge_desc(v):
    n = len(v)
    j = n // 2
    while j >= 1:
        for i in range(n):
            l = i ^ j
            if l > i:
                _cmpx(v, i, l, True)
        j //= 2


def _top16_sorted(rows):
    v = list(rows)
    _bitonic_sort_desc(v)
    for shift in (4, 2, 1):
        w = [pltpu.roll(x, shift, axis=0) for x in v]
        v = [jnp.maximum(v[i], w[PEER_TOPK - 1 - i]) for i in range(PEER_TOPK)]
        _bitonic_merge_desc(v)
    return v


def _peer_score_kernel(h_ref, g_ref, wq_ref, keys_ref, xn_ref, s_ref, st_ref, t2_ref):
    tm = h_ref.shape[0]
    xn = _rms_rows(h_ref[...], g_ref[...]).astype(BF16)
    xn_ref[...] = xn
    q = jnp.dot(xn, wq_ref[...], preferred_element_type=F32).astype(BF16)
    sub = lax.broadcasted_iota(jnp.int32, (SUBLANES, tm), 0)

    def pack(lst):
        out = lst[0]
        for s in range(1, SUBLANES):
            out = jnp.where(sub == s, lst[s], out)
        return out

    for h in range(PEER_HEADS):
        tops = []
        for p in range(2):
            hp = 2 * h + p
            qs = q[:, hp * PEER_KEY_DIM:(hp + 1) * PEER_KEY_DIM]
            s_t = lax.dot_general(keys_ref[hp], qs, _NT, preferred_element_type=F32)
            s_ref[hp] = s_t
            tops.append(_top16_sorted([s_t[SUBLANES * r:SUBLANES * (r + 1), :] for r in range(16)]))
        t1, t2 = tops
        t2lo, t2hi, t1hi = pack(t2[0:8]), pack(t2[8:16]), pack(t1[8:16])
        cands = [t1[0] + t2lo, t1[0] + t2hi] + [t1[a] + t2lo for a in range(1, 8)] + [t1hi + t2[0]]
        cands += [jnp.full((SUBLANES, tm), -jnp.inf, F32)] * (16 - len(cands))
        top = _top16_sorted(cands)
        z = jnp.ones((SUBLANES, tm), F32)
        for k in range(1, PEER_TOPK):
            z = z + jnp.exp(top[k] - top[0])
        for k, val in enumerate((top[PEER_TOPK - 1], t1[0], t2[0], z)):
            st_ref[pl.ds(k * PEER_HEADS + h, 1), :] = val[0:1, :]
        t2_ref[h, 0:SUBLANES, :] = t2lo
        t2_ref[h, SUBLANES:PEER_TOPK, :] = t2hi


def _peer_main_kernel(xn_ref, u_ref, vt_ref, s_ref, st_ref, t2_ref, h_ref, o_ref, acc_ref, e_ref, tau_ref,
                      hid_ref, w_ref, *, rows_per_tile, n_tiles):
    jj = pl.program_id(1)
    tm = xn_ref.shape[0]
    te = hid_ref.shape[1]
    rc = 16

    @pl.when(jj == 0)
    def _():
        acc_ref[...] = jnp.zeros_like(acc_ref)
        hid_ref[...] = jnp.zeros_like(hid_ref)
        w_ref[...] = jnp.zeros_like(w_ref)
        for h in range(PEER_HEADS):
            thr = st_ref[h:h + 1, :]
            m1 = st_ref[PEER_HEADS + h:PEER_HEADS + h + 1, :]
            m2 = st_ref[2 * PEER_HEADS + h:2 * PEER_HEADS + h + 1, :]
            zz = st_ref[3 * PEER_HEADS + h:3 * PEER_HEADS + h + 1, :]
            s1 = s_ref[2 * h]
            e_ref[2 * h] = jnp.exp(s1 - m1)
            e_ref[2 * h + 1] = jnp.exp(s_ref[2 * h + 1] - m2) / zz
            tau = jnp.full(s1.shape, jnp.inf, F32)
            for b in range(PEER_TOPK):
                t2b = t2_ref[h, b:b + 1, :]
                tau = jnp.where(s1 + t2b >= thr, t2b, tau)
            tau_ref[h] = tau

    def project(slot, half):
        hid_ref[slot] = lax.dot_general(u_ref[half * te:(half + 1) * te, :], xn_ref[...], _NT,
                                        preferred_element_type=F32)

    def gate_tile(slot, tile):
        tile = jnp.clip(tile, 0, n_tiles - 1)
        for r in range(rows_per_tile):
            i = tile * rows_per_tile + r
            taus = [tau_ref[h, pl.ds(i, 1), :] for h in range(PEER_HEADS)]
            e1s = [e_ref[2 * h, pl.ds(i, 1), :] for h in range(PEER_HEADS)]
            for c in range(PEER_NKEYS // rc):
                rows = slice(c * rc, (c + 1) * rc)
                gate = jnp.zeros((rc, tm), F32)
                for h in range(PEER_HEADS):
                    gate = gate + jnp.where(s_ref[2 * h + 1, rows, :] >= taus[h],
                                            e_ref[2 * h + 1, rows, :], 0.0) * e1s[h]
                out_rows = slice(r * PEER_NKEYS + c * rc, r * PEER_NKEYS + (c + 1) * rc)
                w_ref[slot, out_rows, :] = (_gelu(hid_ref[slot, out_rows, :]) * gate).astype(BF16)

    def accumulate(slot, half):
        acc_ref[...] += jnp.dot(vt_ref[:, half * te:(half + 1) * te], w_ref[slot],
                                preferred_element_type=F32)

    for half in range(2):
        project(half, half)
        gate_tile(half, 2 * jj + half)
        accumulate(half, half)

    @pl.when(jj == pl.num_programs(1) - 1)
    def _():
        o_ref[...] = acc_ref[...].T + h_ref[...]


def _peer_layer(h, norm_g, w_q, keys, u_tab, v_tab):
    T, D = h.shape
    n_exp = u_tab.shape[0]
    n_hp = 2 * PEER_HEADS
    tm1 = 256
    xn, s_t, stats, t2 = pl.pallas_call(
        _peer_score_kernel,
        grid=(T // tm1,),
        in_specs=[pl.BlockSpec((tm1, D), lambda i: (i, 0)),
                  pl.BlockSpec((1, D), lambda i: (0, 0)),
                  pl.BlockSpec((D, n_hp * PEER_KEY_DIM), lambda i: (0, 0)),
                  pl.BlockSpec((n_hp, PEER_NKEYS, PEER_KEY_DIM), lambda i: (0, 0, 0))],
        out_specs=[pl.BlockSpec((tm1, D), lambda i: (i, 0)),
                   pl.BlockSpec((n_hp, PEER_NKEYS, tm1), lambda i: (0, 0, i)),
                   pl.BlockSpec((4 * PEER_HEADS, tm1), lambda i: (0, i)),
                   pl.BlockSpec((PEER_HEADS, PEER_TOPK, tm1), lambda i: (0, 0, i))],
        out_shape=[jax.ShapeDtypeStruct((T, D), BF16),
                   jax.ShapeDtypeStruct((n_hp, PEER_NKEYS, T), F32),
                   jax.ShapeDtypeStruct((4 * PEER_HEADS, T), F32),
                   jax.ShapeDtypeStruct((PEER_HEADS, PEER_TOPK, T), F32)],
        compiler_params=_cparams(("parallel",)),
        name="peer_score",
    )(h, norm_g.reshape(1, D), w_q.astype(BF16),
      keys.reshape(n_hp, PEER_NKEYS, PEER_KEY_DIM).astype(BF16))

    tm, te = 512, 512
    n_tiles = n_exp // te
    n_pairs = n_tiles // 2
    return pl.pallas_call(
        functools.partial(_peer_main_kernel, rows_per_tile=te // PEER_NKEYS, n_tiles=n_tiles),
        grid=(T // tm, n_pairs),
        in_specs=[pl.BlockSpec((tm, D), lambda i, j: (i, 0)),
                  pl.BlockSpec((2 * te, D), lambda i, j: (j, 0)),
                  pl.BlockSpec((D, 2 * te), lambda i, j: (0, j)),
                  pl.BlockSpec((n_hp, PEER_NKEYS, tm), lambda i, j: (0, 0, i)),
                  pl.BlockSpec((4 * PEER_HEADS, tm), lambda i, j: (0, i)),
                  pl.BlockSpec((PEER_HEADS, PEER_TOPK, tm), lambda i, j: (0, 0, i)),
                  pl.BlockSpec((tm, D), lambda i, j: (i, 0))],
        out_specs=pl.BlockSpec((tm, D), lambda i, j: (i, 0)),
        out_shape=jax.ShapeDtypeStruct((T, D), F32),
        scratch_shapes=[pltpu.VMEM((D, tm), F32),
                        pltpu.VMEM((n_hp, PEER_NKEYS, tm), F32),
                        pltpu.VMEM((PEER_HEADS, PEER_NKEYS, tm), F32),
                        pltpu.VMEM((2, te, tm), F32),
                        pltpu.VMEM((2, te, tm), BF16)],
        compiler_params=_cparams(("parallel", "arbitrary")),
        name="peer_main",
    )(xn, u_tab.astype(BF16), v_tab.T.astype(BF16), s_t, stats, t2, h)


def _kv_proj_kernel(h_ref, g_ref, w_ref, wvt_ref, kg_ref, raw_ref, ks_ref, kw_ref, vt_ref, *, tiles_per_seq):
    G, HD = NSA_KV_GROUPS, NSA_HEAD_DIM
    tk = h_ref.shape[0]
    xn = _rms_rows(h_ref[...], g_ref[...]).astype(BF16)
    y = jnp.dot(xn, w_ref[...], preferred_element_type=F32)
    col = lambda six, c: y[:, (six * G + c) * HD:(six * G + c + 1) * HD]
    nsbp = ks_ref.shape[2] - HD
    blk = lax.rem(pl.program_id(0), tiles_per_seq) * (tk // SLC_BLOCK) \
        + lax.broadcasted_iota(jnp.int32, (tk, nsbp), 0) // SLC_BLOCK
    onehot = jnp.where(lax.broadcasted_iota(jnp.int32, (tk, nsbp), 1) == blk, 1.0, 0.0).astype(BF16)
    for c in range(G):
        raw_ref[c] = col(0, c)
        raw_ref[G + c] = col(1, c)
        ks_ref[c, :, 0:HD] = _rms_rows(col(2, c), kg_ref[1:2, :]).astype(BF16)
        ks_ref[c, :, HD:] = onehot
        kw_ref[c] = _rms_rows(col(3, c), kg_ref[2:3, :]).astype(BF16)
    y_t = lax.dot_general(wvt_ref[...], xn, _NT, preferred_element_type=F32)
    for c in range(2 * G):
        vt_ref[c, 0] = y_t[c * HD:(c + 1) * HD, :].astype(BF16)


def _compress_kernel(x_ref, pe_ref, w1_ref, b1_ref, w2_ref, b2_ref, gain_ref, o_ref, *, is_key):
    x = x_ref[0]
    n, half_k = x.shape
    xa = (x + pe_ref[0, 0:1, :]).astype(BF16)
    xb = (x + pe_ref[0, 1:2, :]).astype(BF16)
    a = jnp.dot(xa, w1_ref[0, :half_k, :], preferred_element_type=F32)
    b = jnp.dot(xb, w1_ref[0, half_k:, :], preferred_element_type=F32)
    hid = _gelu(a + pltpu.roll(b, n - 1, axis=0) + b1_ref[0])
    y = jnp.dot(hid.astype(BF16), w2_ref[0], preferred_element_type=F32) + b2_ref[0]
    if is_key:
        o_ref[0] = _rms_rows(y, gain_ref[...]).astype(BF16)
    else:
        o_ref[0] = y.T.astype(BF16)


def _nsa_shared_kv(h, B, S, kv_norm, kv_w, kv_k_norm, phi_pe, phi_w1, phi_b1, phi_w2, phi_b2, tk):
    T, D = h.shape
    G, HD = NSA_KV_GROUPS, NSA_HEAD_DIM
    GW = G * HD
    w = kv_w.astype(BF16)
    sl = lambda six: w[:, six * GW:(six + 1) * GW]
    wk = jnp.concatenate([sl(0), sl(1), sl(2), sl(4)], axis=1)
    wvt = jnp.concatenate([sl(3), sl(5)], axis=1).T
    nsbp = -(-(S // SLC_BLOCK) // LANES) * LANES
    raw, ks, kw, vt = pl.pallas_call(
        functools.partial(_kv_proj_kernel, tiles_per_seq=S // tk),
        grid=(T // tk,),
        in_specs=[pl.BlockSpec((tk, D), lambda i: (i, 0)),
                  pl.BlockSpec((1, D), lambda i: (0, 0)),
                  pl.BlockSpec((D, 4 * GW), lambda i: (0, 0)),
                  pl.BlockSpec((2 * GW, D), lambda i: (0, 0)),
                  pl.BlockSpec((NSA_BRANCHES, HD), lambda i: (0, 0))],
        out_specs=[pl.BlockSpec((2 * G, tk, HD), lambda i: (0, i, 0)),
                   pl.BlockSpec((G, tk, HD + nsbp), lambda i: (0, i, 0)),
                   pl.BlockSpec((G, tk, HD), lambda i: (0, i, 0)),
                   pl.BlockSpec((2 * G, 1, HD, tk), lambda i: (0, i, 0, 0))],
        out_shape=[jax.ShapeDtypeStruct((2 * G, T, HD), F32),
                   jax.ShapeDtypeStruct((G, T, HD + nsbp), BF16),
                   jax.ShapeDtypeStruct((G, T, HD), BF16),
                   jax.ShapeDtypeStruct((2 * G, T // tk, HD, tk), BF16)],
        compiler_params=_cparams(("parallel",)),
        name="nsa_kv_proj",
    )(h, kv_norm.reshape(1, D), wk, wvt, kv_k_norm)

    ncb = S // CMP_STRIDE
    half_k = CMP_STRIDE * HD
    hidden = phi_w1.shape[-1]
    raw16 = raw.reshape(2 * G, T // CMP_STRIDE, half_k)
    pe = phi_pe.reshape(2, CMP_BLOCK // CMP_STRIDE, half_k)
    outs = []
    for kind in range(2):
        is_key = kind == 0
        o_block = (1, ncb, HD) if is_key else (1, HD, ncb)
        outs.append(pl.pallas_call(
            functools.partial(_compress_kernel, is_key=is_key),
            grid=(B, G),
            in_specs=[pl.BlockSpec((1, ncb, half_k), lambda b, g, kind=kind: (kind * G + g, b, 0)),
                      pl.BlockSpec((1, 2, half_k), lambda b, g, kind=kind: (kind, 0, 0)),
                      pl.BlockSpec((1, 2 * half_k, hidden), lambda b, g, kind=kind: (kind, 0, 0)),
                      pl.BlockSpec((1, 1, hidden), lambda b, g, kind=kind: (kind, 0, 0)),
                      pl.BlockSpec((1, hidden, HD), lambda b, g, kind=kind: (kind, 0, 0)),
                      pl.BlockSpec((1, 1, HD), lambda b, g, kind=kind: (kind, 0, 0)),
                      pl.BlockSpec((1, HD), lambda b, g: (0, 0))],
            out_specs=pl.BlockSpec(o_block, lambda b, g: (b * G + g, 0, 0)),
            out_shape=jax.ShapeDtypeStruct((B * G,) + o_block[1:], BF16),
            compiler_params=_cparams(("parallel", "parallel")),
            name="nsa_compress_k" if is_key else "nsa_compress_v",
        )(raw16, pe, phi_w1.astype(BF16), phi_b1.reshape(2, 1, hidden), phi_w2.astype(BF16),
          phi_b2.reshape(2, 1, HD), kv_k_norm[0:1]))
    return outs[0], outs[1], ks, kw, vt


def _nsa_in_kernel(h_ref, g_ref, wq_ref, wgt_ref, qg_ref, q_ref, gate_ref):
    HD = NSA_HEAD_DIM
    xn = _rms_rows(h_ref[...], g_ref[...]).astype(BF16)
    y = jnp.dot(xn, wq_ref[...], preferred_element_type=F32)
    qscale = HD ** -0.5 * LOG2E
    for hd in range(q_ref.shape[0]):
        q_ref[hd] = (_rms_rows(y[:, hd * HD:(hd + 1) * HD], qg_ref[...]) * qscale).astype(BF16)
    g_t = lax.dot_general(wgt_ref[...], xn, _NT, preferred_element_type=F32)
    gate_ref[...] = jax.nn.sigmoid(g_t)


def _nsa_attn_kernel(slopes_ref, q_ref, gate_ref, kc_ref, vct_ref, ks_ref, vst_ref, kw_ref, vwt_ref, ovt_ref,
                     o_ref, sel_ref, qat_ref, b_ref, s_ref, p_ref, m_ref, l_ref, acc_ref, oc_ref, os_ref, *,
                     tq, hpg):
    HD = NSA_HEAD_DIM
    tk = tq
    g = pl.program_id(1)
    qi = pl.program_id(2)
    t0 = qi * tq
    ncb = kc_ref.shape[1]
    nsb = sel_ref.shape[0]
    nsbp = qat_ref.shape[0] - HD
    slopes = [slopes_ref[g * hpg + hh] for hh in range(hpg)]

    tpos_c = lax.broadcasted_iota(jnp.int32, (ncb, tq), 1) + t0
    cpos = lax.broadcasted_iota(jnp.int32, (ncb, tq), 0) * CMP_STRIDE + (CMP_BLOCK - 1)
    dist_c = tpos_c - cpos
    mask_c = dist_c >= 0
    dist_cf = dist_c.astype(F32)
    p_sum = jnp.zeros((ncb, tq), F32)
    for hh in range(hpg):
        s = lax.dot_general(kc_ref[0], q_ref[hh], _NT, preferred_element_type=F32) - slopes[hh] * dist_cf
        s = jnp.where(mask_c, s, MASK_FILL)
        e = jnp.where(mask_c, jnp.exp2(s - jnp.max(s, axis=0, keepdims=True)), 0.0)
        p = e / jnp.maximum(jnp.sum(e, axis=0, keepdims=True), TINY)
        oc_ref[hh] = jnp.dot(vct_ref[0], p.astype(BF16), preferred_element_type=F32)
        p_sum = p_sum + p

    imp = jnp.zeros((nsb, tq), F32)
    rem = p_sum
    for _ in range(3):
        part = rem.astype(BF16)
        imp = imp + jnp.dot(ovt_ref[...], part, preferred_element_type=F32)
        rem = rem - part.astype(F32)
    jb = lax.broadcasted_iota(jnp.int32, (nsb, tq), 0)
    cur = (lax.broadcasted_iota(jnp.int32, (nsb, tq), 1) + t0) // SLC_BLOCK
    forced = jnp.where(jb == 0, 1.0, jnp.where(jb == cur, 1.0, jnp.where(jb == cur - 1, 1.0, 0.0)))
    score = jnp.where(jb <= cur, imp + FORCE_BONUS * forced, -jnp.inf)
    sel_ref[...] = score

    def rank_body(jp, cnt):
        row = sel_ref[pl.ds(jp, 1), :]
        tie = jnp.where(jb > jp, 1.0, 0.0)
        return cnt + jnp.where(row > score, 1.0, jnp.where(row == score, tie, 0.0))

    n_valid = (t0 + tq) // SLC_BLOCK
    cnt = lax.fori_loop(0, n_valid, rank_body, jnp.zeros((nsb, tq), F32))
    neg_sel = jnp.where(cnt < float(N_SEL), 0.0, MASK_FILL)
    if nsbp > nsb:
        neg_sel = jnp.concatenate([neg_sel, jnp.zeros((nsbp - nsb, tq), F32)], axis=0)
    neg_sel = neg_sel.astype(BF16)

    wl = hpg * tq
    rc = 16
    kio = lax.broadcasted_iota(jnp.int32, (tk, wl), 0)
    kio_c = lax.broadcasted_iota(jnp.int32, (rc, wl), 0)
    tio_c = lax.rem(lax.broadcasted_iota(jnp.int32, (rc, wl), 1), tq)
    slope_row = jnp.concatenate([jnp.full((1, tq), slopes[hh], F32) for hh in range(hpg)], axis=1)
    for hh in range(hpg):
        qat_ref[0:HD, hh * tq:(hh + 1) * tq] = q_ref[hh].astype(F32).T.astype(BF16)
        qat_ref[HD:, hh * tq:(hh + 1) * tq] = neg_sel
    b_ref[...] = slope_row * kio.astype(F32)

    chunks = [slice(r * rc, (r + 1) * rc) for r in range(tk // rc)]

    def key_tile(k_ref, ki):
        return k_ref[0, pl.ds(pl.multiple_of(ki * tk, tk), tk), :]

    def bias_and_max(mask_kind):
        acc = None
        for r, ch in enumerate(chunks):
            t = s_ref[ch, :] + b_ref[ch, :]
            if mask_kind == "causal":
                t = t + jnp.where(kio_c + r * rc > tio_c, MASK_FILL, 0.0)
            elif mask_kind == "beyond":
                t = t + jnp.where(kio_c + r * rc <= tio_c, MASK_FILL, 0.0)
            s_ref[ch, :] = t
            acc = t if acc is None else jnp.maximum(acc, t)
        return jnp.max(acc, axis=0, keepdims=True)

    def exp_pass(shift):
        lsum = jnp.zeros((rc, wl), F32)
        for ch in chunks:
            p = jnp.exp2(s_ref[ch, :] + shift)
            p_ref[ch, :] = p.astype(BF16)
            lsum = lsum + p
        return jnp.sum(lsum, axis=0, keepdims=True)

    def first_tile(s, v_t):
        s_ref[...] = s
        m = bias_and_max("causal")
        m_ref[...] = m
        l_ref[...] = exp_pass(-m)
        acc_ref[...] = jnp.dot(v_t, p_ref[...], preferred_element_type=F32)

    def next_tile(s, back, v_t, mask_kind):
        s_ref[...] = s
        c = -slope_row * back
        m_old = m_ref[...]
        m_new = jnp.maximum(m_old, bias_and_max(mask_kind) + c)
        alpha = jnp.exp2(m_old - m_new)
        l_ref[...] = alpha * l_ref[...] + exp_pass(c - m_new)
        acc_ref[...] = alpha * acc_ref[...] + jnp.dot(v_t, p_ref[...], preferred_element_type=F32)
        m_ref[...] = m_new

    def sel_scores(ki):
        return jnp.dot(key_tile(ks_ref, ki), qat_ref[...], preferred_element_type=F32)

    def win_scores(ki):
        return jnp.dot(key_tile(kw_ref, ki), qat_ref[0:HD, :], preferred_element_type=F32)

    first_tile(sel_scores(qi), vst_ref[0, qi])

    def sel_body(step, carry):
        ki = qi - 1 - step
        next_tile(sel_scores(ki), ((step + 1) * tk).astype(F32), vst_ref[0, ki], None)
        return carry

    lax.fori_loop(0, qi, sel_body, 0)
    os_ref[...] = acc_ref[...] / jnp.maximum(l_ref[...], TINY)

    first_tile(win_scores(qi), vwt_ref[0, qi])
    for back_tiles in range(1, WINDOW // tk + 1):
        @pl.when(qi >= back_tiles)
        def _(back_tiles=back_tiles):
            ki = qi - back_tiles
            next_tile(win_scores(ki), float(back_tiles * tk), vwt_ref[0, ki],
                      "beyond" if back_tiles == WINDOW // tk else None)
    o_w = acc_ref[...] / jnp.maximum(l_ref[...], TINY)

    for hh in range(hpg):
        cols = slice(hh * tq, (hh + 1) * tq)
        gates = [gate_ref[0, hh * NSA_BRANCHES + br:hh * NSA_BRANCHES + br + 1, :] for br in range(NSA_BRANCHES)]
        o_t = gates[0] * oc_ref[hh] + gates[1] * os_ref[:, cols] + gates[2] * o_w[:, cols]
        o_ref[:, hh * HD:(hh + 1) * HD] = o_t.T.astype(BF16)


def _mm_res_kernel(a_ref, w_ref, h_ref, o_ref):
    o_ref[...] = jnp.dot(a_ref[...], w_ref[...], preferred_element_type=F32) + h_ref[...]


def _nsa_layer(h, B, S, norm_g, w_in, q_norm, w_out, k_c, v_ct, ks, kw, vt, tq):
    T, D = h.shape
    G, HD = NSA_KV_GROUPS, NSA_HEAD_DIM
    width = w_out.shape[0]
    heads = width // HD
    hpg = heads // G
    n_gate = heads * NSA_BRANCHES
    w = w_in.astype(BF16)
    tm = 512
    q, gate_t = pl.pallas_call(
        _nsa_in_kernel,
        grid=(T // tm,),
        in_specs=[pl.BlockSpec((tm, D), lambda i: (i, 0)),
                  pl.BlockSpec((1, D), lambda i: (0, 0)),
                  pl.BlockSpec((D, width), lambda i: (0, 0)),
                  pl.BlockSpec((n_gate, D), lambda i: (0, 0)),
                  pl.BlockSpec((1, HD), lambda i: (0, 0))],
        out_specs=[pl.BlockSpec((heads, tm, HD), lambda i: (0, i, 0)),
                   pl.BlockSpec((n_gate, tm), lambda i: (0, i))],
        out_shape=[jax.ShapeDtypeStruct((heads, T, HD), BF16), jax.ShapeDtypeStruct((n_gate, T), F32)],
        compiler_params=_cparams(("parallel",)),
        name="nsa_in",
    )(h, norm_g.reshape(1, D), w[:, :width], w[:, width:].T, q_norm.reshape(1, HD))

    ncb = S // CMP_STRIDE
    nsb = S // SLC_BLOCK
    n_cmp = (S - CMP_BLOCK) // CMP_STRIDE + 1
    c_start = jnp.arange(ncb) * CMP_STRIDE
    c_pos = c_start + CMP_BLOCK - 1
    s_start = jnp.arange(nsb) * SLC_BLOCK
    ov_t = ((c_start[None, :] < s_start[:, None] + SLC_BLOCK) & (c_pos[None, :] >= s_start[:, None])
            & (jnp.arange(ncb)[None, :] < n_cmp)).astype(BF16)
    h_idx = jnp.arange(1, heads + 1, dtype=F32)
    slopes = 2.0 ** (-8.0 * h_idx / heads) * LOG2E
    nq = S // tq
    w_sel = ks.shape[2]
    o = pl.pallas_call(
        functools.partial(_nsa_attn_kernel, tq=tq, hpg=hpg),
        grid_spec=pltpu.PrefetchScalarGridSpec(
            num_scalar_prefetch=1,
            grid=(B, G, nq),
            in_specs=[pl.BlockSpec((hpg, tq, HD), lambda b, g, qi, sl: (g, b * nq + qi, 0)),
                      pl.BlockSpec((1, hpg * NSA_BRANCHES, tq), lambda b, g, qi, sl: (g, 0, b * nq + qi)),
                      pl.BlockSpec((1, ncb, HD), lambda b, g, qi, sl: (b * G + g, 0, 0)),
                      pl.BlockSpec((1, HD, ncb), lambda b, g, qi, sl: (b * G + g, 0, 0)),
                      pl.BlockSpec((1, S, w_sel), lambda b, g, qi, sl: (g, b, 0)),
                      pl.BlockSpec((1, nq, HD, tq), lambda b, g, qi, sl: (g, b, 0, 0)),
                      pl.BlockSpec((1, S, HD), lambda b, g, qi, sl: (g, b, 0)),
                      pl.BlockSpec((1, nq, HD, tq), lambda b, g, qi, sl: (G + g, b, 0, 0)),
                      pl.BlockSpec((nsb, ncb), lambda b, g, qi, sl: (0, 0))],
            out_specs=pl.BlockSpec((tq, hpg * HD), lambda b, g, qi, sl: (b * nq + qi, g)),
            scratch_shapes=[pltpu.VMEM((nsb, tq), F32),
                            pltpu.VMEM((w_sel, hpg * tq), BF16),
                            pltpu.VMEM((tq, hpg * tq), F32),
                            pltpu.VMEM((tq, hpg * tq), F32),
                            pltpu.VMEM((tq, hpg * tq), BF16),
                            pltpu.VMEM((1, hpg * tq), F32),
                            pltpu.VMEM((1, hpg * tq), F32),
                            pltpu.VMEM((HD, hpg * tq), F32),
                            pltpu.VMEM((hpg, HD, tq), F32),
                            pltpu.VMEM((HD, hpg * tq), F32)]),
        out_shape=jax.ShapeDtypeStruct((T, width), BF16),
        compiler_params=_cparams(("parallel", "parallel", "arbitrary")),
        name="nsa_attn",
    )(slopes, q, gate_t.reshape(G, hpg * NSA_BRANCHES, T), k_c, v_ct, ks, vt, kw, vt, ov_t)

    return pl.pallas_call(
        _mm_res_kernel,
        grid=(T // tm,),
        in_specs=[pl.BlockSpec((tm, width), lambda i: (i, 0)),
                  pl.BlockSpec((width, D), lambda i: (0, 0)),
                  pl.BlockSpec((tm, D), lambda i: (i, 0))],
        out_specs=pl.BlockSpec((tm, D), lambda i: (i, 0)),
        out_shape=jax.ShapeDtypeStruct((T, D), F32),
        compiler_params=_cparams(("parallel",)),
        name="nsa_out",
    )(o, w_out.astype(BF16), h)


def kernel(x, a_norm, a_w_in, a_v_norm, a_w_s, a_b_s, a_w_out, kv_norm, kv_w, kv_k_norm, phi_pe, phi_w1, phi_b1, phi_w2, phi_b2, b_norm, b_w_in, b_q_norm, b_w_out, ffn_norm, peer_w_q, peer_keys, peer_u, peer_v):
    B, S, D = x.shape
    depth = ffn_norm.shape[0]
    n_a = a_norm.shape[0]
    tq = 256
    h = x.reshape(B * S, D)
    shared = None
    for layer in range(depth):
        if layer < n_a:
            a = layer
            h = _gmlp_layer(h, a_norm[a], a_w_in[a], a_v_norm[a], a_w_s[a], a_b_s[a], a_w_out[a])
        else:
            if shared is None:
                shared = _nsa_shared_kv(h, B, S, kv_norm, kv_w, kv_k_norm, phi_pe, phi_w1, phi_b1, phi_w2,
                                        phi_b2, tq)
            b = layer - n_a
            h = _nsa_layer(h, B, S, b_norm[b], b_w_in[b], b_q_norm[b], b_w_out[b], *shared, tq)
        h = _peer_layer(h, ffn_norm[layer], peer_w_q[layer], peer_keys[layer], peer_u[layer], peer_v[layer])
    return h.reshape(B, S, D)
```

```python
import functools
import math

import jax
import jax.numpy as jnp
from jax import lax
from jax.experimental import pallas as pl
from jax.experimental.pallas import tpu as pltpu

F32 = jnp.float32
BF16 = jnp.bfloat16

GMLP_CHUNK = 128
GMLP_GROUPS = 8
NSA_HEAD_DIM = 128
NSA_KV_GROUPS = 4
NSA_BRANCHES = 3
CMP_BLOCK = 32
CMP_STRIDE = 16
SLC_BLOCK = 64
N_SEL = 16
WINDOW = 512
FORCE_BONUS = 1000.0
PEER_HEADS = 8
PEER_NKEYS = 128
PEER_TOPK = 16
PEER_KEY_DIM = 128
RMS_EPS = 1e-6
MASK_FILL = -1e30
TINY = float(jnp.finfo(jnp.float32).tiny)

V7X_VMEM_LIMIT_BYTES = 56 * 1024 * 1024
LANES = 128
SUBLANES = 8
LOG2E = 1.4426950408889634

_NT = (((1,), (1,)), ((), ()))


def _cparams(sem):
    return pltpu.CompilerParams(dimension_semantics=sem, vmem_limit_bytes=V7X_VMEM_LIMIT_BYTES)


def _gelu(x):
    c = math.sqrt(2.0 / math.pi)
    return x * (0.5 * (1.0 + jnp.tanh(c * (x + 0.044715 * (x * x * x)))))


def _rms_rows(x, gain):
    r = lax.rsqrt(jnp.mean(x * x, axis=-1, keepdims=True) + RMS_EPS)
    return (x * r) * gain


def _gmlp_in_kernel(h_ref, g_ref, w_ref, z_ref, ss_ref, xn_ref, *, n_u_tiles):
    j = pl.program_id(1)

    @pl.when(j == 0)
    def _():
        xn_ref[...] = _rms_rows(h_ref[...], g_ref[...]).astype(BF16)
        ss_ref[...] = jnp.zeros_like(ss_ref)

    z = _gelu(jnp.dot(xn_ref[...], w_ref[...], preferred_element_type=F32))
    z_ref[...] = z.astype(BF16)

    @pl.when(j >= n_u_tiles)
    def _():
        ss_ref[...] += jnp.sum(z * z, axis=-1, keepdims=True)


def _gmlp_out_kernel(u_ref, v_ref, ss_ref, vg_ref, ws_ref, bs_ref, wo_ref, h_ref, o_ref, prod_ref, *,
                     n_chunks, half):
    j = pl.program_id(1)
    gdim = half // GMLP_GROUPS

    @pl.when(j == 0)
    def _():
        row = lax.broadcasted_iota(jnp.int32, (GMLP_CHUNK, GMLP_CHUNK), 0)
        col = lax.broadcasted_iota(jnp.int32, (GMLP_CHUNK, GMLP_CHUNK), 1)
        tril = row >= col
        for c in range(n_chunks):
            rows = slice(c * GMLP_CHUNK, (c + 1) * GMLP_CHUNK)
            rinv = lax.rsqrt(ss_ref[c] * (1.0 / half) + RMS_EPS)
            for g in range(GMLP_GROUPS):
                cols = slice(g * gdim, (g + 1) * gdim)
                wc = (jnp.where(tril, ws_ref[g], 0.0) * rinv).astype(BF16)
                sv = jnp.dot(wc, v_ref[rows, cols], preferred_element_type=F32)
                sv = sv * vg_ref[:, cols] + bs_ref[:, g:g + 1]
                prod_ref[rows, cols] = (u_ref[rows, cols].astype(F32) * sv).astype(BF16)

    o_ref[...] = jnp.dot(prod_ref[...], wo_ref[...], preferred_element_type=F32) + h_ref[...]


def _gmlp_layer(h, norm_g, w_in, v_norm, w_s, b_s, w_out):
    T, D = h.shape
    half = w_out.shape[0]
    tm, tn = 512, 1024
    n_u = half // tn
    z, ss = pl.pallas_call(
        functools.partial(_gmlp_in_kernel, n_u_tiles=n_u),
        grid=(T // tm, 2 * half // tn),
        in_specs=[pl.BlockSpec((tm, D), lambda i, j: (i, 0)),
                  pl.BlockSpec((1, D), lambda i, j: (0, 0)),
                  pl.BlockSpec((D, tn), lambda i, j: (0, j))],
        out_specs=[pl.BlockSpec((tm, tn), lambda i, j: (i, j)),
                   pl.BlockSpec((tm, 1), lambda i, j: (i, 0))],
        out_shape=[jax.ShapeDtypeStruct((T, 2 * half), BF16), jax.ShapeDtypeStruct((T, 1), F32)],
        scratch_shapes=[pltpu.VMEM((tm, D), BF16)],
        compiler_params=_cparams(("parallel", "arbitrary")),
        name="gmlp_in",
    )(h, norm_g.reshape(1, D), w_in.astype(BF16))

    tn2 = 512
    n_chunks = tm // GMLP_CHUNK
    return pl.pallas_call(
        functools.partial(_gmlp_out_kernel, n_chunks=n_chunks, half=half),
        grid=(T // tm, D // tn2),
        in_specs=[pl.BlockSpec((tm, half), lambda i, j: (i, 0)),
                  pl.BlockSpec((tm, half), lambda i, j: (i, 1)),
                  pl.BlockSpec((n_chunks, 1, GMLP_CHUNK), lambda i, j: (i, 0, 0)),
                  pl.BlockSpec((1, half), lambda i, j: (0, 0)),
                  pl.BlockSpec((GMLP_GROUPS, GMLP_CHUNK, GMLP_CHUNK), lambda i, j: (0, 0, 0)),
                  pl.BlockSpec((GMLP_CHUNK, GMLP_GROUPS), lambda i, j: (0, 0)),
                  pl.BlockSpec((half, tn2), lambda i, j: (0, j)),
                  pl.BlockSpec((tm, tn2), lambda i, j: (i, j))],
        out_specs=pl.BlockSpec((tm, tn2), lambda i, j: (i, j)),
        out_shape=jax.ShapeDtypeStruct((T, D), F32),
        scratch_shapes=[pltpu.VMEM((tm, half), BF16)],
        compiler_params=_cparams(("parallel", "arbitrary")),
        name="gmlp_out",
    )(z, z, ss.reshape(T // GMLP_CHUNK, 1, GMLP_CHUNK), v_norm.reshape(1, half), w_s, b_s.T,
      w_out.astype(BF16), h)


def _cmpx(v, i, l, desc):
    hi, lo = jnp.maximum(v[i], v[l]), jnp.minimum(v[i], v[l])
    v[i], v[l] = (hi, lo) if desc else (lo, hi)


def _bitonic_sort_desc(v):
    n = len(v)
    k = 2
    while k <= n:
        j = k // 2
        while j >= 1:
            for i in range(n):
                l = i ^ j
                if l > i:
                    _cmpx(v, i, l, (i & k) == 0)
            j //= 2
        k *= 2


def _bitonic_merge_desc(v):
    n = len(v)
    j = n // 2
    while j >= 1:
        for i in range(n):
            l = i ^ j
            if l > i:
                _cmpx(v, i, l, True)
        j //= 2


def _top16_sorted(rows):
    v = list(rows)
    _bitonic_sort_desc(v)
    for shift in (4, 2, 1):
        w = [pltpu.roll(x, shift, axis=0) for x in v]
        v = [jnp.maximum(v[i], w[PEER_TOPK - 1 - i]) for i in range(PEER_TOPK)]
        _bitonic_merge_desc(v)
    return v


def _peer_score_kernel(h_ref, g_ref, wq_ref, keys_ref, xn_ref, s_ref, st_ref, t2_ref):
    tm = h_ref.shape[0]
    xn = _rms_rows(h_ref[...], g_ref[...]).astype(BF16)
    xn_ref[...] = xn
    q = jnp.dot(xn, wq_ref[...], preferred_element_type=F32).astype(BF16)
    sub = lax.broadcasted_iota(jnp.int32, (SUBLANES, tm), 0)

    def pack(lst):
        out = lst[0]
        for s in range(1, SUBLANES):
            out = jnp.where(sub == s, lst[s], out)
        return out

    for h in range(PEER_HEADS):
        tops = []
        for p in range(2):
            hp = 2 * h + p
            qs = q[:, hp * PEER_KEY_DIM:(hp + 1) * PEER_KEY_DIM]
            s_t = lax.dot_general(keys_ref[hp], qs, _NT, preferred_element_type=F32)
            s_ref[hp] = s_t
            tops.append(_top16_sorted([s_t[SUBLANES * r:SUBLANES * (r + 1), :] for r in range(16)]))
        t1, t2 = tops
        t2lo, t2hi, t1hi = pack(t2[0:8]), pack(t2[8:16]), pack(t1[8:16])
        cands = [t1[0] + t2lo, t1[0] + t2hi] + [t1[a] + t2lo for a in range(1, 8)] + [t1hi + t2[0]]
        cands += [jnp.full((SUBLANES, tm), -jnp.inf, F32)] * (16 - len(cands))
        top = _top16_sorted(cands)
        z = jnp.ones((SUBLANES, tm), F32)
        for k in range(1, PEER_TOPK):
            z = z + jnp.exp(top[k] - top[0])
        for k, val in enumerate((top[PEER_TOPK - 1], t1[0], t2[0], z)):
            st_ref[pl.ds(k * PEER_HEADS + h, 1), :] = val[0:1, :]
        t2_ref[h, 0:SUBLANES, :] = t2lo
        t2_ref[h, SUBLANES:PEER_TOPK, :] = t2hi


def _peer_main_kernel(xn_ref, u_ref, vt_ref, s_ref, st_ref, t2_ref, h_ref, o_ref, acc_ref, e_ref, tau_ref,
                      hid_ref, w_ref, *, rows_per_tile):
    jj = pl.program_id(1)
    tm = xn_ref.shape[0]
    te = hid_ref.shape[1]
    rc = 16

    @pl.when(jj == 0)
    def _():
        acc_ref[...] = jnp.zeros_like(acc_ref)
        hid_ref[...] = jnp.zeros_like(hid_ref)
        w_ref[...] = jnp.zeros_like(w_ref)
        for h in range(PEER_HEADS):
            thr = st_ref[h:h + 1, :]
            m1 = st_ref[PEER_HEADS + h:PEER_HEADS + h + 1, :]
            m2 = st_ref[2 * PEER_HEADS + h:2 * PEER_HEADS + h + 1, :]
            zz = st_ref[3 * PEER_HEADS + h:3 * PEER_HEADS + h + 1, :]
            s1 = s_ref[2 * h]
            e_ref[2 * h] = jnp.exp(s1 - m1)
            e_ref[2 * h + 1] = jnp.exp(s_ref[2 * h + 1] - m2) / zz
            tau = jnp.full(s1.shape, jnp.inf, F32)
            for b in range(PEER_TOPK):
                t2b = t2_ref[h, b:b + 1, :]
                tau = jnp.where(s1 + t2b >= thr, t2b, tau)
            tau_ref[h] = tau

    tn = 256
    n_pieces = tm // tn

    def project_pieces(slot):
        def piece(n):
            hid_ref[slot, :, n * tn:(n + 1) * tn] = lax.dot_general(
                u_ref[slot * te:(slot + 1) * te, :], xn_ref[n * tn:(n + 1) * tn, :], _NT,
                preferred_element_type=F32)
        return [functools.partial(piece, n) for n in range(n_pieces)]

    def gate_chunks(slot):
        tile = 2 * jj + slot
        out = []
        for r in range(rows_per_tile):
            i = tile * rows_per_tile + r
            taus = [tau_ref[h, pl.ds(i, 1), :] for h in range(PEER_HEADS)]
            e1s = [e_ref[2 * h, pl.ds(i, 1), :] for h in range(PEER_HEADS)]
            for c in range(PEER_NKEYS // rc):
                def chunk(r=r, c=c, taus=taus, e1s=e1s):
                    rows = slice(c * rc, (c + 1) * rc)
                    gate = jnp.zeros((rc, tm), F32)
                    for h in range(PEER_HEADS):
                        gate = gate + jnp.where(s_ref[2 * h + 1, rows, :] >= taus[h],
                                                e_ref[2 * h + 1, rows, :], 0.0) * e1s[h]
                    out_rows = slice(r * PEER_NKEYS + c * rc, r * PEER_NKEYS + (c + 1) * rc)
                    w_ref[slot, out_rows, :] = (_gelu(hid_ref[slot, out_rows, :]) * gate).astype(BF16)
                out.append(chunk)
        return out

    def accumulate_pieces(slot):
        def piece(n):
            acc_ref[:, n * tn:(n + 1) * tn] += jnp.dot(vt_ref[:, slot * te:(slot + 1) * te],
                                                       w_ref[slot, :, n * tn:(n + 1) * tn],
                                                       preferred_element_type=F32)
        return [functools.partial(piece, n) for n in range(n_pieces)]

    def emit_interleaved(many, few):
        stride = len(many) // len(few)
        for k, f in enumerate(many):
            if k % stride == 0 and k // stride < len(few):
                few[k // stride]()
            f()

    for f in project_pieces(0):
        f()
    emit_interleaved(gate_chunks(0), project_pieces(1))
    emit_interleaved(gate_chunks(1), accumulate_pieces(0))
    for f in accumulate_pieces(1):
        f()

    @pl.when(jj == pl.num_programs(1) - 1)
    def _():
        o_ref[...] = acc_ref[...].T + h_ref[...]


def _peer_layer(h, norm_g, w_q, keys, u_tab, v_tab):
    T, D = h.shape
    n_exp = u_tab.shape[0]
    n_hp = 2 * PEER_HEADS
    tm1 = 256
    xn, s_t, stats, t2 = pl.pallas_call(
        _peer_score_kernel,
        grid=(T // tm1,),
        in_specs=[pl.BlockSpec((tm1, D), lambda i: (i, 0)),
                  pl.BlockSpec((1, D), lambda i: (0, 0)),
                  pl.BlockSpec((D, n_hp * PEER_KEY_DIM), lambda i: (0, 0)),
                  pl.BlockSpec((n_hp, PEER_NKEYS, PEER_KEY_DIM), lambda i: (0, 0, 0))],
        out_specs=[pl.BlockSpec((tm1, D), lambda i: (i, 0)),
                   pl.BlockSpec((n_hp, PEER_NKEYS, tm1), lambda i: (0, 0, i)),
                   pl.BlockSpec((4 * PEER_HEADS, tm1), lambda i: (0, i)),
                   pl.BlockSpec((PEER_HEADS, PEER_TOPK, tm1), lambda i: (0, 0, i))],
        out_shape=[jax.ShapeDtypeStruct((T, D), BF16),
                   jax.ShapeDtypeStruct((n_hp, PEER_NKEYS, T), F32),
                   jax.ShapeDtypeStruct((4 * PEER_HEADS, T), F32),
                   jax.ShapeDtypeStruct((PEER_HEADS, PEER_TOPK, T), F32)],
        compiler_params=_cparams(("parallel",)),
        name="peer_score",
    )(h, norm_g.reshape(1, D), w_q.astype(BF16),
      keys.reshape(n_hp, PEER_NKEYS, PEER_KEY_DIM).astype(BF16))

    tm, te = 512, 512
    n_tiles = n_exp // te
    n_pairs = n_tiles // 2
    once = pl.Buffered(1)
    return pl.pallas_call(
        functools.partial(_peer_main_kernel, rows_per_tile=te // PEER_NKEYS),
        grid=(T // tm, n_pairs),
        in_specs=[pl.BlockSpec((tm, D), lambda i, j: (i, 0), pipeline_mode=once),
                  pl.BlockSpec((2 * te, D), lambda i, j: (j, 0)),
                  pl.BlockSpec((D, 2 * te), lambda i, j: (0, j)),
                  pl.BlockSpec((n_hp, PEER_NKEYS, tm), lambda i, j: (0, 0, i), pipeline_mode=once),
                  pl.BlockSpec((4 * PEER_HEADS, tm), lambda i, j: (0, i), pipeline_mode=once),
                  pl.BlockSpec((PEER_HEADS, PEER_TOPK, tm), lambda i, j: (0, 0, i), pipeline_mode=once),
                  pl.BlockSpec((tm, D), lambda i, j: (i, 0), pipeline_mode=once)],
        out_specs=pl.BlockSpec((tm, D), lambda i, j: (i, 0)),
        out_shape=jax.ShapeDtypeStruct((T, D), F32),
        scratch_shapes=[pltpu.VMEM((D, tm), F32),
                        pltpu.VMEM((n_hp, PEER_NKEYS, tm), F32),
                        pltpu.VMEM((PEER_HEADS, PEER_NKEYS, tm), F32),
                        pltpu.VMEM((2, te, tm), F32),
                        pltpu.VMEM((2, te, tm), BF16)],
        compiler_params=_cparams(("parallel", "arbitrary")),
        name="peer_main",
    )(xn, u_tab.astype(BF16), v_tab.T.astype(BF16), s_t, stats, t2, h)


def _kv_proj_kernel(h_ref, g_ref, w_ref, wvt_ref, kg_ref, raw_ref, ks_ref, kw_ref, vt_ref, *, tiles_per_seq):
    G, HD = NSA_KV_GROUPS, NSA_HEAD_DIM
    tk = h_ref.shape[0]
    xn = _rms_rows(h_ref[...], g_ref[...]).astype(BF16)
    y = jnp.dot(xn, w_ref[...], preferred_element_type=F32)
    col = lambda six, c: y[:, (six * G + c) * HD:(six * G + c + 1) * HD]
    nsbp = ks_ref.shape[2] - HD
    blk = lax.rem(pl.program_id(0), tiles_per_seq) * (tk // SLC_BLOCK) \
        + lax.broadcasted_iota(jnp.int32, (tk, nsbp), 0) // SLC_BLOCK
    onehot = jnp.where(lax.broadcasted_iota(jnp.int32, (tk, nsbp), 1) == blk, 1.0, 0.0).astype(BF16)
    for c in range(G):
        raw_ref[c] = col(0, c)
        raw_ref[G + c] = col(1, c)
        ks_ref[c, :, 0:HD] = _rms_rows(col(2, c), kg_ref[1:2, :]).astype(BF16)
        ks_ref[c, :, HD:] = onehot
        kw_ref[c] = _rms_rows(col(3, c), kg_ref[2:3, :]).astype(BF16)
    y_t = lax.dot_general(wvt_ref[...], xn, _NT, preferred_element_type=F32)
    for c in range(2 * G):
        vt_ref[c, 0] = y_t[c * HD:(c + 1) * HD, :].astype(BF16)


def _compress_kernel(x_ref, pe_ref, w1_ref, b1_ref, w2_ref, b2_ref, gain_ref, o_ref, *, is_key):
    x = x_ref[0]
    n, half_k = x.shape
    xa = (x + pe_ref[0, 0:1, :]).astype(BF16)
    xb = (x + pe_ref[0, 1:2, :]).astype(BF16)
    a = jnp.dot(xa, w1_ref[0, :half_k, :], preferred_element_type=F32)
    b = jnp.dot(xb, w1_ref[0, half_k:, :], preferred_element_type=F32)
    hid = _gelu(a + pltpu.roll(b, n - 1, axis=0) + b1_ref[0])
    y = jnp.dot(hid.astype(BF16), w2_ref[0], preferred_element_type=F32) + b2_ref[0]
    if is_key:
        o_ref[0] = _rms_rows(y, gain_ref[...]).astype(BF16)
    else:
        o_ref[0] = y.T.astype(BF16)


def _nsa_shared_kv(h, B, S, kv_norm, kv_w, kv_k_norm, phi_pe, phi_w1, phi_b1, phi_w2, phi_b2, tk):
    T, D = h.shape
    G, HD = NSA_KV_GROUPS, NSA_HEAD_DIM
    GW = G * HD
    w = kv_w.astype(BF16)
    sl = lambda six: w[:, six * GW:(six + 1) * GW]
    wk = jnp.concatenate([sl(0), sl(1), sl(2), sl(4)], axis=1)
    wvt = jnp.concatenate([sl(3), sl(5)], axis=1).T
    nsbp = -(-(S // SLC_BLOCK) // LANES) * LANES
    raw, ks, kw, vt = pl.pallas_call(
        functools.partial(_kv_proj_kernel, tiles_per_seq=S // tk),
        grid=(T // tk,),
        in_specs=[pl.BlockSpec((tk, D), lambda i: (i, 0)),
                  pl.BlockSpec((1, D), lambda i: (0, 0)),
                  pl.BlockSpec((D, 4 * GW), lambda i: (0, 0)),
                  pl.BlockSpec((2 * GW, D), lambda i: (0, 0)),
                  pl.BlockSpec((NSA_BRANCHES, HD), lambda i: (0, 0))],
        out_specs=[pl.BlockSpec((2 * G, tk, HD), lambda i: (0, i, 0)),
                   pl.BlockSpec((G, tk, HD + nsbp), lambda i: (0, i, 0)),
                   pl.BlockSpec((G, tk, HD), lambda i: (0, i, 0)),
                   pl.BlockSpec((2 * G, 1, HD, tk), lambda i: (0, i, 0, 0))],
        out_shape=[jax.ShapeDtypeStruct((2 * G, T, HD), F32),
                   jax.ShapeDtypeStruct((G, T, HD + nsbp), BF16),
                   jax.ShapeDtypeStruct((G, T, HD), BF16),
                   jax.ShapeDtypeStruct((2 * G, T // tk, HD, tk), BF16)],
        compiler_params=_cparams(("parallel",)),
        name="nsa_kv_proj",
    )(h, kv_norm.reshape(1, D), wk, wvt, kv_k_norm)

    ncb = S // CMP_STRIDE
    half_k = CMP_STRIDE * HD
    hidden = phi_w1.shape[-1]
    raw16 = raw.reshape(2 * G, T // CMP_STRIDE, half_k)
    pe = phi_pe.reshape(2, CMP_BLOCK // CMP_STRIDE, half_k)
    outs = []
    for kind in range(2):
        is_key = kind == 0
        o_block = (1, ncb, HD) if is_key else (1, HD, ncb)
        outs.append(pl.pallas_call(
            functools.partial(_compress_kernel, is_key=is_key),
            grid=(B, G),
            in_specs=[pl.BlockSpec((1, ncb, half_k), lambda b, g, kind=kind: (kind * G + g, b, 0)),
                      pl.BlockSpec((1, 2, half_k), lambda b, g, kind=kind: (kind, 0, 0)),
                      pl.BlockSpec((1, 2 * half_k, hidden), lambda b, g, kind=kind: (kind, 0, 0)),
                      pl.BlockSpec((1, 1, hidden), lambda b, g, kind=kind: (kind, 0, 0)),
                      pl.BlockSpec((1, hidden, HD), lambda b, g, kind=kind: (kind, 0, 0)),
                      pl.BlockSpec((1, 1, HD), lambda b, g, kind=kind: (kind, 0, 0)),
                      pl.BlockSpec((1, HD), lambda b, g: (0, 0))],
            out_specs=pl.BlockSpec(o_block, lambda b, g: (b * G + g, 0, 0)),
            out_shape=jax.ShapeDtypeStruct((B * G,) + o_block[1:], BF16),
            compiler_params=_cparams(("parallel", "parallel")),
            name="nsa_compress_k" if is_key else "nsa_compress_v",
        )(raw16, pe, phi_w1.astype(BF16), phi_b1.reshape(2, 1, hidden), phi_w2.astype(BF16),
          phi_b2.reshape(2, 1, HD), kv_k_norm[0:1]))
    return outs[0], outs[1], ks, kw, vt


def _nsa_in_kernel(h_ref, g_ref, wq_ref, wgt_ref, qg_ref, q_ref, gate_ref):
    HD = NSA_HEAD_DIM
    xn = _rms_rows(h_ref[...], g_ref[...]).astype(BF16)
    y = jnp.dot(xn, wq_ref[...], preferred_element_type=F32)
    qscale = HD ** -0.5 * LOG2E
    for hd in range(q_ref.shape[0]):
        q_ref[hd] = (_rms_rows(y[:, hd * HD:(hd + 1) * HD], qg_ref[...]) * qscale).astype(BF16)
    g_t = lax.dot_general(wgt_ref[...], xn, _NT, preferred_element_type=F32)
    gate_ref[...] = jax.nn.sigmoid(g_t)


def _nsa_attn_kernel(slopes_ref, q_ref, gate_ref, kc_ref, vct_ref, ks_ref, vst_ref, kw_ref, vwt_ref, ovt_ref,
                     o_ref, sel_ref, qat_ref, b_ref, s_ref, p_ref, m_ref, l_ref, acc_ref, oc_ref, os_ref, *,
                     tq, hpg):
    HD = NSA_HEAD_DIM
    tk = tq
    g = pl.program_id(1)
    qi = pl.program_id(2)
    t0 = qi * tq
    ncb = kc_ref.shape[1]
    nsb = sel_ref.shape[0]
    nsbp = qat_ref.shape[0] - HD
    slopes = [slopes_ref[g * hpg + hh] for hh in range(hpg)]

    tpos_c = lax.broadcasted_iota(jnp.int32, (ncb, tq), 1) + t0
    cpos = lax.broadcasted_iota(jnp.int32, (ncb, tq), 0) * CMP_STRIDE + (CMP_BLOCK - 1)
    dist_c = tpos_c - cpos
    mask_c = dist_c >= 0
    dist_cf = dist_c.astype(F32)
    p_sum = jnp.zeros((ncb, tq), F32)
    for hh in range(hpg):
        s = lax.dot_general(kc_ref[0], q_ref[hh], _NT, preferred_element_type=F32) - slopes[hh] * dist_cf
        s = jnp.where(mask_c, s, MASK_FILL)
        e = jnp.where(mask_c, jnp.exp2(s - jnp.max(s, axis=0, keepdims=True)), 0.0)
        p = e / jnp.maximum(jnp.sum(e, axis=0, keepdims=True), TINY)
        oc_ref[hh] = jnp.dot(vct_ref[0], p.astype(BF16), preferred_element_type=F32)
        p_sum = p_sum + p

    imp = jnp.zeros((nsb, tq), F32)
    rem = p_sum
    for _ in range(3):
        part = rem.astype(BF16)
        imp = imp + jnp.dot(ovt_ref[...], part, preferred_element_type=F32)
        rem = rem - part.astype(F32)
    jb = lax.broadcasted_iota(jnp.int32, (nsb, tq), 0)
    cur = (lax.broadcasted_iota(jnp.int32, (nsb, tq), 1) + t0) // SLC_BLOCK
    forced = jnp.where(jb == 0, 1.0, jnp.where(jb == cur, 1.0, jnp.where(jb == cur - 1, 1.0, 0.0)))
    score = jnp.where(jb <= cur, imp + FORCE_BONUS * forced, -jnp.inf)
    sel_ref[...] = score

    def rank_body(jp, cnt):
        row = sel_ref[pl.ds(jp, 1), :]
        tie = jnp.where(jb > jp, 1.0, 0.0)
        return cnt + jnp.where(row > score, 1.0, jnp.where(row == score, tie, 0.0))

    n_valid = (t0 + tq) // SLC_BLOCK
    cnt = lax.fori_loop(0, n_valid, rank_body, jnp.zeros((nsb, tq), F32))
    neg_sel = jnp.where(cnt < float(N_SEL), 0.0, MASK_FILL)
    if nsbp > nsb:
        neg_sel = jnp.concatenate([neg_sel, jnp.zeros((nsbp - nsb, tq), F32)], axis=0)
    neg_sel = neg_sel.astype(BF16)

    wl = hpg * tq
    rc = 16
    kio = lax.broadcasted_iota(jnp.int32, (tk, wl), 0)
    kio_c = lax.broadcasted_iota(jnp.int32, (rc, wl), 0)
    tio_c = lax.rem(lax.broadcasted_iota(jnp.int32, (rc, wl), 1), tq)
    slope_row = jnp.concatenate([jnp.full((1, tq), slopes[hh], F32) for hh in range(hpg)], axis=1)
    for hh in range(hpg):
        qat_ref[0:HD, hh * tq:(hh + 1) * tq] = q_ref[hh].astype(F32).T.astype(BF16)
        qat_ref[HD:, hh * tq:(hh + 1) * tq] = neg_sel
    b_ref[...] = slope_row * kio.astype(F32)

    chunks = [slice(r * rc, (r + 1) * rc) for r in range(tk // rc)]

    def key_tile(k_ref, ki):
        return k_ref[0, pl.ds(pl.multiple_of(ki * tk, tk), tk), :]

    def bias_and_max(mask_kind):
        acc = None
        for r, ch in enumerate(chunks):
            t = s_ref[ch, :] + b_ref[ch, :]
            if mask_kind == "causal":
                t = t + jnp.where(kio_c + r * rc > tio_c, MASK_FILL, 0.0)
            elif mask_kind == "beyond":
                t = t + jnp.where(kio_c + r * rc <= tio_c, MASK_FILL, 0.0)
            s_ref[ch, :] = t
            acc = t if acc is None else jnp.maximum(acc, t)
        return jnp.max(acc, axis=0, keepdims=True)

    def exp_pass(shift):
        lsum = jnp.zeros((rc, wl), F32)
        for ch in chunks:
            p = jnp.exp2(s_ref[ch, :] + shift)
            p_ref[ch, :] = p.astype(BF16)
            lsum = lsum + p
        return jnp.sum(lsum, axis=0, keepdims=True)

    def first_tile(s, v_t):
        s_ref[...] = s
        m = bias_and_max("causal")
        m_ref[...] = m
        l_ref[...] = exp_pass(-m)
        acc_ref[...] = jnp.dot(v_t, p_ref[...], preferred_element_type=F32)

    def next_tile(s, back, v_t, mask_kind):
        s_ref[...] = s
        c = -slope_row * back
        m_old = m_ref[...]
        m_new = jnp.maximum(m_old, bias_and_max(mask_kind) + c)
        alpha = jnp.exp2(m_old - m_new)
        l_ref[...] = alpha * l_ref[...] + exp_pass(c - m_new)
        acc_ref[...] = alpha * acc_ref[...] + jnp.dot(v_t, p_ref[...], preferred_element_type=F32)
        m_ref[...] = m_new

    def sel_scores(ki):
        return jnp.dot(key_tile(ks_ref, ki), qat_ref[...], preferred_element_type=F32)

    def win_scores(ki):
        return jnp.dot(key_tile(kw_ref, ki), qat_ref[0:HD, :], preferred_element_type=F32)

    first_tile(sel_scores(qi), vst_ref[0, qi])

    def sel_body(step, carry):
        ki = qi - 1 - step
        next_tile(sel_scores(ki), ((step + 1) * tk).astype(F32), vst_ref[0, ki], None)
        return carry

    lax.fori_loop(0, qi, sel_body, 0)
    os_ref[...] = acc_ref[...] / jnp.maximum(l_ref[...], TINY)

    first_tile(win_scores(qi), vwt_ref[0, qi])
    for back_tiles in range(1, WINDOW // tk + 1):
        @pl.when(qi >= back_tiles)
        def _(back_tiles=back_tiles):
            ki = qi - back_tiles
            next_tile(win_scores(ki), float(back_tiles * tk), vwt_ref[0, ki],
                      "beyond" if back_tiles == WINDOW // tk else None)
    o_w = acc_ref[...] / jnp.maximum(l_ref[...], TINY)

    for hh in range(hpg):
        cols = slice(hh * tq, (hh + 1) * tq)
        gates = [gate_ref[0, hh * NSA_BRANCHES + br:hh * NSA_BRANCHES + br + 1, :] for br in range(NSA_BRANCHES)]
        o_t = gates[0] * oc_ref[hh] + gates[1] * os_ref[:, cols] + gates[2] * o_w[:, cols]
        o_ref[:, hh * HD:(hh + 1) * HD] = o_t.T.astype(BF16)


def _mm_res_kernel(a_ref, w_ref, h_ref, o_ref):
    o_ref[...] = jnp.dot(a_ref[...], w_ref[...], preferred_element_type=F32) + h_ref[...]


def _nsa_layer(h, B, S, norm_g, w_in, q_norm, w_out, k_c, v_ct, ks, kw, vt, tq):
    T, D = h.shape
    G, HD = NSA_KV_GROUPS, NSA_HEAD_DIM
    width = w_out.shape[0]
    heads = width // HD
    hpg = heads // G
    n_gate = heads * NSA_BRANCHES
    w = w_in.astype(BF16)
    tm = 512
    q, gate_t = pl.pallas_call(
        _nsa_in_kernel,
        grid=(T // tm,),
        in_specs=[pl.BlockSpec((tm, D), lambda i: (i, 0)),
                  pl.BlockSpec((1, D), lambda i: (0, 0)),
                  pl.BlockSpec((D, width), lambda i: (0, 0)),
                  pl.BlockSpec((n_gate, D), lambda i: (0, 0)),
                  pl.BlockSpec((1, HD), lambda i: (0, 0))],
        out_specs=[pl.BlockSpec((heads, tm, HD), lambda i: (0, i, 0)),
                   pl.BlockSpec((n_gate, tm), lambda i: (0, i))],
        out_shape=[jax.ShapeDtypeStruct((heads, T, HD), BF16), jax.ShapeDtypeStruct((n_gate, T), F32)],
        compiler_params=_cparams(("parallel",)),
        name="nsa_in",
    )(h, norm_g.reshape(1, D), w[:, :width], w[:, width:].T, q_norm.reshape(1, HD))

    ncb = S // CMP_STRIDE
    nsb = S // SLC_BLOCK
    n_cmp = (S - CMP_BLOCK) // CMP_STRIDE + 1
    c_start = jnp.arange(ncb) * CMP_STRIDE
    c_pos = c_start + CMP_BLOCK - 1
    s_start = jnp.arange(nsb) * SLC_BLOCK
    ov_t = ((c_start[None, :] < s_start[:, None] + SLC_BLOCK) & (c_pos[None, :] >= s_start[:, None])
            & (jnp.arange(ncb)[None, :] < n_cmp)).astype(BF16)
    h_idx = jnp.arange(1, heads + 1, dtype=F32)
    slopes = 2.0 ** (-8.0 * h_idx / heads) * LOG2E
    nq = S // tq
    w_sel = ks.shape[2]
    o = pl.pallas_call(
        functools.partial(_nsa_attn_kernel, tq=tq, hpg=hpg),
        grid_spec=pltpu.PrefetchScalarGridSpec(
            num_scalar_prefetch=1,
            grid=(B, G, nq),
            in_specs=[pl.BlockSpec((hpg, tq, HD), lambda b, g, qi, sl: (g, b * nq + qi, 0)),
                      pl.BlockSpec((1, hpg * NSA_BRANCHES, tq), lambda b, g, qi, sl: (g, 0, b * nq + qi)),
                      pl.BlockSpec((1, ncb, HD), lambda b, g, qi, sl: (b * G + g, 0, 0)),
                      pl.BlockSpec((1, HD, ncb), lambda b, g, qi, sl: (b * G + g, 0, 0)),
                      pl.BlockSpec((1, S, w_sel), lambda b, g, qi, sl: (g, b, 0)),
                      pl.BlockSpec((1, nq, HD, tq), lambda b, g, qi, sl: (g, b, 0, 0)),
                      pl.BlockSpec((1, S, HD), lambda b, g, qi, sl: (g, b, 0)),
                      pl.BlockSpec((1, nq, HD, tq), lambda b, g, qi, sl: (G + g, b, 0, 0)),
                      pl.BlockSpec((nsb, ncb), lambda b, g, qi, sl: (0, 0))],
            out_specs=pl.BlockSpec((tq, hpg * HD), lambda b, g, qi, sl: (b * nq + qi, g)),
            scratch_shapes=[pltpu.VMEM((nsb, tq), F32),
                            pltpu.VMEM((w_sel, hpg * tq), BF16),
                            pltpu.VMEM((tq, hpg * tq), F32),
                            pltpu.VMEM((tq, hpg * tq), F32),
                            pltpu.VMEM((tq, hpg * tq), BF16),
                            pltpu.VMEM((1, hpg * tq), F32),
                            pltpu.VMEM((1, hpg * tq), F32),
                            pltpu.VMEM((HD, hpg * tq), F32),
                            pltpu.VMEM((hpg, HD, tq), F32),
                            pltpu.VMEM((HD, hpg * tq), F32)]),
        out_shape=jax.ShapeDtypeStruct((T, width), BF16),
        compiler_params=_cparams(("parallel", "parallel", "arbitrary")),
        name="nsa_attn",
    )(slopes, q, gate_t.reshape(G, hpg * NSA_BRANCHES, T), k_c, v_ct, ks, vt, kw, vt, ov_t)

    return pl.pallas_call(
        _mm_res_kernel,
        grid=(T // tm,),
        in_specs=[pl.BlockSpec((tm, width), lambda i: (i, 0)),
                  pl.BlockSpec((width, D), lambda i: (0, 0)),
                  pl.BlockSpec((tm, D), lambda i: (i, 0))],
        out_specs=pl.BlockSpec((tm, D), lambda i: (i, 0)),
        out_shape=jax.ShapeDtypeStruct((T, D), F32),
        compiler_params=_cparams(("parallel",)),
        name="nsa_out",
    )(o, w_out.astype(BF16), h)


def kernel(x, a_norm, a_w_in, a_v_norm, a_w_s, a_b_s, a_w_out, kv_norm, kv_w, kv_k_norm, phi_pe, phi_w1, phi_b1, phi_w2, phi_b2, b_norm, b_w_in, b_q_norm, b_w_out, ffn_norm, peer_w_q, peer_keys, peer_u, peer_v):
    B, S, D = x.shape
    depth = ffn_norm.shape[0]
    n_a = a_norm.shape[0]
    tq = 256
    h = x.reshape(B * S, D)
    shared = None
    for layer in range(depth):
        if layer < n_a:
            a = layer
            h = _gmlp_layer(h, a_norm[a], a_w_in[a], a_v_norm[a], a_w_s[a], a_b_s[a], a_w_out[a])
        else:
            if shared is None:
                shared = _nsa_shared_kv(h, B, S, kv_norm, kv_w, kv_k_norm, phi_pe, phi_w1, phi_b1, phi_w2,
                                        phi_b2, tq)
            b = layer - n_a
            h = _nsa_layer(h, B, S, b_norm[b], b_w_in[b], b_q_norm[b], b_w_out[b], *shared, tq)
        h = _peer_layer(h, ffn_norm[layer], peer_w_q[layer], peer_keys[layer], peer_u[layer], peer_v[layer])
    return h.reshape(B, S, D)
```

```python
import functools
import math

import jax
import jax.numpy as jnp
from jax import lax
from jax.experimental import pallas as pl
from jax.experimental.pallas import tpu as pltpu

F32 = jnp.float32
BF16 = jnp.bfloat16

GMLP_CHUNK = 128
GMLP_GROUPS = 8
NSA_HEAD_DIM = 128
NSA_KV_GROUPS = 4
NSA_BRANCHES = 3
CMP_BLOCK = 32
CMP_STRIDE = 16
SLC_BLOCK = 64
N_SEL = 16
WINDOW = 512
FORCE_BONUS = 1000.0
PEER_HEADS = 8
PEER_NKEYS = 128
PEER_TOPK = 16
PEER_KEY_DIM = 128
RMS_EPS = 1e-6
MASK_FILL = -1e30
TINY = float(jnp.finfo(jnp.float32).tiny)

V7X_VMEM_LIMIT_BYTES = 56 * 1024 * 1024
LANES = 128
SUBLANES = 8
LOG2E = 1.4426950408889634
TILE_WORD_BITS = 16

_NT = (((1,), (1,)), ((), ()))


def _cparams(sem):
    return pltpu.CompilerParams(dimension_semantics=sem, vmem_limit_bytes=V7X_VMEM_LIMIT_BYTES)


def _gelu(x):
    c = math.sqrt(2.0 / math.pi)
    return x * (0.5 * (1.0 + jnp.tanh(c * (x + 0.044715 * (x * x * x)))))


def _rms_rows(x, gain):
    r = lax.rsqrt(jnp.mean(x * x, axis=-1, keepdims=True) + RMS_EPS)
    return (x * r) * gain


def _gmlp_in_kernel(h_ref, g_ref, w_ref, z_ref, ss_ref, xn_ref, *, n_u_tiles):
    j = pl.program_id(1)

    @pl.when(j == 0)
    def _():
        xn_ref[...] = _rms_rows(h_ref[...], g_ref[...]).astype(BF16)
        ss_ref[...] = jnp.zeros_like(ss_ref)

    z = _gelu(jnp.dot(xn_ref[...], w_ref[...], preferred_element_type=F32))
    z_ref[...] = z.astype(BF16)

    @pl.when(j >= n_u_tiles)
    def _():
        ss_ref[...] += jnp.sum(z * z, axis=-1, keepdims=True)


def _gmlp_out_kernel(u_ref, v_ref, ss_ref, vg_ref, ws_ref, bs_ref, wo_ref, h_ref, o_ref, prod_ref, *,
                     n_chunks, half):
    j = pl.program_id(1)
    gdim = half // GMLP_GROUPS

    @pl.when(j == 0)
    def _():
        row = lax.broadcasted_iota(jnp.int32, (GMLP_CHUNK, GMLP_CHUNK), 0)
        col = lax.broadcasted_iota(jnp.int32, (GMLP_CHUNK, GMLP_CHUNK), 1)
        tril = row >= col
        for c in range(n_chunks):
            rows = slice(c * GMLP_CHUNK, (c + 1) * GMLP_CHUNK)
            rinv = lax.rsqrt(ss_ref[c] * (1.0 / half) + RMS_EPS)
            for g in range(GMLP_GROUPS):
                cols = slice(g * gdim, (g + 1) * gdim)
                wc = (jnp.where(tril, ws_ref[g], 0.0) * rinv).astype(BF16)
                sv = jnp.dot(wc, v_ref[rows, cols], preferred_element_type=F32)
                sv = sv * vg_ref[:, cols] + bs_ref[:, g:g + 1]
                prod_ref[rows, cols] = (u_ref[rows, cols].astype(F32) * sv).astype(BF16)

    o_ref[...] = jnp.dot(prod_ref[...], wo_ref[...], preferred_element_type=F32) + h_ref[...]


def _gmlp_layer(h, norm_g, w_in, v_norm, w_s, b_s, w_out):
    T, D = h.shape
    half = w_out.shape[0]
    tm, tn = 512, 1024
    n_u = half // tn
    z, ss = pl.pallas_call(
        functools.partial(_gmlp_in_kernel, n_u_tiles=n_u),
        grid=(T // tm, 2 * half // tn),
        in_specs=[pl.BlockSpec((tm, D), lambda i, j: (i, 0)),
                  pl.BlockSpec((1, D), lambda i, j: (0, 0)),
                  pl.BlockSpec((D, tn), lambda i, j: (0, j))],
        out_specs=[pl.BlockSpec((tm, tn), lambda i, j: (i, j)),
                   pl.BlockSpec((tm, 1), lambda i, j: (i, 0))],
        out_shape=[jax.ShapeDtypeStruct((T, 2 * half), BF16), jax.ShapeDtypeStruct((T, 1), F32)],
        scratch_shapes=[pltpu.VMEM((tm, D), BF16)],
        compiler_params=_cparams(("parallel", "arbitrary")),
        name="gmlp_in",
    )(h, norm_g.reshape(1, D), w_in.astype(BF16))

    tn2 = 512
    n_chunks = tm // GMLP_CHUNK
    return pl.pallas_call(
        functools.partial(_gmlp_out_kernel, n_chunks=n_chunks, half=half),
        grid=(T // tm, D // tn2),
        in_specs=[pl.BlockSpec((tm, half), lambda i, j: (i, 0)),
                  pl.BlockSpec((tm, half), lambda i, j: (i, 1)),
                  pl.BlockSpec((n_chunks, 1, GMLP_CHUNK), lambda i, j: (i, 0, 0)),
                  pl.BlockSpec((1, half), lambda i, j: (0, 0)),
                  pl.BlockSpec((GMLP_GROUPS, GMLP_CHUNK, GMLP_CHUNK), lambda i, j: (0, 0, 0)),
                  pl.BlockSpec((GMLP_CHUNK, GMLP_GROUPS), lambda i, j: (0, 0)),
                  pl.BlockSpec((half, tn2), lambda i, j: (0, j)),
                  pl.BlockSpec((tm, tn2), lambda i, j: (i, j))],
        out_specs=pl.BlockSpec((tm, tn2), lambda i, j: (i, j)),
        out_shape=jax.ShapeDtypeStruct((T, D), F32),
        scratch_shapes=[pltpu.VMEM((tm, half), BF16)],
        compiler_params=_cparams(("parallel", "arbitrary")),
        name="gmlp_out",
    )(z, z, ss.reshape(T // GMLP_CHUNK, 1, GMLP_CHUNK), v_norm.reshape(1, half), w_s, b_s.T,
      w_out.astype(BF16), h)


def _cmpx(v, i, l, desc):
    hi, lo = jnp.maximum(v[i], v[l]), jnp.minimum(v[i], v[l])
    v[i], v[l] = (hi, lo) if desc else (lo, hi)


def _bitonic_sort_desc(v):
    n = len(v)
    k = 2
    while k <= n:
        j = k // 2
        while j >= 1:
            for i in range(n):
                l = i ^ j
                if l > i:
                    _cmpx(v, i, l, (i & k) == 0)
            j //= 2
        k *= 2


def _bitonic_merge_desc(v):
    n = len(v)
    j = n // 2
    while j >= 1:
        for i in range(n):
            l = i ^ j
            if l > i:
                _cmpx(v, i, l, True)
        j //= 2


def _top16_sorted(rows):
    v = list(rows)
    _bitonic_sort_desc(v)
    for shift in (4, 2, 1):
        w = [pltpu.roll(x, shift, axis=0) for x in v]
        v = [jnp.maximum(v[i], w[PEER_TOPK - 1 - i]) for i in range(PEER_TOPK)]
        _bitonic_merge_desc(v)
    return v


def _peer_score_kernel(h_ref, g_ref, wq_ref, keys_ref, xn_ref, s_ref, st_ref, t2_ref):
    tm = h_ref.shape[0]
    xn = _rms_rows(h_ref[...], g_ref[...]).astype(BF16)
    xn_ref[...] = xn
    q = jnp.dot(xn, wq_ref[...], preferred_element_type=F32).astype(BF16)
    sub = lax.broadcasted_iota(jnp.int32, (SUBLANES, tm), 0)

    def pack(lst):
        out = lst[0]
        for s in range(1, SUBLANES):
            out = jnp.where(sub == s, lst[s], out)
        return out

    for h in range(PEER_HEADS):
        tops = []
        for p in range(2):
            hp = 2 * h + p
            qs = q[:, hp * PEER_KEY_DIM:(hp + 1) * PEER_KEY_DIM]
            s_t = lax.dot_general(keys_ref[hp], qs, _NT, preferred_element_type=F32)
            s_ref[hp] = s_t
            tops.append(_top16_sorted([s_t[SUBLANES * r:SUBLANES * (r + 1), :] for r in range(16)]))
        t1, t2 = tops
        t2lo, t2hi, t1hi = pack(t2[0:8]), pack(t2[8:16]), pack(t1[8:16])
        cands = [t1[0] + t2lo, t1[0] + t2hi] + [t1[a] + t2lo for a in range(1, 8)] + [t1hi + t2[0]]
        cands += [jnp.full((SUBLANES, tm), -jnp.inf, F32)] * (16 - len(cands))
        top = _top16_sorted(cands)
        z = jnp.ones((SUBLANES, tm), F32)
        for k in range(1, PEER_TOPK):
            z = z + jnp.exp(top[k] - top[0])
        for k, val in enumerate((top[PEER_TOPK - 1], t1[0], t2[0], z)):
            st_ref[pl.ds(k * PEER_HEADS + h, 1), :] = val[0:1, :]
        t2_ref[h, 0:SUBLANES, :] = t2lo
        t2_ref[h, SUBLANES:PEER_TOPK, :] = t2hi


def _peer_main_kernel(xnt_ref, u_ref, vt_ref, s_ref, st_ref, t2_ref, h_ref, o_ref, acc_ref, e_ref, tau_ref,
                      hid_ref, w_ref, *, rows_per_tile):
    jj = pl.program_id(1)
    tm = xnt_ref.shape[1]
    te = hid_ref.shape[1]
    rc = 16

    @pl.when(jj == 0)
    def _():
        acc_ref[...] = jnp.zeros_like(acc_ref)
        hid_ref[...] = jnp.zeros_like(hid_ref)
        w_ref[...] = jnp.zeros_like(w_ref)
        for h in range(PEER_HEADS):
            thr = st_ref[h:h + 1, :]
            m1 = st_ref[PEER_HEADS + h:PEER_HEADS + h + 1, :]
            m2 = st_ref[2 * PEER_HEADS + h:2 * PEER_HEADS + h + 1, :]
            zz = st_ref[3 * PEER_HEADS + h:3 * PEER_HEADS + h + 1, :]
            s1 = s_ref[2 * h]
            e_ref[2 * h] = jnp.exp(s1 - m1)
            e_ref[2 * h + 1] = jnp.exp(s_ref[2 * h + 1] - m2) / zz
            tau = jnp.full(s1.shape, jnp.inf, F32)
            for b in range(PEER_TOPK):
                t2b = t2_ref[h, b:b + 1, :]
                tau = jnp.where(s1 + t2b >= thr, t2b, tau)
            tau_ref[h] = tau

    tn = 256
    n_pieces = tm // tn

    def project_pieces(slot):
        def piece(n):
            hid_ref[slot, :, n * tn:(n + 1) * tn] = jnp.dot(
                u_ref[slot * te:(slot + 1) * te, :], xnt_ref[:, n * tn:(n + 1) * tn],
                preferred_element_type=F32)
        return [functools.partial(piece, n) for n in range(n_pieces)]

    def gate_chunks(slot):
        tile = 2 * jj + slot
        out = []
        for r in range(rows_per_tile):
            i = tile * rows_per_tile + r
            taus = [tau_ref[h, pl.ds(i, 1), :] for h in range(PEER_HEADS)]
            e1s = [e_ref[2 * h, pl.ds(i, 1), :] for h in range(PEER_HEADS)]
            for c in range(PEER_NKEYS // rc):
                def chunk(r=r, c=c, taus=taus, e1s=e1s):
                    rows = slice(c * rc, (c + 1) * rc)
                    gate = jnp.zeros((rc, tm), F32)
                    for h in range(PEER_HEADS):
                        gate = gate + jnp.where(s_ref[2 * h + 1, rows, :] >= taus[h],
                                                e_ref[2 * h + 1, rows, :], 0.0) * e1s[h]
                    out_rows = slice(r * PEER_NKEYS + c * rc, r * PEER_NKEYS + (c + 1) * rc)
                    w_ref[slot, out_rows, :] = (_gelu(hid_ref[slot, out_rows, :]) * gate).astype(BF16)
                out.append(chunk)
        return out

    def accumulate_pieces(slot):
        def piece(n):
            acc_ref[:, n * tn:(n + 1) * tn] += jnp.dot(vt_ref[:, slot * te:(slot + 1) * te],
                                                       w_ref[slot, :, n * tn:(n + 1) * tn],
                                                       preferred_element_type=F32)
        return [functools.partial(piece, n) for n in range(n_pieces)]

    def emit_interleaved(many, few):
        stride = len(many) // len(few)
        for k, f in enumerate(many):
            if k % stride == 0 and k // stride < len(few):
                few[k // stride]()
            f()

    for f in project_pieces(0):
        f()
    emit_interleaved(gate_chunks(0), project_pieces(1))
    emit_interleaved(gate_chunks(1), accumulate_pieces(0))
    for f in accumulate_pieces(1):
        f()

    @pl.when(jj == pl.num_programs(1) - 1)
    def _():
        o_ref[...] = acc_ref[...].T + h_ref[...]


def _peer_layer(h, norm_g, w_q, keys, u_tab, v_tab):
    T, D = h.shape
    n_exp = u_tab.shape[0]
    n_hp = 2 * PEER_HEADS
    tm1 = 256
    xn, s_t, stats, t2 = pl.pallas_call(
        _peer_score_kernel,
        grid=(T // tm1,),
        in_specs=[pl.BlockSpec((tm1, D), lambda i: (i, 0)),
                  pl.BlockSpec((1, D), lambda i: (0, 0)),
                  pl.BlockSpec((D, n_hp * PEER_KEY_DIM), lambda i: (0, 0)),
                  pl.BlockSpec((n_hp, PEER_NKEYS, PEER_KEY_DIM), lambda i: (0, 0, 0))],
        out_specs=[pl.BlockSpec((tm1, D), lambda i: (i, 0)),
                   pl.BlockSpec((n_hp, PEER_NKEYS, tm1), lambda i: (0, 0, i)),
                   pl.BlockSpec((4 * PEER_HEADS, tm1), lambda i: (0, i)),
                   pl.BlockSpec((PEER_HEADS, PEER_TOPK, tm1), lambda i: (0, 0, i))],
        out_shape=[jax.ShapeDtypeStruct((T, D), BF16),
                   jax.ShapeDtypeStruct((n_hp, PEER_NKEYS, T), F32),
                   jax.ShapeDtypeStruct((4 * PEER_HEADS, T), F32),
                   jax.ShapeDtypeStruct((PEER_HEADS, PEER_TOPK, T), F32)],
        compiler_params=_cparams(("parallel",)),
        name="peer_score",
    )(h, norm_g.reshape(1, D), w_q.astype(BF16),
      keys.reshape(n_hp, PEER_NKEYS, PEER_KEY_DIM).astype(BF16))

    tm, te = 512, 512
    n_tiles = n_exp // te
    n_pairs = n_tiles // 2
    once = pl.Buffered(1)
    return pl.pallas_call(
        functools.partial(_peer_main_kernel, rows_per_tile=te // PEER_NKEYS),
        grid=(T // tm, n_pairs),
        in_specs=[pl.BlockSpec((D, tm), lambda i, j: (0, i), pipeline_mode=once),
                  pl.BlockSpec((2 * te, D), lambda i, j: (j, 0)),
                  pl.BlockSpec((D, 2 * te), lambda i, j: (0, j)),
                  pl.BlockSpec((n_hp, PEER_NKEYS, tm), lambda i, j: (0, 0, i), pipeline_mode=once),
                  pl.BlockSpec((4 * PEER_HEADS, tm), lambda i, j: (0, i), pipeline_mode=once),
                  pl.BlockSpec((PEER_HEADS, PEER_TOPK, tm), lambda i, j: (0, 0, i), pipeline_mode=once),
                  pl.BlockSpec((tm, D), lambda i, j: (i, 0), pipeline_mode=once)],
        out_specs=pl.BlockSpec((tm, D), lambda i, j: (i, 0)),
        out_shape=jax.ShapeDtypeStruct((T, D), F32),
        scratch_shapes=[pltpu.VMEM((D, tm), F32),
                        pltpu.VMEM((n_hp, PEER_NKEYS, tm), F32),
                        pltpu.VMEM((PEER_HEADS, PEER_NKEYS, tm), F32),
                        pltpu.VMEM((2, te, tm), F32),
                        pltpu.VMEM((2, te, tm), BF16)],
        compiler_params=_cparams(("parallel", "arbitrary")),
        name="peer_main",
    )(xn.T, u_tab.astype(BF16), v_tab.T.astype(BF16), s_t, stats, t2, h)


def _kv_proj_kernel(h_ref, g_ref, w_ref, wvt_ref, kg_ref, raw_ref, ks_ref, kw_ref, vt_ref, *, tiles_per_seq):
    G, HD = NSA_KV_GROUPS, NSA_HEAD_DIM
    tk = h_ref.shape[0]
    xn = _rms_rows(h_ref[...], g_ref[...]).astype(BF16)
    y = jnp.dot(xn, w_ref[...], preferred_element_type=F32)
    col = lambda six, c: y[:, (six * G + c) * HD:(six * G + c + 1) * HD]
    nsbp = ks_ref.shape[2] - HD
    blk = lax.rem(pl.program_id(0), tiles_per_seq) * (tk // SLC_BLOCK) \
        + lax.broadcasted_iota(jnp.int32, (tk, nsbp), 0) // SLC_BLOCK
    onehot = jnp.where(lax.broadcasted_iota(jnp.int32, (tk, nsbp), 1) == blk, 1.0, 0.0).astype(BF16)
    for c in range(G):
        raw_ref[c] = col(0, c)
        raw_ref[G + c] = col(1, c)
        ks_ref[c, :, 0:HD] = _rms_rows(col(2, c), kg_ref[1:2, :]).astype(BF16)
        ks_ref[c, :, HD:] = onehot
        kw_ref[c] = _rms_rows(col(3, c), kg_ref[2:3, :]).astype(BF16)
    y_t = lax.dot_general(wvt_ref[...], xn, _NT, preferred_element_type=F32)
    for c in range(2 * G):
        vt_ref[c, 0] = y_t[c * HD:(c + 1) * HD, :].astype(BF16)


def _compress_kernel(x_ref, pe_ref, w1_ref, b1_ref, w2_ref, b2_ref, gain_ref, o_ref, *, is_key):
    x = x_ref[0]
    n, half_k = x.shape
    xa = (x + pe_ref[0, 0:1, :]).astype(BF16)
    xb = (x + pe_ref[0, 1:2, :]).astype(BF16)
    a = jnp.dot(xa, w1_ref[0, :half_k, :], preferred_element_type=F32)
    b = jnp.dot(xb, w1_ref[0, half_k:, :], preferred_element_type=F32)
    hid = _gelu(a + pltpu.roll(b, n - 1, axis=0) + b1_ref[0])
    y = jnp.dot(hid.astype(BF16), w2_ref[0], preferred_element_type=F32) + b2_ref[0]
    if is_key:
        o_ref[0] = _rms_rows(y, gain_ref[...]).astype(BF16)
    else:
        o_ref[0] = y.T.astype(BF16)


def _nsa_shared_kv(h, B, S, kv_norm, kv_w, kv_k_norm, phi_pe, phi_w1, phi_b1, phi_w2, phi_b2, tk):
    T, D = h.shape
    G, HD = NSA_KV_GROUPS, NSA_HEAD_DIM
    GW = G * HD
    w = kv_w.astype(BF16)
    sl = lambda six: w[:, six * GW:(six + 1) * GW]
    wk = jnp.concatenate([sl(0), sl(1), sl(2), sl(4)], axis=1)
    wvt = jnp.concatenate([sl(3), sl(5)], axis=1).T
    nsbp = -(-(S // SLC_BLOCK) // LANES) * LANES
    raw, ks, kw, vt = pl.pallas_call(
        functools.partial(_kv_proj_kernel, tiles_per_seq=S // tk),
        grid=(T // tk,),
        in_specs=[pl.BlockSpec((tk, D), lambda i: (i, 0)),
                  pl.BlockSpec((1, D), lambda i: (0, 0)),
                  pl.BlockSpec((D, 4 * GW), lambda i: (0, 0)),
                  pl.BlockSpec((2 * GW, D), lambda i: (0, 0)),
                  pl.BlockSpec((NSA_BRANCHES, HD), lambda i: (0, 0))],
        out_specs=[pl.BlockSpec((2 * G, tk, HD), lambda i: (0, i, 0)),
                   pl.BlockSpec((G, tk, HD + nsbp), lambda i: (0, i, 0)),
                   pl.BlockSpec((G, tk, HD), lambda i: (0, i, 0)),
                   pl.BlockSpec((2 * G, 1, HD, tk), lambda i: (0, i, 0, 0))],
        out_shape=[jax.ShapeDtypeStruct((2 * G, T, HD), F32),
                   jax.ShapeDtypeStruct((G, T, HD + nsbp), BF16),
                   jax.ShapeDtypeStruct((G, T, HD), BF16),
                   jax.ShapeDtypeStruct((2 * G, T // tk, HD, tk), BF16)],
        compiler_params=_cparams(("parallel",)),
        name="nsa_kv_proj",
    )(h, kv_norm.reshape(1, D), wk, wvt, kv_k_norm)

    ncb = S // CMP_STRIDE
    half_k = CMP_STRIDE * HD
    hidden = phi_w1.shape[-1]
    raw16 = raw.reshape(2 * G, T // CMP_STRIDE, half_k)
    pe = phi_pe.reshape(2, CMP_BLOCK // CMP_STRIDE, half_k)
    outs = []
    for kind in range(2):
        is_key = kind == 0
        o_block = (1, ncb, HD) if is_key else (1, HD, ncb)
        outs.append(pl.pallas_call(
            functools.partial(_compress_kernel, is_key=is_key),
            grid=(B, G),
            in_specs=[pl.BlockSpec((1, ncb, half_k), lambda b, g, kind=kind: (kind * G + g, b, 0)),
                      pl.BlockSpec((1, 2, half_k), lambda b, g, kind=kind: (kind, 0, 0)),
                      pl.BlockSpec((1, 2 * half_k, hidden), lambda b, g, kind=kind: (kind, 0, 0)),
                      pl.BlockSpec((1, 1, hidden), lambda b, g, kind=kind: (kind, 0, 0)),
                      pl.BlockSpec((1, hidden, HD), lambda b, g, kind=kind: (kind, 0, 0)),
                      pl.BlockSpec((1, 1, HD), lambda b, g, kind=kind: (kind, 0, 0)),
                      pl.BlockSpec((1, HD), lambda b, g: (0, 0))],
            out_specs=pl.BlockSpec(o_block, lambda b, g: (b * G + g, 0, 0)),
            out_shape=jax.ShapeDtypeStruct((B * G,) + o_block[1:], BF16),
            compiler_params=_cparams(("parallel", "parallel")),
            name="nsa_compress_k" if is_key else "nsa_compress_v",
        )(raw16, pe, phi_w1.astype(BF16), phi_b1.reshape(2, 1, hidden), phi_w2.astype(BF16),
          phi_b2.reshape(2, 1, HD), kv_k_norm[0:1]))
    return outs[0], outs[1], ks, kw, vt


def _nsa_in_kernel(h_ref, g_ref, wq_ref, wgt_ref, qg_ref, q_ref, gate_ref):
    HD = NSA_HEAD_DIM
    xn = _rms_rows(h_ref[...], g_ref[...]).astype(BF16)
    y = jnp.dot(xn, wq_ref[...], preferred_element_type=F32)
    qscale = HD ** -0.5 * LOG2E
    for hd in range(q_ref.shape[0]):
        q_ref[hd] = (_rms_rows(y[:, hd * HD:(hd + 1) * HD], qg_ref[...]) * qscale).astype(BF16)
    g_t = lax.dot_general(wgt_ref[...], xn, _NT, preferred_element_type=F32)
    gate_ref[...] = jax.nn.sigmoid(g_t)


def _nsa_attn_kernel(slopes_ref, q_ref, gate_ref, kc_ref, vct_ref, ks_ref, vst_ref, kw_ref, vwt_ref, ovt_ref,
                     o_ref, qat_ref, b_ref, s_ref, p_ref, m_ref, l_ref, acc_ref, oc_ref, os_ref, *,
                     tq, hpg):
    HD = NSA_HEAD_DIM
    tk = tq
    g = pl.program_id(1)
    qi = pl.program_id(2)
    t0 = qi * tq
    ncb = kc_ref.shape[1]
    nsb = ovt_ref.shape[0]
    nsbp = qat_ref.shape[0] - HD
    slopes = [slopes_ref[g * hpg + hh] for hh in range(hpg)]

    tpos_c = lax.broadcasted_iota(jnp.int32, (ncb, tq), 1) + t0
    cpos = lax.broadcasted_iota(jnp.int32, (ncb, tq), 0) * CMP_STRIDE + (CMP_BLOCK - 1)
    dist_c = tpos_c - cpos
    mask_c = dist_c >= 0
    dist_cf = dist_c.astype(F32)
    p_sum = jnp.zeros((ncb, tq), F32)
    for hh in range(hpg):
        s = lax.dot_general(kc_ref[0], q_ref[hh], _NT, preferred_element_type=F32) - slopes[hh] * dist_cf
        s = jnp.where(mask_c, s, MASK_FILL)
        e = jnp.where(mask_c, jnp.exp2(s - jnp.max(s, axis=0, keepdims=True)), 0.0)
        p = e / jnp.maximum(jnp.sum(e, axis=0, keepdims=True), TINY)
        oc_ref[hh] = jnp.dot(vct_ref[0], p.astype(BF16), preferred_element_type=F32)
        p_sum = p_sum + p

    imp = jnp.zeros((nsb, tq), F32)
    rem = p_sum
    for _ in range(3):
        part = rem.astype(BF16)
        imp = imp + jnp.dot(ovt_ref[...], part, preferred_element_type=F32)
        rem = rem - part.astype(F32)
    jb = lax.broadcasted_iota(jnp.int32, (nsb, tq), 0)
    cur = (lax.broadcasted_iota(jnp.int32, (nsb, tq), 1) + t0) // SLC_BLOCK
    forced = jnp.where(jb == 0, 1.0, jnp.where(jb == cur, 1.0, jnp.where(jb == cur - 1, 1.0, 0.0)))
    score = jnp.where(jb <= cur, imp + FORCE_BONUS * forced, -jnp.inf)
    blocks = [score[SUBLANES * r:SUBLANES * (r + 1), :] for r in range(nsb // SUBLANES)]
    pad = [jnp.full((SUBLANES, tq), -jnp.inf, F32)] * (N_SEL - len(blocks))
    kth = _top16_sorted(blocks + pad)[N_SEL - 1][0:1, :]
    above = [jnp.where(b > kth, 1.0, 0.0) for b in blocks]
    equal = [jnp.where(b == kth, 1.0, 0.0) for b in blocks]
    n_above = above[0]
    for a in above[1:]:
        n_above = n_above + a
    need = float(N_SEL) - jnp.sum(n_above, axis=0, keepdims=True)
    sub = lax.broadcasted_iota(jnp.int32, (SUBLANES, tq), 0)
    before = jnp.zeros((1, tq), F32)
    picked = []
    for a, e in zip(above, equal):
        run = e
        for sh in (1, 2, 4):
            run = run + jnp.where(sub >= sh, pltpu.roll(run, sh, axis=0), 0.0)
        first = jnp.where((run - e) + before < need, e, 0.0)
        picked.append(jnp.where(a + first > 0.5, 0.0, MASK_FILL))
        before = before + jnp.max(run, axis=0, keepdims=True)
    neg_sel = jnp.concatenate(picked, axis=0)

    bpt = tk // SLC_BLOCK
    tile_max = neg_sel
    for sh in range(1, bpt):
        tile_max = jnp.maximum(tile_max, pltpu.roll(neg_sel, nsb - sh, axis=0))
    hit = jnp.max(tile_max, axis=1, keepdims=True) > 0.5 * MASK_FILL
    row = lax.broadcasted_iota(jnp.int32, (nsb, 1), 0)
    bit = jnp.where(hit & (lax.rem(row, bpt) == 0),
                    lax.shift_left(jnp.int32(1), lax.rem(row // bpt, TILE_WORD_BITS)), 0).astype(F32)
    in_lo = row < TILE_WORD_BITS * bpt
    word_lo = jnp.sum(jnp.where(in_lo, bit, 0.0)).astype(jnp.int32)
    word_hi = jnp.sum(jnp.where(in_lo, 0.0, bit)).astype(jnp.int32)

    if nsbp > nsb:
        neg_sel = jnp.concatenate([neg_sel, jnp.zeros((nsbp - nsb, tq), F32)], axis=0)
    neg_sel = neg_sel.astype(BF16)

    wl = hpg * tq
    rc = 16
    kio = lax.broadcasted_iota(jnp.int32, (tk, wl), 0)
    kio_c = lax.broadcasted_iota(jnp.int32, (rc, wl), 0)
    tio_c = lax.rem(lax.broadcasted_iota(jnp.int32, (rc, wl), 1), tq)
    slope_row = jnp.concatenate([jnp.full((1, tq), slopes[hh], F32) for hh in range(hpg)], axis=1)
    for hh in range(hpg):
        qat_ref[0:HD, hh * tq:(hh + 1) * tq] = q_ref[hh].astype(F32).T.astype(BF16)
        qat_ref[HD:, hh * tq:(hh + 1) * tq] = neg_sel
    b_ref[...] = slope_row * kio.astype(F32)

    chunks = [slice(r * rc, (r + 1) * rc) for r in range(tk // rc)]

    def key_tile(k_ref, ki):
        return k_ref[0, pl.ds(pl.multiple_of(ki * tk, tk), tk), :]

    def bias_and_max(mask_kind):
        acc = None
        for r, ch in enumerate(chunks):
            t = s_ref[ch, :] + b_ref[ch, :]
            if mask_kind == "causal":
                t = t + jnp.where(kio_c + r * rc > tio_c, MASK_FILL, 0.0)
            elif mask_kind == "beyond":
                t = t + jnp.where(kio_c + r * rc <= tio_c, MASK_FILL, 0.0)
            s_ref[ch, :] = t
            acc = t if acc is None else jnp.maximum(acc, t)
        return jnp.max(acc, axis=0, keepdims=True)

    def exp_pass(shift):
        lsum = jnp.zeros((rc, wl), F32)
        for ch in chunks:
            p = jnp.exp2(s_ref[ch, :] + shift)
            p_ref[ch, :] = p.astype(BF16)
            lsum = lsum + p
        return jnp.sum(lsum, axis=0, keepdims=True)

    def first_tile(s, v_t):
        s_ref[...] = s
        m = bias_and_max("causal")
        m_ref[...] = m
        l_ref[...] = exp_pass(-m)
        acc_ref[...] = jnp.dot(v_t, p_ref[...], preferred_element_type=F32)

    def next_tile(s, back, v_t, mask_kind):
        s_ref[...] = s
        c = -slope_row * back
        m_old = m_ref[...]
        m_new = jnp.maximum(m_old, bias_and_max(mask_kind) + c)
        alpha = jnp.exp2(m_old - m_new)
        l_ref[...] = alpha * l_ref[...] + exp_pass(c - m_new)
        acc_ref[...] = alpha * acc_ref[...] + jnp.dot(v_t, p_ref[...], preferred_element_type=F32)
        m_ref[...] = m_new

    def sel_scores(ki):
        return jnp.dot(key_tile(ks_ref, ki), qat_ref[...], preferred_element_type=F32)

    def win_scores(ki):
        return jnp.dot(key_tile(kw_ref, ki), qat_ref[0:HD, :], preferred_element_type=F32)

    first_tile(sel_scores(qi), vst_ref[0, qi])

    def sel_body(step, carry):
        ki = qi - 1 - step
        word = jnp.where(ki < TILE_WORD_BITS, word_lo, word_hi)

        @pl.when((lax.shift_right_logical(word, lax.rem(ki, TILE_WORD_BITS)) & 1) != 0)
        def _():
            next_tile(sel_scores(ki), ((step + 1) * tk).astype(F32), vst_ref[0, ki], None)
        return carry

    lax.fori_loop(0, qi, sel_body, 0)
    os_ref[...] = acc_ref[...] / jnp.maximum(l_ref[...], TINY)

    first_tile(win_scores(qi), vwt_ref[0, qi])
    for back_tiles in range(1, WINDOW // tk + 1):
        @pl.when(qi >= back_tiles)
        def _(back_tiles=back_tiles):
            ki = qi - back_tiles
            next_tile(win_scores(ki), float(back_tiles * tk), vwt_ref[0, ki],
                      "beyond" if back_tiles == WINDOW // tk else None)
    o_w = acc_ref[...] / jnp.maximum(l_ref[...], TINY)

    for hh in range(hpg):
        cols = slice(hh * tq, (hh + 1) * tq)
        gates = [gate_ref[0, hh * NSA_BRANCHES + br:hh * NSA_BRANCHES + br + 1, :] for br in range(NSA_BRANCHES)]
        o_t = gates[0] * oc_ref[hh] + gates[1] * os_ref[:, cols] + gates[2] * o_w[:, cols]
        o_ref[:, hh * HD:(hh + 1) * HD] = o_t.T.astype(BF16)


def _mm_res_kernel(a_ref, w_ref, h_ref, o_ref):
    o_ref[...] = jnp.dot(a_ref[...], w_ref[...], preferred_element_type=F32) + h_ref[...]


def _nsa_layer(h, B, S, norm_g, w_in, q_norm, w_out, k_c, v_ct, ks, kw, vt, tq):
    T, D = h.shape
    G, HD = NSA_KV_GROUPS, NSA_HEAD_DIM
    width = w_out.shape[0]
    heads = width // HD
    hpg = heads // G
    n_gate = heads * NSA_BRANCHES
    w = w_in.astype(BF16)
    tm = 512
    q, gate_t = pl.pallas_call(
        _nsa_in_kernel,
        grid=(T // tm,),
        in_specs=[pl.BlockSpec((tm, D), lambda i: (i, 0)),
                  pl.BlockSpec((1, D), lambda i: (0, 0)),
                  pl.BlockSpec((D, width), lambda i: (0, 0)),
                  pl.BlockSpec((n_gate, D), lambda i: (0, 0)),
                  pl.BlockSpec((1, HD), lambda i: (0, 0))],
        out_specs=[pl.BlockSpec((heads, tm, HD), lambda i: (0, i, 0)),
                   pl.BlockSpec((n_gate, tm), lambda i: (0, i))],
        out_shape=[jax.ShapeDtypeStruct((heads, T, HD), BF16), jax.ShapeDtypeStruct((n_gate, T), F32)],
        compiler_params=_cparams(("parallel",)),
        name="nsa_in",
    )(h, norm_g.reshape(1, D), w[:, :width], w[:, width:].T, q_norm.reshape(1, HD))

    ncb = S // CMP_STRIDE
    nsb = S // SLC_BLOCK
    n_cmp = (S - CMP_BLOCK) // CMP_STRIDE + 1
    c_start = jnp.arange(ncb) * CMP_STRIDE
    c_pos = c_start + CMP_BLOCK - 1
    s_start = jnp.arange(nsb) * SLC_BLOCK
    ov_t = ((c_start[None, :] < s_start[:, None] + SLC_BLOCK) & (c_pos[None, :] >= s_start[:, None])
            & (jnp.arange(ncb)[None, :] < n_cmp)).astype(BF16)
    h_idx = jnp.arange(1, heads + 1, dtype=F32)
    slopes = 2.0 ** (-8.0 * h_idx / heads) * LOG2E
    nq = S // tq
    assert nq <= 2 * TILE_WORD_BITS, "the key-tile flags of a query tile are kept in two words"
    w_sel = ks.shape[2]
    o = pl.pallas_call(
        functools.partial(_nsa_attn_kernel, tq=tq, hpg=hpg),
        grid_spec=pltpu.PrefetchScalarGridSpec(
            num_scalar_prefetch=1,
            grid=(B, G, nq),
            in_specs=[pl.BlockSpec((hpg, tq, HD), lambda b, g, qi, sl: (g, b * nq + qi, 0)),
                      pl.BlockSpec((1, hpg * NSA_BRANCHES, tq), lambda b, g, qi, sl: (g, 0, b * nq + qi)),
                      pl.BlockSpec((1, ncb, HD), lambda b, g, qi, sl: (b * G + g, 0, 0)),
                      pl.BlockSpec((1, HD, ncb), lambda b, g, qi, sl: (b * G + g, 0, 0)),
                      pl.BlockSpec((1, S, w_sel), lambda b, g, qi, sl: (g, b, 0)),
                      pl.BlockSpec((1, nq, HD, tq), lambda b, g, qi, sl: (g, b, 0, 0)),
                      pl.BlockSpec((1, S, HD), lambda b, g, qi, sl: (g, b, 0)),
                      pl.BlockSpec((1, nq, HD, tq), lambda b, g, qi, sl: (G + g, b, 0, 0)),
                      pl.BlockSpec((nsb, ncb), lambda b, g, qi, sl: (0, 0))],
            out_specs=pl.BlockSpec((tq, hpg * HD), lambda b, g, qi, sl: (b * nq + qi, g)),
            scratch_shapes=[pltpu.VMEM((w_sel, hpg * tq), BF16),
                            pltpu.VMEM((tq, hpg * tq), F32),
                            pltpu.VMEM((tq, hpg * tq), F32),
                            pltpu.VMEM((tq, hpg * tq), BF16),
                            pltpu.VMEM((1, hpg * tq), F32),
                            pltpu.VMEM((1, hpg * tq), F32),
                            pltpu.VMEM((HD, hpg * tq), F32),
                            pltpu.VMEM((hpg, HD, tq), F32),
                            pltpu.VMEM((HD, hpg * tq), F32)]),
        out_shape=jax.ShapeDtypeStruct((T, width), BF16),
        compiler_params=_cparams(("parallel", "parallel", "arbitrary")),
        name="nsa_attn",
    )(slopes, q, gate_t.reshape(G, hpg * NSA_BRANCHES, T), k_c, v_ct, ks, vt, kw, vt, ov_t)

    return pl.pallas_call(
        _mm_res_kernel,
        grid=(T // tm,),
        in_specs=[pl.BlockSpec((tm, width), lambda i: (i, 0)),
                  pl.BlockSpec((width, D), lambda i: (0, 0)),
                  pl.BlockSpec((tm, D), lambda i: (i, 0))],
        out_specs=pl.BlockSpec((tm, D), lambda i: (i, 0)),
        out_shape=jax.ShapeDtypeStruct((T, D), F32),
        compiler_params=_cparams(("parallel",)),
        name="nsa_out",
    )(o, w_out.astype(BF16), h)


def kernel(x, a_norm, a_w_in, a_v_norm, a_w_s, a_b_s, a_w_out, kv_norm, kv_w, kv_k_norm, phi_pe, phi_w1, phi_b1, phi_w2, phi_b2, b_norm, b_w_in, b_q_norm, b_w_out, ffn_norm, peer_w_q, peer_keys, peer_u, peer_v):
    B, S, D = x.shape
    depth = ffn_norm.shape[0]
    n_a = a_norm.shape[0]
    tq = 256
    h = x.reshape(B * S, D)
    shared = None
    for layer in range(depth):
        if layer < n_a:
            a = layer
            h = _gmlp_layer(h, a_norm[a], a_w_in[a], a_v_norm[a], a_w_s[a], a_b_s[a], a_w_out[a])
        else:
            if shared is None:
                shared = _nsa_shared_kv(h, B, S, kv_norm, kv_w, kv_k_norm, phi_pe, phi_w1, phi_b1, phi_w2,
                                        phi_b2, tq)
            b = layer - n_a
            h = _nsa_layer(h, B, S, b_norm[b], b_w_in[b], b_q_norm[b], b_w_out[b], *shared, tq)
        h = _peer_layer(h, ffn_norm[layer], peer_w_q[layer], peer_keys[layer], peer_u[layer], peer_v[layer])
    return h.reshape(B, S, D)
```

```python
import functools
import math

import jax
import jax.numpy as jnp
from jax import lax
from jax.experimental import pallas as pl
from jax.experimental.pallas import tpu as pltpu

F32 = jnp.float32
BF16 = jnp.bfloat16

GMLP_CHUNK = 128
GMLP_GROUPS = 8
NSA_HEAD_DIM = 128
NSA_KV_GROUPS = 4
NSA_BRANCHES = 3
CMP_BLOCK = 32
CMP_STRIDE = 16
SLC_BLOCK = 64
N_SEL = 16
WINDOW = 512
FORCE_BONUS = 1000.0
PEER_HEADS = 8
PEER_NKEYS = 128
PEER_TOPK = 16
PEER_KEY_DIM = 128
RMS_EPS = 1e-6
MASK_FILL = -1e30
TINY = float(jnp.finfo(jnp.float32).tiny)

V7X_VMEM_LIMIT_BYTES = 56 * 1024 * 1024
LANES = 128
SUBLANES = 8
LOG2E = 1.4426950408889634
TILE_WORD_BITS = 16

_NT = (((1,), (1,)), ((), ()))


def _cparams(sem):
    return pltpu.CompilerParams(dimension_semantics=sem, vmem_limit_bytes=V7X_VMEM_LIMIT_BYTES)


GELU_C0 = math.sqrt(2.0 / math.pi)
GELU_C1 = 0.044715 * GELU_C0


def _gelu(x):
    return x * (0.5 * (1.0 + jnp.tanh(GELU_C0 * (x + 0.044715 * (x * x * x)))))


def _rms_rows(x, gain):
    r = lax.rsqrt(jnp.mean(x * x, axis=-1, keepdims=True) + RMS_EPS)
    return (x * r) * gain


def _gmlp_in_kernel(h_ref, g_ref, w_ref, z_ref, ss_ref, xn_ref, *, n_u_tiles):
    j = pl.program_id(1)

    @pl.when(j == 0)
    def _():
        xn_ref[...] = _rms_rows(h_ref[...], g_ref[...]).astype(BF16)
        ss_ref[...] = jnp.zeros_like(ss_ref)

    z = _gelu(jnp.dot(xn_ref[...], w_ref[...], preferred_element_type=F32))
    z_ref[...] = z.astype(BF16)

    @pl.when(j >= n_u_tiles)
    def _():
        ss_ref[...] += jnp.sum(z * z, axis=-1, keepdims=True)


def _gmlp_out_kernel(u_ref, v_ref, ss_ref, vg_ref, ws_ref, bs_ref, wo_ref, h_ref, o_ref, prod_ref, *,
                     n_chunks, half):
    j = pl.program_id(1)
    gdim = half // GMLP_GROUPS

    @pl.when(j == 0)
    def _():
        row = lax.broadcasted_iota(jnp.int32, (GMLP_CHUNK, GMLP_CHUNK), 0)
        col = lax.broadcasted_iota(jnp.int32, (GMLP_CHUNK, GMLP_CHUNK), 1)
        tril = row >= col
        for c in range(n_chunks):
            rows = slice(c * GMLP_CHUNK, (c + 1) * GMLP_CHUNK)
            rinv = lax.rsqrt(ss_ref[c] * (1.0 / half) + RMS_EPS)
            for g in range(GMLP_GROUPS):
                cols = slice(g * gdim, (g + 1) * gdim)
                wc = (jnp.where(tril, ws_ref[g], 0.0) * rinv).astype(BF16)
                sv = jnp.dot(wc, v_ref[rows, cols], preferred_element_type=F32)
                sv = sv * vg_ref[:, cols] + bs_ref[:, g:g + 1]
                prod_ref[rows, cols] = (u_ref[rows, cols].astype(F32) * sv).astype(BF16)

    o_ref[...] = jnp.dot(prod_ref[...], wo_ref[...], preferred_element_type=F32) + h_ref[...]


def _gmlp_layer(h, norm_g, w_in, v_norm, w_s, b_s, w_out):
    T, D = h.shape
    half = w_out.shape[0]
    tm, tn = 512, 1024
    n_u = half // tn
    z, ss = pl.pallas_call(
        functools.partial(_gmlp_in_kernel, n_u_tiles=n_u),
        grid=(T // tm, 2 * half // tn),
        in_specs=[pl.BlockSpec((tm, D), lambda i, j: (i, 0)),
                  pl.BlockSpec((1, D), lambda i, j: (0, 0)),
                  pl.BlockSpec((D, tn), lambda i, j: (0, j))],
        out_specs=[pl.BlockSpec((tm, tn), lambda i, j: (i, j)),
                   pl.BlockSpec((tm, 1), lambda i, j: (i, 0))],
        out_shape=[jax.ShapeDtypeStruct((T, 2 * half), BF16), jax.ShapeDtypeStruct((T, 1), F32)],
        scratch_shapes=[pltpu.VMEM((tm, D), BF16)],
        compiler_params=_cparams(("parallel", "arbitrary")),
        name="gmlp_in",
    )(h, norm_g.reshape(1, D), w_in.astype(BF16))

    tn2 = 512
    n_chunks = tm // GMLP_CHUNK
    return pl.pallas_call(
        functools.partial(_gmlp_out_kernel, n_chunks=n_chunks, half=half),
        grid=(T // tm, D // tn2),
        in_specs=[pl.BlockSpec((tm, half), lambda i, j: (i, 0)),
                  pl.BlockSpec((tm, half), lambda i, j: (i, 1)),
                  pl.BlockSpec((n_chunks, 1, GMLP_CHUNK), lambda i, j: (i, 0, 0)),
                  pl.BlockSpec((1, half), lambda i, j: (0, 0)),
                  pl.BlockSpec((GMLP_GROUPS, GMLP_CHUNK, GMLP_CHUNK), lambda i, j: (0, 0, 0)),
                  pl.BlockSpec((GMLP_CHUNK, GMLP_GROUPS), lambda i, j: (0, 0)),
                  pl.BlockSpec((half, tn2), lambda i, j: (0, j)),
                  pl.BlockSpec((tm, tn2), lambda i, j: (i, j))],
        out_specs=pl.BlockSpec((tm, tn2), lambda i, j: (i, j)),
        out_shape=jax.ShapeDtypeStruct((T, D), F32),
        scratch_shapes=[pltpu.VMEM((tm, half), BF16)],
        compiler_params=_cparams(("parallel", "arbitrary")),
        name="gmlp_out",
    )(z, z, ss.reshape(T // GMLP_CHUNK, 1, GMLP_CHUNK), v_norm.reshape(1, half), w_s, b_s.T,
      w_out.astype(BF16), h)


def _cmpx(v, i, l, desc):
    hi, lo = jnp.maximum(v[i], v[l]), jnp.minimum(v[i], v[l])
    v[i], v[l] = (hi, lo) if desc else (lo, hi)


def _bitonic_sort_desc(v):
    n = len(v)
    k = 2
    while k <= n:
        j = k // 2
        while j >= 1:
            for i in range(n):
                l = i ^ j
                if l > i:
                    _cmpx(v, i, l, (i & k) == 0)
            j //= 2
        k *= 2


def _bitonic_merge_desc(v):
    n = len(v)
    j = n // 2
    while j >= 1:
        for i in range(n):
            l = i ^ j
            if l > i:
                _cmpx(v, i, l, True)
        j //= 2


def _top16_sorted(rows):
    v = list(rows)
    _bitonic_sort_desc(v)
    for shift in (4, 2, 1):
        w = [pltpu.roll(x, shift, axis=0) for x in v]
        v = [jnp.maximum(v[i], w[PEER_TOPK - 1 - i]) for i in range(PEER_TOPK)]
        _bitonic_merge_desc(v)
    return v


def _peer_score_kernel(h_ref, g_ref, wq_ref, keys_ref, xn_ref, s_ref, st_ref, t2_ref):
    tm = h_ref.shape[0]
    xn = _rms_rows(h_ref[...], g_ref[...]).astype(BF16)
    xn_ref[...] = xn
    q = jnp.dot(xn, wq_ref[...], preferred_element_type=F32).astype(BF16)
    sub = lax.broadcasted_iota(jnp.int32, (SUBLANES, tm), 0)

    def pack(lst):
        out = lst[0]
        for s in range(1, SUBLANES):
            out = jnp.where(sub == s, lst[s], out)
        return out

    for h in range(PEER_HEADS):
        tops = []
        for p in range(2):
            hp = 2 * h + p
            qs = q[:, hp * PEER_KEY_DIM:(hp + 1) * PEER_KEY_DIM]
            s_t = lax.dot_general(keys_ref[hp], qs, _NT, preferred_element_type=F32)
            s_ref[hp] = s_t
            tops.append(_top16_sorted([s_t[SUBLANES * r:SUBLANES * (r + 1), :] for r in range(16)]))
        t1, t2 = tops
        t2lo, t2hi, t1hi = pack(t2[0:8]), pack(t2[8:16]), pack(t1[8:16])
        cands = [t1[0] + t2lo, t1[0] + t2hi] + [t1[a] + t2lo for a in range(1, 8)] + [t1hi + t2[0]]
        cands += [jnp.full((SUBLANES, tm), -jnp.inf, F32)] * (16 - len(cands))
        top = _top16_sorted(cands)
        z = jnp.ones((SUBLANES, tm), F32)
        for k in range(1, PEER_TOPK):
            z = z + jnp.exp(top[k] - top[0])
        for k, val in enumerate((top[PEER_TOPK - 1], t1[0], t2[0], z)):
            st_ref[pl.ds(k * PEER_HEADS + h, 1), :] = val[0:1, :]
        t2_ref[h, 0:SUBLANES, :] = t2lo
        t2_ref[h, SUBLANES:PEER_TOPK, :] = t2hi


def _peer_main_kernel(xnt_ref, u_ref, vt_ref, s_ref, st_ref, t2_ref, h_ref, o_ref, acc_ref, e1_ref, cnt_ref,
                      e2_ref, code_ref, hid_ref, w_ref, *, rows_per_tile):
    jj = pl.program_id(1)
    tm = xnt_ref.shape[1]
    te = hid_ref.shape[1]
    rc = 16

    @pl.when(jj == 0)
    def _():
        acc_ref[...] = jnp.zeros_like(acc_ref)
        for h in range(PEER_HEADS):
            thr = st_ref[h:h + 1, :]
            m1 = st_ref[PEER_HEADS + h:PEER_HEADS + h + 1, :]
            m2 = st_ref[2 * PEER_HEADS + h:2 * PEER_HEADS + h + 1, :]
            zz = st_ref[3 * PEER_HEADS + h:3 * PEER_HEADS + h + 1, :]
            s1 = s_ref[2 * h]
            s2 = s_ref[2 * h + 1]
            e1_ref[h] = 0.5 * jnp.exp(s1 - m1)
            e2_ref[h] = (jnp.exp(s2 - m2) / zz).astype(BF16)
            cnt = jnp.zeros(s1.shape, F32)
            code = jnp.zeros(s1.shape, F32)
            for b in range(PEER_TOPK):
                t2b = t2_ref[h, b:b + 1, :]
                cnt = jnp.where(s1 + t2b >= thr, float(b + 1), cnt)
                code = jnp.where(t2b > s2, float(b + 1), code)
            cnt_ref[h] = cnt
            code_ref[h] = code.astype(BF16)

    tn = 256
    n_pieces = tm // tn

    def project_pieces(slot):
        def piece(n):
            hid_ref[slot, :, n * tn:(n + 1) * tn] = jnp.dot(
                u_ref[slot * te:(slot + 1) * te, :], xnt_ref[:, n * tn:(n + 1) * tn],
                preferred_element_type=F32)
        return [functools.partial(piece, n) for n in range(n_pieces)]

    def gate_chunks(slot):
        tile = 2 * jj + slot
        out = []
        for r in range(rows_per_tile):
            i = tile * rows_per_tile + r
            cnts = [cnt_ref[h, pl.ds(i, 1), :].astype(BF16) for h in range(PEER_HEADS)]
            e1s = [e1_ref[h, pl.ds(i, 1), :].astype(BF16) for h in range(PEER_HEADS)]
            for c in range(PEER_NKEYS // rc):
                def chunk(r=r, c=c, cnts=cnts, e1s=e1s):
                    rows = slice(c * rc, (c + 1) * rc)
                    gate = jnp.zeros((rc, tm), BF16)
                    for h in range(PEER_HEADS):
                        gate = gate + jnp.where(code_ref[h, rows, :] < cnts[h], e2_ref[h, rows, :],
                                                jnp.zeros((), BF16)) * e1s[h]
                    out_rows = slice(r * PEER_NKEYS + c * rc, r * PEER_NKEYS + (c + 1) * rc)
                    x = hid_ref[slot, out_rows, :]
                    xg = x * gate.astype(F32)
                    inner = x * (GELU_C0 + GELU_C1 * (x * x))
                    w_ref[slot, out_rows, :] = (xg + xg * jnp.tanh(inner)).astype(BF16)
                out.append(chunk)
        return out

    def accumulate_pieces(slot):
        def piece(n):
            acc_ref[:, n * tn:(n + 1) * tn] += jnp.dot(vt_ref[:, slot * te:(slot + 1) * te],
                                                       w_ref[slot, :, n * tn:(n + 1) * tn],
                                                       preferred_element_type=F32)
        return [functools.partial(piece, n) for n in range(n_pieces)]

    def emit_interleaved(many, few):
        stride = len(many) // len(few)
        for k, f in enumerate(many):
            if k % stride == 0 and k // stride < len(few):
                few[k // stride]()
            f()

    for f in project_pieces(0):
        f()
    emit_interleaved(gate_chunks(0), project_pieces(1))
    emit_interleaved(gate_chunks(1), accumulate_pieces(0))
    for f in accumulate_pieces(1):
        f()

    @pl.when(jj == pl.num_programs(1) - 1)
    def _():
        o_ref[...] = acc_ref[...].T + h_ref[...]


def _peer_layer(h, norm_g, w_q, keys, u_tab, v_tab):
    T, D = h.shape
    n_exp = u_tab.shape[0]
    n_hp = 2 * PEER_HEADS
    tm1 = 256
    xn, s_t, stats, t2 = pl.pallas_call(
        _peer_score_kernel,
        grid=(T // tm1,),
        in_specs=[pl.BlockSpec((tm1, D), lambda i: (i, 0)),
                  pl.BlockSpec((1, D), lambda i: (0, 0)),
                  pl.BlockSpec((D, n_hp * PEER_KEY_DIM), lambda i: (0, 0)),
                  pl.BlockSpec((n_hp, PEER_NKEYS, PEER_KEY_DIM), lambda i: (0, 0, 0))],
        out_specs=[pl.BlockSpec((tm1, D), lambda i: (i, 0)),
                   pl.BlockSpec((n_hp, PEER_NKEYS, tm1), lambda i: (0, 0, i)),
                   pl.BlockSpec((4 * PEER_HEADS, tm1), lambda i: (0, i)),
                   pl.BlockSpec((PEER_HEADS, PEER_TOPK, tm1), lambda i: (0, 0, i))],
        out_shape=[jax.ShapeDtypeStruct((T, D), BF16),
                   jax.ShapeDtypeStruct((n_hp, PEER_NKEYS, T), F32),
                   jax.ShapeDtypeStruct((4 * PEER_HEADS, T), F32),
                   jax.ShapeDtypeStruct((PEER_HEADS, PEER_TOPK, T), F32)],
        compiler_params=_cparams(("parallel",)),
        name="peer_score",
    )(h, norm_g.reshape(1, D), w_q.astype(BF16),
      keys.reshape(n_hp, PEER_NKEYS, PEER_KEY_DIM).astype(BF16))

    tm, te = 512, 512
    n_tiles = n_exp // te
    n_pairs = n_tiles // 2
    once = pl.Buffered(1)
    return pl.pallas_call(
        functools.partial(_peer_main_kernel, rows_per_tile=te // PEER_NKEYS),
        grid=(T // tm, n_pairs),
        in_specs=[pl.BlockSpec((D, tm), lambda i, j: (0, i), pipeline_mode=once),
                  pl.BlockSpec((2 * te, D), lambda i, j: (j, 0)),
                  pl.BlockSpec((D, 2 * te), lambda i, j: (0, j)),
                  pl.BlockSpec((n_hp, PEER_NKEYS, tm), lambda i, j: (0, 0, i), pipeline_mode=once),
                  pl.BlockSpec((4 * PEER_HEADS, tm), lambda i, j: (0, i), pipeline_mode=once),
                  pl.BlockSpec((PEER_HEADS, PEER_TOPK, tm), lambda i, j: (0, 0, i), pipeline_mode=once),
                  pl.BlockSpec((tm, D), lambda i, j: (i, 0), pipeline_mode=once)],
        out_specs=pl.BlockSpec((tm, D), lambda i, j: (i, 0)),
        out_shape=jax.ShapeDtypeStruct((T, D), F32),
        scratch_shapes=[pltpu.VMEM((D, tm), F32),
                        pltpu.VMEM((PEER_HEADS, PEER_NKEYS, tm), F32),
                        pltpu.VMEM((PEER_HEADS, PEER_NKEYS, tm), F32),
                        pltpu.VMEM((PEER_HEADS, PEER_NKEYS, tm), BF16),
                        pltpu.VMEM((PEER_HEADS, PEER_NKEYS, tm), BF16),
                        pltpu.VMEM((2, te, tm), F32),
                        pltpu.VMEM((2, te, tm), BF16)],
        compiler_params=_cparams(("parallel", "arbitrary")),
        name="peer_main",
    )(xn.T, u_tab.astype(BF16), v_tab.T.astype(BF16), s_t, stats, t2, h)


def _kv_proj_kernel(h_ref, g_ref, w_ref, wvt_ref, kg_ref, raw_ref, ks_ref, kw_ref, vt_ref, *, tiles_per_seq):
    G, HD = NSA_KV_GROUPS, NSA_HEAD_DIM
    tk = h_ref.shape[0]
    xn = _rms_rows(h_ref[...], g_ref[...]).astype(BF16)
    y = jnp.dot(xn, w_ref[...], preferred_element_type=F32)
    col = lambda six, c: y[:, (six * G + c) * HD:(six * G + c + 1) * HD]
    nsbp = ks_ref.shape[2] - HD
    blk = lax.rem(pl.program_id(0), tiles_per_seq) * (tk // SLC_BLOCK) \
        + lax.broadcasted_iota(jnp.int32, (tk, nsbp), 0) // SLC_BLOCK
    onehot = jnp.where(lax.broadcasted_iota(jnp.int32, (tk, nsbp), 1) == blk, 1.0, 0.0).astype(BF16)
    for c in range(G):
        raw_ref[c] = col(0, c)
        raw_ref[G + c] = col(1, c)
        ks_ref[c, :, 0:HD] = _rms_rows(col(2, c), kg_ref[1:2, :]).astype(BF16)
        ks_ref[c, :, HD:] = onehot
        kw_ref[c] = _rms_rows(col(3, c), kg_ref[2:3, :]).astype(BF16)
    y_t = lax.dot_general(wvt_ref[...], xn, _NT, preferred_element_type=F32)
    for c in range(2 * G):
        vt_ref[c, 0] = y_t[c * HD:(c + 1) * HD, :].astype(BF16)


def _compress_kernel(x_ref, pe_ref, w1_ref, b1_ref, w2_ref, b2_ref, gain_ref, o_ref, *, is_key):
    x = x_ref[0]
    n, half_k = x.shape
    xa = (x + pe_ref[0, 0:1, :]).astype(BF16)
    xb = (x + pe_ref[0, 1:2, :]).astype(BF16)
    a = jnp.dot(xa, w1_ref[0, :half_k, :], preferred_element_type=F32)
    b = jnp.dot(xb, w1_ref[0, half_k:, :], preferred_element_type=F32)
    hid = _gelu(a + pltpu.roll(b, n - 1, axis=0) + b1_ref[0])
    y = jnp.dot(hid.astype(BF16), w2_ref[0], preferred_element_type=F32) + b2_ref[0]
    if is_key:
        o_ref[0] = _rms_rows(y, gain_ref[...]).astype(BF16)
    else:
        o_ref[0] = y.T.astype(BF16)


def _nsa_shared_kv(h, B, S, kv_norm, kv_w, kv_k_norm, phi_pe, phi_w1, phi_b1, phi_w2, phi_b2, tk):
    T, D = h.shape
    G, HD = NSA_KV_GROUPS, NSA_HEAD_DIM
    GW = G * HD
    w = kv_w.astype(BF16)
    sl = lambda six: w[:, six * GW:(six + 1) * GW]
    wk = jnp.concatenate([sl(0), sl(1), sl(2), sl(4)], axis=1)
    wvt = jnp.concatenate([sl(3), sl(5)], axis=1).T
    nsbp = -(-(S // SLC_BLOCK) // LANES) * LANES
    raw, ks, kw, vt = pl.pallas_call(
        functools.partial(_kv_proj_kernel, tiles_per_seq=S // tk),
        grid=(T // tk,),
        in_specs=[pl.BlockSpec((tk, D), lambda i: (i, 0)),
                  pl.BlockSpec((1, D), lambda i: (0, 0)),
                  pl.BlockSpec((D, 4 * GW), lambda i: (0, 0)),
                  pl.BlockSpec((2 * GW, D), lambda i: (0, 0)),
                  pl.BlockSpec((NSA_BRANCHES, HD), lambda i: (0, 0))],
        out_specs=[pl.BlockSpec((2 * G, tk, HD), lambda i: (0, i, 0)),
                   pl.BlockSpec((G, tk, HD + nsbp), lambda i: (0, i, 0)),
                   pl.BlockSpec((G, tk, HD), lambda i: (0, i, 0)),
                   pl.BlockSpec((2 * G, 1, HD, tk), lambda i: (0, i, 0, 0))],
        out_shape=[jax.ShapeDtypeStruct((2 * G, T, HD), F32),
                   jax.ShapeDtypeStruct((G, T, HD + nsbp), BF16),
                   jax.ShapeDtypeStruct((G, T, HD), BF16),
                   jax.ShapeDtypeStruct((2 * G, T // tk, HD, tk), BF16)],
        compiler_params=_cparams(("parallel",)),
        name="nsa_kv_proj",
    )(h, kv_norm.reshape(1, D), wk, wvt, kv_k_norm)

    ncb = S // CMP_STRIDE
    half_k = CMP_STRIDE * HD
    hidden = phi_w1.shape[-1]
    raw16 = raw.reshape(2 * G, T // CMP_STRIDE, half_k)
    pe = phi_pe.reshape(2, CMP_BLOCK // CMP_STRIDE, half_k)
    outs = []
    for kind in range(2):
        is_key = kind == 0
        o_block = (1, ncb, HD) if is_key else (1, HD, ncb)
        outs.append(pl.pallas_call(
            functools.partial(_compress_kernel, is_key=is_key),
            grid=(B, G),
            in_specs=[pl.BlockSpec((1, ncb, half_k), lambda b, g, kind=kind: (kind * G + g, b, 0)),
                      pl.BlockSpec((1, 2, half_k), lambda b, g, kind=kind: (kind, 0, 0)),
                      pl.BlockSpec((1, 2 * half_k, hidden), lambda b, g, kind=kind: (kind, 0, 0)),
                      pl.BlockSpec((1, 1, hidden), lambda b, g, kind=kind: (kind, 0, 0)),
                      pl.BlockSpec((1, hidden, HD), lambda b, g, kind=kind: (kind, 0, 0)),
                      pl.BlockSpec((1, 1, HD), lambda b, g, kind=kind: (kind, 0, 0)),
                      pl.BlockSpec((1, HD), lambda b, g: (0, 0))],
            out_specs=pl.BlockSpec(o_block, lambda b, g: (b * G + g, 0, 0)),
            out_shape=jax.ShapeDtypeStruct((B * G,) + o_block[1:], BF16),
            compiler_params=_cparams(("parallel", "parallel")),
            name="nsa_compress_k" if is_key else "nsa_compress_v",
        )(raw16, pe, phi_w1.astype(BF16), phi_b1.reshape(2, 1, hidden), phi_w2.astype(BF16),
          phi_b2.reshape(2, 1, HD), kv_k_norm[0:1]))
    return outs[0], outs[1], ks, kw, vt


def _nsa_in_kernel(h_ref, g_ref, wq_ref, wgt_ref, qg_ref, q_ref, gate_ref):
    HD = NSA_HEAD_DIM
    xn = _rms_rows(h_ref[...], g_ref[...]).astype(BF16)
    y = jnp.dot(xn, wq_ref[...], preferred_element_type=F32)
    qscale = HD ** -0.5 * LOG2E
    for hd in range(q_ref.shape[0]):
        q_ref[hd] = (_rms_rows(y[:, hd * HD:(hd + 1) * HD], qg_ref[...]) * qscale).astype(BF16)
    g_t = lax.dot_general(wgt_ref[...], xn, _NT, preferred_element_type=F32)
    gate_ref[...] = jax.nn.sigmoid(g_t)


def _nsa_attn_kernel(slopes_ref, q_ref, gate_ref, kc_ref, vct_ref, ks_ref, vst_ref, kw_ref, vwt_ref, ovt_ref,
                     o_ref, qat_ref, b_ref, s_ref, p_ref, m_ref, l_ref, acc_ref, oc_ref, os_ref, *,
                     tq, hpg):
    HD = NSA_HEAD_DIM
    tk = tq
    g = pl.program_id(1)
    qi = pl.program_id(2)
    t0 = qi * tq
    ncb = kc_ref.shape[1]
    nsb = ovt_ref.shape[0]
    nsbp = qat_ref.shape[0] - HD
    slopes = [slopes_ref[g * hpg + hh] for hh in range(hpg)]

    tpos_c = lax.broadcasted_iota(jnp.int32, (ncb, tq), 1) + t0
    cpos = lax.broadcasted_iota(jnp.int32, (ncb, tq), 0) * CMP_STRIDE + (CMP_BLOCK - 1)
    dist_c = tpos_c - cpos
    mask_c = dist_c >= 0
    dist_cf = dist_c.astype(F32)
    p_sum = jnp.zeros((ncb, tq), F32)
    for hh in range(hpg):
        s = lax.dot_general(kc_ref[0], q_ref[hh], _NT, preferred_element_type=F32) - slopes[hh] * dist_cf
        s = jnp.where(mask_c, s, MASK_FILL)
        e = jnp.where(mask_c, jnp.exp2(s - jnp.max(s, axis=0, keepdims=True)), 0.0)
        p = e / jnp.maximum(jnp.sum(e, axis=0, keepdims=True), TINY)
        oc_ref[hh] = jnp.dot(vct_ref[0], p.astype(BF16), preferred_element_type=F32)
        p_sum = p_sum + p

    imp = jnp.zeros((nsb, tq), F32)
    rem = p_sum
    for _ in range(3):
        part = rem.astype(BF16)
        imp = imp + jnp.dot(ovt_ref[...], part, preferred_element_type=F32)
        rem = rem - part.astype(F32)
    jb = lax.broadcasted_iota(jnp.int32, (nsb, tq), 0)
    cur = (lax.broadcasted_iota(jnp.int32, (nsb, tq), 1) + t0) // SLC_BLOCK
    forced = jnp.where(jb == 0, 1.0, jnp.where(jb == cur, 1.0, jnp.where(jb == cur - 1, 1.0, 0.0)))
    score = jnp.where(jb <= cur, imp + FORCE_BONUS * forced, -jnp.inf)
    blocks = [score[SUBLANES * r:SUBLANES * (r + 1), :] for r in range(nsb // SUBLANES)]
    pad = [jnp.full((SUBLANES, tq), -jnp.inf, F32)] * (N_SEL - len(blocks))
    kth = _top16_sorted(blocks + pad)[N_SEL - 1][0:1, :]
    above = [jnp.where(b > kth, 1.0, 0.0) for b in blocks]
    equal = [jnp.where(b == kth, 1.0, 0.0) for b in blocks]
    n_above = above[0]
    for a in above[1:]:
        n_above = n_above + a
    need = float(N_SEL) - jnp.sum(n_above, axis=0, keepdims=True)
    sub = lax.broadcasted_iota(jnp.int32, (SUBLANES, tq), 0)
    before = jnp.zeros((1, tq), F32)
    picked = []
    for a, e in zip(above, equal):
        run = e
        for sh in (1, 2, 4):
            run = run + jnp.where(sub >= sh, pltpu.roll(run, sh, axis=0), 0.0)
        first = jnp.where((run - e) + before < need, e, 0.0)
        picked.append(jnp.where(a + first > 0.5, 0.0, MASK_FILL))
        before = before + jnp.max(run, axis=0, keepdims=True)
    neg_sel = jnp.concatenate(picked, axis=0)

    bpt = tk // SLC_BLOCK
    tile_max = neg_sel
    for sh in range(1, bpt):
        tile_max = jnp.maximum(tile_max, pltpu.roll(neg_sel, nsb - sh, axis=0))
    hit = jnp.max(tile_max, axis=1, keepdims=True) > 0.5 * MASK_FILL
    row = lax.broadcasted_iota(jnp.int32, (nsb, 1), 0)
    bit = jnp.where(hit & (lax.rem(row, bpt) == 0),
                    lax.shift_left(jnp.int32(1), lax.rem(row // bpt, TILE_WORD_BITS)), 0).astype(F32)
    in_lo = row < TILE_WORD_BITS * bpt
    word_lo = jnp.sum(jnp.where(in_lo, bit, 0.0)).astype(jnp.int32)
    word_hi = jnp.sum(jnp.where(in_lo, 0.0, bit)).astype(jnp.int32)

    if nsbp > nsb:
        neg_sel = jnp.concatenate([neg_sel, jnp.zeros((nsbp - nsb, tq), F32)], axis=0)
    neg_sel = neg_sel.astype(BF16)

    wl = hpg * tq
    rc = 16
    kio = lax.broadcasted_iota(jnp.int32, (tk, wl), 0)
    kio_c = lax.broadcasted_iota(jnp.int32, (rc, wl), 0)
    tio_c = lax.rem(lax.broadcasted_iota(jnp.int32, (rc, wl), 1), tq)
    slope_row = jnp.concatenate([jnp.full((1, tq), slopes[hh], F32) for hh in range(hpg)], axis=1)
    for hh in range(hpg):
        qat_ref[0:HD, hh * tq:(hh + 1) * tq] = q_ref[hh].astype(F32).T.astype(BF16)
        qat_ref[HD:, hh * tq:(hh + 1) * tq] = neg_sel
    b_ref[...] = slope_row * kio.astype(F32)

    chunks = [slice(r * rc, (r + 1) * rc) for r in range(tk // rc)]

    def key_tile(k_ref, ki):
        return k_ref[0, pl.ds(pl.multiple_of(ki * tk, tk), tk), :]

    def bias_and_max(mask_kind):
        acc = None
        for r, ch in enumerate(chunks):
            t = s_ref[ch, :] + b_ref[ch, :]
            if mask_kind == "causal":
                t = t + jnp.where(kio_c + r * rc > tio_c, MASK_FILL, 0.0)
            elif mask_kind == "beyond":
                t = t + jnp.where(kio_c + r * rc <= tio_c, MASK_FILL, 0.0)
            s_ref[ch, :] = t
            acc = t if acc is None else jnp.maximum(acc, t)
        return jnp.max(acc, axis=0, keepdims=True)

    def exp_pass(shift):
        lsum = jnp.zeros((rc, wl), F32)
        for ch in chunks:
            p = jnp.exp2(s_ref[ch, :] + shift)
            p_ref[ch, :] = p.astype(BF16)
            lsum = lsum + p
        return jnp.sum(lsum, axis=0, keepdims=True)

    def first_tile(s, v_t):
        s_ref[...] = s
        m = bias_and_max("causal")
        m_ref[...] = m
        l_ref[...] = exp_pass(-m)
        acc_ref[...] = jnp.dot(v_t, p_ref[...], preferred_element_type=F32)

    def next_tile(s, back, v_t, mask_kind):
        s_ref[...] = s
        c = -slope_row * back
        m_old = m_ref[...]
        m_new = jnp.maximum(m_old, bias_and_max(mask_kind) + c)
        alpha = jnp.exp2(m_old - m_new)
        l_ref[...] = alpha * l_ref[...] + exp_pass(c - m_new)
        acc_ref[...] = alpha * acc_ref[...] + jnp.dot(v_t, p_ref[...], preferred_element_type=F32)
        m_ref[...] = m_new

    def sel_scores(ki):
        return jnp.dot(key_tile(ks_ref, ki), qat_ref[...], preferred_element_type=F32)

    def win_scores(ki):
        return jnp.dot(key_tile(kw_ref, ki), qat_ref[0:HD, :], preferred_element_type=F32)

    first_tile(sel_scores(qi), vst_ref[0, qi])

    def sel_body(step, carry):
        ki = qi - 1 - step
        word = jnp.where(ki < TILE_WORD_BITS, word_lo, word_hi)

        @pl.when((lax.shift_right_logical(word, lax.rem(ki, TILE_WORD_BITS)) & 1) != 0)
        def _():
            next_tile(sel_scores(ki), ((step + 1) * tk).astype(F32), vst_ref[0, ki], None)
        return carry

    lax.fori_loop(0, qi, sel_body, 0)
    os_ref[...] = acc_ref[...] / jnp.maximum(l_ref[...], TINY)

    first_tile(win_scores(qi), vwt_ref[0, qi])
    for back_tiles in range(1, WINDOW // tk + 1):
        @pl.when(qi >= back_tiles)
        def _(back_tiles=back_tiles):
            ki = qi - back_tiles
            next_tile(win_scores(ki), float(back_tiles * tk), vwt_ref[0, ki],
                      "beyond" if back_tiles == WINDOW // tk else None)
    o_w = acc_ref[...] / jnp.maximum(l_ref[...], TINY)

    for hh in range(hpg):
        cols = slice(hh * tq, (hh + 1) * tq)
        gates = [gate_ref[0, hh * NSA_BRANCHES + br:hh * NSA_BRANCHES + br + 1, :] for br in range(NSA_BRANCHES)]
        o_t = gates[0] * oc_ref[hh] + gates[1] * os_ref[:, cols] + gates[2] * o_w[:, cols]
        o_ref[:, hh * HD:(hh + 1) * HD] = o_t.T.astype(BF16)


def _mm_res_kernel(a_ref, w_ref, h_ref, o_ref):
    o_ref[...] = jnp.dot(a_ref[...], w_ref[...], preferred_element_type=F32) + h_ref[...]


def _nsa_layer(h, B, S, norm_g, w_in, q_norm, w_out, k_c, v_ct, ks, kw, vt, tq):
    T, D = h.shape
    G, HD = NSA_KV_GROUPS, NSA_HEAD_DIM
    width = w_out.shape[0]
    heads = width // HD
    hpg = heads // G
    n_gate = heads * NSA_BRANCHES
    w = w_in.astype(BF16)
    tm = 512
    q, gate_t = pl.pallas_call(
        _nsa_in_kernel,
        grid=(T // tm,),
        in_specs=[pl.BlockSpec((tm, D), lambda i: (i, 0)),
                  pl.BlockSpec((1, D), lambda i: (0, 0)),
                  pl.BlockSpec((D, width), lambda i: (0, 0)),
                  pl.BlockSpec((n_gate, D), lambda i: (0, 0)),
                  pl.BlockSpec((1, HD), lambda i: (0, 0))],
        out_specs=[pl.BlockSpec((heads, tm, HD), lambda i: (0, i, 0)),
                   pl.BlockSpec((n_gate, tm), lambda i: (0, i))],
        out_shape=[jax.ShapeDtypeStruct((heads, T, HD), BF16), jax.ShapeDtypeStruct((n_gate, T), F32)],
        compiler_params=_cparams(("parallel",)),
        name="nsa_in",
    )(h, norm_g.reshape(1, D), w[:, :width], w[:, width:].T, q_norm.reshape(1, HD))

    ncb = S // CMP_STRIDE
    nsb = S // SLC_BLOCK
    n_cmp = (S - CMP_BLOCK) // CMP_STRIDE + 1
    c_start = jnp.arange(ncb) * CMP_STRIDE
    c_pos = c_start + CMP_BLOCK - 1
    s_start = jnp.arange(nsb) * SLC_BLOCK
    ov_t = ((c_start[None, :] < s_start[:, None] + SLC_BLOCK) & (c_pos[None, :] >= s_start[:, None])
            & (jnp.arange(ncb)[None, :] < n_cmp)).astype(BF16)
    h_idx = jnp.arange(1, heads + 1, dtype=F32)
    slopes = 2.0 ** (-8.0 * h_idx / heads) * LOG2E
    nq = S // tq
    assert nq <= 2 * TILE_WORD_BITS, "the key-tile flags of a query tile are kept in two words"
    w_sel = ks.shape[2]
    o = pl.pallas_call(
        functools.partial(_nsa_attn_kernel, tq=tq, hpg=hpg),
        grid_spec=pltpu.PrefetchScalarGridSpec(
            num_scalar_prefetch=1,
            grid=(B, G, nq),
            in_specs=[pl.BlockSpec((hpg, tq, HD), lambda b, g, qi, sl: (g, b * nq + qi, 0)),
                      pl.BlockSpec((1, hpg * NSA_BRANCHES, tq), lambda b, g, qi, sl: (g, 0, b * nq + qi)),
                      pl.BlockSpec((1, ncb, HD), lambda b, g, qi, sl: (b * G + g, 0, 0)),
                      pl.BlockSpec((1, HD, ncb), lambda b, g, qi, sl: (b * G + g, 0, 0)),
                      pl.BlockSpec((1, S, w_sel), lambda b, g, qi, sl: (g, b, 0)),
                      pl.BlockSpec((1, nq, HD, tq), lambda b, g, qi, sl: (g, b, 0, 0)),
                      pl.BlockSpec((1, S, HD), lambda b, g, qi, sl: (g, b, 0)),
                      pl.BlockSpec((1, nq, HD, tq), lambda b, g, qi, sl: (G + g, b, 0, 0)),
                      pl.BlockSpec((nsb, ncb), lambda b, g, qi, sl: (0, 0))],
            out_specs=pl.BlockSpec((tq, hpg * HD), lambda b, g, qi, sl: (b * nq + qi, g)),
            scratch_shapes=[pltpu.VMEM((w_sel, hpg * tq), BF16),
                            pltpu.VMEM((tq, hpg * tq), F32),
                            pltpu.VMEM((tq, hpg * tq), F32),
                            pltpu.VMEM((tq, hpg * tq), BF16),
                            pltpu.VMEM((1, hpg * tq), F32),
                            pltpu.VMEM((1, hpg * tq), F32),
                            pltpu.VMEM((HD, hpg * tq), F32),
                            pltpu.VMEM((hpg, HD, tq), F32),
                            pltpu.VMEM((HD, hpg * tq), F32)]),
        out_shape=jax.ShapeDtypeStruct((T, width), BF16),
        compiler_params=_cparams(("parallel", "parallel", "arbitrary")),
        name="nsa_attn",
    )(slopes, q, gate_t.reshape(G, hpg * NSA_BRANCHES, T), k_c, v_ct, ks, vt, kw, vt, ov_t)

    return pl.pallas_call(
        _mm_res_kernel,
        grid=(T // tm,),
        in_specs=[pl.BlockSpec((tm, width), lambda i: (i, 0)),
                  pl.BlockSpec((width, D), lambda i: (0, 0)),
                  pl.BlockSpec((tm, D), lambda i: (i, 0))],
        out_specs=pl.BlockSpec((tm, D), lambda i: (i, 0)),
        out_shape=jax.ShapeDtypeStruct((T, D), F32),
        compiler_params=_cparams(("parallel",)),
        name="nsa_out",
    )(o, w_out.astype(BF16), h)


def kernel(x, a_norm, a_w_in, a_v_norm, a_w_s, a_b_s, a_w_out, kv_norm, kv_w, kv_k_norm, phi_pe, phi_w1, phi_b1, phi_w2, phi_b2, b_norm, b_w_in, b_q_norm, b_w_out, ffn_norm, peer_w_q, peer_keys, peer_u, peer_v):
    B, S, D = x.shape
    depth = ffn_norm.shape[0]
    n_a = a_norm.shape[0]
    tq = 256
    h = x.reshape(B * S, D)
    shared = None
    for layer in range(depth):
        if layer < n_a:
            a = layer
            h = _gmlp_layer(h, a_norm[a], a_w_in[a], a_v_norm[a], a_w_s[a], a_b_s[a], a_w_out[a])
        else:
            if shared is None:
                shared = _nsa_shared_kv(h, B, S, kv_norm, kv_w, kv_k_norm, phi_pe, phi_w1, phi_b1, phi_w2,
                                        phi_b2, tq)
            b = layer - n_a
            h = _nsa_layer(h, B, S, b_norm[b], b_w_in[b], b_q_norm[b], b_w_out[b], *shared, tq)
        h = _peer_layer(h, ffn_norm[layer], peer_w_q[layer], peer_keys[layer], peer_u[layer], peer_v[layer])
    return h.reshape(B, S, D)
```

```python
import functools
import math
from typing import NamedTuple

import jax
import jax.numpy as jnp
from jax import lax
from jax.experimental import pallas as pl
from jax.experimental.pallas import tpu as pltpu

F32 = jnp.float32
BF16 = jnp.bfloat16

GMLP_CHUNK = 128
GMLP_GROUPS = 8
NSA_HEAD_DIM = 128
NSA_KV_GROUPS = 4
NSA_BRANCHES = 3
CMP_BLOCK = 32
CMP_STRIDE = 16
SLC_BLOCK = 64
N_SEL = 16
WINDOW = 512
FORCE_BONUS = 1000.0
PEER_HEADS = 8
PEER_NKEYS = 128
PEER_TOPK = 16
PEER_KEY_DIM = 128
RMS_EPS = 1e-6
MASK_FILL = -1e30
TINY = float(jnp.finfo(jnp.float32).tiny)

V7X_VMEM_LIMIT_BYTES = 56 * 1024 * 1024
LANES = 128
SUBLANES = 8
LOG2E = 1.4426950408889634
TILE_WORD_BITS = 16
BF16_ROW_GROUP = 2 * SUBLANES


class _Tiles(NamedTuple):
    tokens: int = 512
    gmlp_in_cols: int = 1024
    gmlp_out_cols: int = 512
    peer_score_tokens: int = 256
    peer_experts: int = 512
    peer_tiles_per_step: int = 2
    peer_piece_tokens: int = 256
    nsa: int = 256


TILES = _Tiles()

_NT = (((1,), (1,)), ((), ()))


def _cparams(sem):
    return pltpu.CompilerParams(dimension_semantics=sem, vmem_limit_bytes=V7X_VMEM_LIMIT_BYTES)


GELU_C0 = math.sqrt(2.0 / math.pi)
GELU_C1 = 0.044715 * GELU_C0


def _gelu(x):
    return x * (0.5 * (1.0 + jnp.tanh(GELU_C0 * (x + 0.044715 * (x * x * x)))))


def _rms_rows(x, gain):
    r = lax.rsqrt(jnp.mean(x * x, axis=-1, keepdims=True) + RMS_EPS)
    return (x * r) * gain


def _gmlp_in_kernel(h_ref, g_ref, w_ref, z_ref, ss_ref, xn_ref, *, n_u_tiles):
    j = pl.program_id(1)

    @pl.when(j == 0)
    def _():
        xn_ref[...] = _rms_rows(h_ref[...], g_ref[...]).astype(BF16)
        ss_ref[...] = jnp.zeros_like(ss_ref)

    z = _gelu(jnp.dot(xn_ref[...], w_ref[...], preferred_element_type=F32))
    z_ref[...] = z.astype(BF16)

    @pl.when(j >= n_u_tiles)
    def _():
        ss_ref[...] += jnp.sum(z * z, axis=-1, keepdims=True)


def _gmlp_out_kernel(u_ref, v_ref, ss_ref, vg_ref, ws_ref, bs_ref, wo_ref, h_ref, o_ref, prod_ref, *,
                     n_chunks, half):
    j = pl.program_id(1)
    gdim = half // GMLP_GROUPS

    @pl.when(j == 0)
    def _():
        row = lax.broadcasted_iota(jnp.int32, (GMLP_CHUNK, GMLP_CHUNK), 0)
        col = lax.broadcasted_iota(jnp.int32, (GMLP_CHUNK, GMLP_CHUNK), 1)
        tril = row >= col
        for c in range(n_chunks):
            rows = slice(c * GMLP_CHUNK, (c + 1) * GMLP_CHUNK)
            rinv = lax.rsqrt(ss_ref[c] * (1.0 / half) + RMS_EPS)
            for g in range(GMLP_GROUPS):
                cols = slice(g * gdim, (g + 1) * gdim)
                wc = (jnp.where(tril, ws_ref[g], 0.0) * rinv).astype(BF16)
                sv = jnp.dot(wc, v_ref[rows, cols], preferred_element_type=F32)
                sv = sv * vg_ref[:, cols] + bs_ref[:, g:g + 1]
                prod_ref[rows, cols] = (u_ref[rows, cols].astype(F32) * sv).astype(BF16)

    o_ref[...] = jnp.dot(prod_ref[...], wo_ref[...], preferred_element_type=F32) + h_ref[...]


def _gmlp_layer(h, norm_g, w_in, v_norm, w_s, b_s, w_out):
    T, D = h.shape
    half = w_out.shape[0]
    tm, tn = TILES.tokens, TILES.gmlp_in_cols
    n_u = half // tn
    z, ss = pl.pallas_call(
        functools.partial(_gmlp_in_kernel, n_u_tiles=n_u),
        grid=(T // tm, 2 * half // tn),
        in_specs=[pl.BlockSpec((tm, D), lambda i, j: (i, 0)),
                  pl.BlockSpec((1, D), lambda i, j: (0, 0)),
                  pl.BlockSpec((D, tn), lambda i, j: (0, j))],
        out_specs=[pl.BlockSpec((tm, tn), lambda i, j: (i, j)),
                   pl.BlockSpec((tm, 1), lambda i, j: (i, 0))],
        out_shape=[jax.ShapeDtypeStruct((T, 2 * half), BF16), jax.ShapeDtypeStruct((T, 1), F32)],
        scratch_shapes=[pltpu.VMEM((tm, D), BF16)],
        compiler_params=_cparams(("parallel", "arbitrary")),
        name="gmlp_in",
    )(h, norm_g.reshape(1, D), w_in.astype(BF16))

    tn2 = TILES.gmlp_out_cols
    n_chunks = tm // GMLP_CHUNK
    return pl.pallas_call(
        functools.partial(_gmlp_out_kernel, n_chunks=n_chunks, half=half),
        grid=(T // tm, D // tn2),
        in_specs=[pl.BlockSpec((tm, half), lambda i, j: (i, 0)),
                  pl.BlockSpec((tm, half), lambda i, j: (i, 1)),
                  pl.BlockSpec((n_chunks, 1, GMLP_CHUNK), lambda i, j: (i, 0, 0)),
                  pl.BlockSpec((1, half), lambda i, j: (0, 0)),
                  pl.BlockSpec((GMLP_GROUPS, GMLP_CHUNK, GMLP_CHUNK), lambda i, j: (0, 0, 0)),
                  pl.BlockSpec((GMLP_CHUNK, GMLP_GROUPS), lambda i, j: (0, 0)),
                  pl.BlockSpec((half, tn2), lambda i, j: (0, j)),
                  pl.BlockSpec((tm, tn2), lambda i, j: (i, j))],
        out_specs=pl.BlockSpec((tm, tn2), lambda i, j: (i, j)),
        out_shape=jax.ShapeDtypeStruct((T, D), F32),
        scratch_shapes=[pltpu.VMEM((tm, half), BF16)],
        compiler_params=_cparams(("parallel", "arbitrary")),
        name="gmlp_out",
    )(z, z, ss.reshape(T // GMLP_CHUNK, 1, GMLP_CHUNK), v_norm.reshape(1, half), w_s, b_s.T,
      w_out.astype(BF16), h)


def _cmpx(v, i, l, desc):
    hi, lo = jnp.maximum(v[i], v[l]), jnp.minimum(v[i], v[l])
    v[i], v[l] = (hi, lo) if desc else (lo, hi)


def _bitonic_sort_desc(v):
    n = len(v)
    k = 2
    while k <= n:
        j = k // 2
        while j >= 1:
            for i in range(n):
                l = i ^ j
                if l > i:
                    _cmpx(v, i, l, (i & k) == 0)
            j //= 2
        k *= 2


def _bitonic_merge_desc(v):
    n = len(v)
    j = n // 2
    while j >= 1:
        for i in range(n):
            l = i ^ j
            if l > i:
                _cmpx(v, i, l, True)
        j //= 2


def _top16_sorted(rows):
    v = list(rows)
    _bitonic_sort_desc(v)
    for shift in (4, 2, 1):
        w = [pltpu.roll(x, shift, axis=0) for x in v]
        v = [jnp.maximum(v[i], w[PEER_TOPK - 1 - i]) for i in range(PEER_TOPK)]
        _bitonic_merge_desc(v)
    return v


def _peer_score_kernel(h_ref, g_ref, wq_ref, keys_ref, xn_ref, s_ref, st_ref, t2_ref):
    tm = h_ref.shape[0]
    xn = _rms_rows(h_ref[...], g_ref[...]).astype(BF16)
    xn_ref[...] = xn
    q = jnp.dot(xn, wq_ref[...], preferred_element_type=F32).astype(BF16)
    sub = lax.broadcasted_iota(jnp.int32, (SUBLANES, tm), 0)

    def pack(lst):
        out = lst[0]
        for s in range(1, SUBLANES):
            out = jnp.where(sub == s, lst[s], out)
        return out

    for h in range(PEER_HEADS):
        tops = []
        for p in range(2):
            hp = 2 * h + p
            qs = q[:, hp * PEER_KEY_DIM:(hp + 1) * PEER_KEY_DIM]
            s_t = lax.dot_general(keys_ref[hp], qs, _NT, preferred_element_type=F32)
            s_ref[hp] = s_t
            tops.append(_top16_sorted([s_t[SUBLANES * r:SUBLANES * (r + 1), :] for r in range(16)]))
        t1, t2 = tops
        t2lo, t2hi, t1hi = pack(t2[0:8]), pack(t2[8:16]), pack(t1[8:16])
        cands = [t1[0] + t2lo, t1[0] + t2hi] + [t1[a] + t2lo for a in range(1, 8)] + [t1hi + t2[0]]
        cands += [jnp.full((SUBLANES, tm), -jnp.inf, F32)] * (16 - len(cands))
        top = _top16_sorted(cands)
        z = jnp.ones((SUBLANES, tm), F32)
        for k in range(1, PEER_TOPK):
            z = z + jnp.exp(top[k] - top[0])
        for k, val in enumerate((top[PEER_TOPK - 1], t1[0], t2[0], z)):
            st_ref[pl.ds(k * PEER_HEADS + h, 1), :] = val[0:1, :]
        t2_ref[h, 0:SUBLANES, :] = t2lo
        t2_ref[h, SUBLANES:PEER_TOPK, :] = t2hi


def _peer_main_kernel(xnt_ref, u_ref, vt_ref, s_ref, st_ref, t2_ref, h_ref, o_ref, acc_ref, e1_ref, cnt_ref,
                      e2_ref, code_ref, hid_ref, w_ref, *, rows_per_tile):
    jj = pl.program_id(1)
    tm = xnt_ref.shape[1]
    te = hid_ref.shape[1]
    rc = BF16_ROW_GROUP

    @pl.when(jj == 0)
    def _():
        acc_ref[...] = jnp.zeros_like(acc_ref)
        for h in range(PEER_HEADS):
            thr = st_ref[h:h + 1, :]
            m1 = st_ref[PEER_HEADS + h:PEER_HEADS + h + 1, :]
            m2 = st_ref[2 * PEER_HEADS + h:2 * PEER_HEADS + h + 1, :]
            zz = st_ref[3 * PEER_HEADS + h:3 * PEER_HEADS + h + 1, :]
            s1 = s_ref[2 * h]
            s2 = s_ref[2 * h + 1]
            e1_ref[h] = 0.5 * jnp.exp(s1 - m1)
            e2_ref[h] = (jnp.exp(s2 - m2) / zz).astype(BF16)
            cnt = jnp.zeros(s1.shape, F32)
            code = jnp.zeros(s1.shape, F32)
            for b in range(PEER_TOPK):
                t2b = t2_ref[h, b:b + 1, :]
                cnt = jnp.where(s1 + t2b >= thr, float(b + 1), cnt)
                code = jnp.where(t2b > s2, float(b + 1), code)
            cnt_ref[h] = cnt
            code_ref[h] = code.astype(BF16)

    tn = TILES.peer_piece_tokens
    n_pieces = tm // tn

    def project_pieces(slot):
        def piece(n):
            hid_ref[slot, :, n * tn:(n + 1) * tn] = jnp.dot(
                u_ref[slot * te:(slot + 1) * te, :], xnt_ref[:, n * tn:(n + 1) * tn],
                preferred_element_type=F32)
        return [functools.partial(piece, n) for n in range(n_pieces)]

    def gate_chunks(slot):
        tile = hid_ref.shape[0] * jj + slot
        out = []
        for r in range(rows_per_tile):
            i = tile * rows_per_tile + r
            cnts = [cnt_ref[h, pl.ds(i, 1), :].astype(BF16) for h in range(PEER_HEADS)]
            e1s = [e1_ref[h, pl.ds(i, 1), :].astype(BF16) for h in range(PEER_HEADS)]
            for c in range(PEER_NKEYS // rc):
                def chunk(r=r, c=c, cnts=cnts, e1s=e1s):
                    rows = slice(c * rc, (c + 1) * rc)
                    gate = jnp.zeros((rc, tm), BF16)
                    for h in range(PEER_HEADS):
                        gate = gate + jnp.where(code_ref[h, rows, :] < cnts[h], e2_ref[h, rows, :],
                                                jnp.zeros((), BF16)) * e1s[h]
                    out_rows = slice(r * PEER_NKEYS + c * rc, r * PEER_NKEYS + (c + 1) * rc)
                    x = hid_ref[slot, out_rows, :]
                    xg = x * gate.astype(F32)
                    inner = x * (GELU_C0 + GELU_C1 * (x * x))
                    w_ref[slot, out_rows, :] = (xg + xg * jnp.tanh(inner)).astype(BF16)
                out.append(chunk)
        return out

    def accumulate_pieces(slot):
        def piece(n):
            acc_ref[:, n * tn:(n + 1) * tn] += jnp.dot(vt_ref[:, slot * te:(slot + 1) * te],
                                                       w_ref[slot, :, n * tn:(n + 1) * tn],
                                                       preferred_element_type=F32)
        return [functools.partial(piece, n) for n in range(n_pieces)]

    def emit_interleaved(many, few):
        stride = max(len(many) // max(len(few), 1), 1)
        for k, f in enumerate(many):
            if k % stride == 0 and k // stride < len(few):
                few[k // stride]()
            f()

    n_slots = hid_ref.shape[0]
    for f in project_pieces(0):
        f()
    for k in range(n_slots):
        matmuls = accumulate_pieces(k - 1) if k >= 1 else []
        if k + 1 < n_slots:
            matmuls = matmuls + project_pieces(k + 1)
        emit_interleaved(gate_chunks(k), matmuls)
    for f in accumulate_pieces(n_slots - 1):
        f()

    @pl.when(jj == pl.num_programs(1) - 1)
    def _():
        o_ref[...] = acc_ref[...].T + h_ref[...]


def _peer_layer(h, norm_g, w_q, keys, u_tab, v_tab):
    T, D = h.shape
    n_exp = u_tab.shape[0]
    n_hp = 2 * PEER_HEADS
    tm1 = TILES.peer_score_tokens
    xn, s_t, stats, t2 = pl.pallas_call(
        _peer_score_kernel,
        grid=(T // tm1,),
        in_specs=[pl.BlockSpec((tm1, D), lambda i: (i, 0)),
                  pl.BlockSpec((1, D), lambda i: (0, 0)),
                  pl.BlockSpec((D, n_hp * PEER_KEY_DIM), lambda i: (0, 0)),
                  pl.BlockSpec((n_hp, PEER_NKEYS, PEER_KEY_DIM), lambda i: (0, 0, 0))],
        out_specs=[pl.BlockSpec((tm1, D), lambda i: (i, 0)),
                   pl.BlockSpec((n_hp, PEER_NKEYS, tm1), lambda i: (0, 0, i)),
                   pl.BlockSpec((4 * PEER_HEADS, tm1), lambda i: (0, i)),
                   pl.BlockSpec((PEER_HEADS, PEER_TOPK, tm1), lambda i: (0, 0, i))],
        out_shape=[jax.ShapeDtypeStruct((T, D), BF16),
                   jax.ShapeDtypeStruct((n_hp, PEER_NKEYS, T), F32),
                   jax.ShapeDtypeStruct((4 * PEER_HEADS, T), F32),
                   jax.ShapeDtypeStruct((PEER_HEADS, PEER_TOPK, T), F32)],
        compiler_params=_cparams(("parallel",)),
        name="peer_score",
    )(h, norm_g.reshape(1, D), w_q.astype(BF16),
      keys.reshape(n_hp, PEER_NKEYS, PEER_KEY_DIM).astype(BF16))

    tm, te, tps = TILES.tokens, TILES.peer_experts, TILES.peer_tiles_per_step
    n_tiles = n_exp // te
    n_pairs = n_tiles // tps
    once = pl.Buffered(1)
    return pl.pallas_call(
        functools.partial(_peer_main_kernel, rows_per_tile=te // PEER_NKEYS),
        grid=(T // tm, n_pairs),
        in_specs=[pl.BlockSpec((D, tm), lambda i, j: (0, i), pipeline_mode=once),
                  pl.BlockSpec((tps * te, D), lambda i, j: (j, 0)),
                  pl.BlockSpec((D, tps * te), lambda i, j: (0, j)),
                  pl.BlockSpec((n_hp, PEER_NKEYS, tm), lambda i, j: (0, 0, i), pipeline_mode=once),
                  pl.BlockSpec((4 * PEER_HEADS, tm), lambda i, j: (0, i), pipeline_mode=once),
                  pl.BlockSpec((PEER_HEADS, PEER_TOPK, tm), lambda i, j: (0, 0, i), pipeline_mode=once),
                  pl.BlockSpec((tm, D), lambda i, j: (i, 0), pipeline_mode=once)],
        out_specs=pl.BlockSpec((tm, D), lambda i, j: (i, 0)),
        out_shape=jax.ShapeDtypeStruct((T, D), F32),
        scratch_shapes=[pltpu.VMEM((D, tm), F32),
                        pltpu.VMEM((PEER_HEADS, PEER_NKEYS, tm), F32),
                        pltpu.VMEM((PEER_HEADS, PEER_NKEYS, tm), F32),
                        pltpu.VMEM((PEER_HEADS, PEER_NKEYS, tm), BF16),
                        pltpu.VMEM((PEER_HEADS, PEER_NKEYS, tm), BF16),
                        pltpu.VMEM((tps, te, tm), F32),
                        pltpu.VMEM((tps, te, tm), BF16)],
        compiler_params=_cparams(("parallel", "arbitrary")),
        name="peer_main",
    )(xn.T, u_tab.astype(BF16), v_tab.T.astype(BF16), s_t, stats, t2, h)


def _kv_proj_kernel(h_ref, g_ref, w_ref, wvt_ref, kg_ref, raw_ref, ks_ref, kw_ref, vt_ref, *, tiles_per_seq):
    G, HD = NSA_KV_GROUPS, NSA_HEAD_DIM
    tk = h_ref.shape[0]
    xn = _rms_rows(h_ref[...], g_ref[...]).astype(BF16)
    y = jnp.dot(xn, w_ref[...], preferred_element_type=F32)
    col = lambda six, c: y[:, (six * G + c) * HD:(six * G + c + 1) * HD]
    nsbp = ks_ref.shape[2] - HD
    blk = lax.rem(pl.program_id(0), tiles_per_seq) * (tk // SLC_BLOCK) \
        + lax.broadcasted_iota(jnp.int32, (tk, nsbp), 0) // SLC_BLOCK
    onehot = jnp.where(lax.broadcasted_iota(jnp.int32, (tk, nsbp), 1) == blk, 1.0, 0.0).astype(BF16)
    for c in range(G):
        raw_ref[c] = col(0, c)
        raw_ref[G + c] = col(1, c)
        ks_ref[c, :, 0:HD] = _rms_rows(col(2, c), kg_ref[1:2, :]).astype(BF16)
        ks_ref[c, :, HD:] = onehot
        kw_ref[c] = _rms_rows(col(3, c), kg_ref[2:3, :]).astype(BF16)
    y_t = lax.dot_general(wvt_ref[...], xn, _NT, preferred_element_type=F32)
    for c in range(2 * G):
        vt_ref[c, 0] = y_t[c * HD:(c + 1) * HD, :].astype(BF16)


def _compress_kernel(x_ref, pe_ref, w1_ref, b1_ref, w2_ref, b2_ref, gain_ref, o_ref, *, is_key):
    x = x_ref[0]
    n, half_k = x.shape
    xa = (x + pe_ref[0, 0:1, :]).astype(BF16)
    xb = (x + pe_ref[0, 1:2, :]).astype(BF16)
    a = jnp.dot(xa, w1_ref[0, :half_k, :], preferred_element_type=F32)
    b = jnp.dot(xb, w1_ref[0, half_k:, :], preferred_element_type=F32)
    hid = _gelu(a + pltpu.roll(b, n - 1, axis=0) + b1_ref[0])
    y = jnp.dot(hid.astype(BF16), w2_ref[0], preferred_element_type=F32) + b2_ref[0]
    if is_key:
        o_ref[0] = _rms_rows(y, gain_ref[...]).astype(BF16)
    else:
        o_ref[0] = y.T.astype(BF16)


def _nsa_shared_kv(h, B, S, kv_norm, kv_w, kv_k_norm, phi_pe, phi_w1, phi_b1, phi_w2, phi_b2, tk):
    T, D = h.shape
    G, HD = NSA_KV_GROUPS, NSA_HEAD_DIM
    GW = G * HD
    w = kv_w.astype(BF16)
    sl = lambda six: w[:, six * GW:(six + 1) * GW]
    wk = jnp.concatenate([sl(0), sl(1), sl(2), sl(4)], axis=1)
    wvt = jnp.concatenate([sl(3), sl(5)], axis=1).T
    nsbp = -(-(S // SLC_BLOCK) // LANES) * LANES
    raw, ks, kw, vt = pl.pallas_call(
        functools.partial(_kv_proj_kernel, tiles_per_seq=S // tk),
        grid=(T // tk,),
        in_specs=[pl.BlockSpec((tk, D), lambda i: (i, 0)),
                  pl.BlockSpec((1, D), lambda i: (0, 0)),
                  pl.BlockSpec((D, 4 * GW), lambda i: (0, 0)),
                  pl.BlockSpec((2 * GW, D), lambda i: (0, 0)),
                  pl.BlockSpec((NSA_BRANCHES, HD), lambda i: (0, 0))],
        out_specs=[pl.BlockSpec((2 * G, tk, HD), lambda i: (0, i, 0)),
                   pl.BlockSpec((G, tk, HD + nsbp), lambda i: (0, i, 0)),
                   pl.BlockSpec((G, tk, HD), lambda i: (0, i, 0)),
                   pl.BlockSpec((2 * G, 1, HD, tk), lambda i: (0, i, 0, 0))],
        out_shape=[jax.ShapeDtypeStruct((2 * G, T, HD), F32),
                   jax.ShapeDtypeStruct((G, T, HD + nsbp), BF16),
                   jax.ShapeDtypeStruct((G, T, HD), BF16),
                   jax.ShapeDtypeStruct((2 * G, T // tk, HD, tk), BF16)],
        compiler_params=_cparams(("parallel",)),
        name="nsa_kv_proj",
    )(h, kv_norm.reshape(1, D), wk, wvt, kv_k_norm)

    ncb = S // CMP_STRIDE
    half_k = CMP_STRIDE * HD
    hidden = phi_w1.shape[-1]
    raw16 = raw.reshape(2 * G, T // CMP_STRIDE, half_k)
    pe = phi_pe.reshape(2, CMP_BLOCK // CMP_STRIDE, half_k)
    outs = []
    for kind in range(2):
        is_key = kind == 0
        o_block = (1, ncb, HD) if is_key else (1, HD, ncb)
        outs.append(pl.pallas_call(
            functools.partial(_compress_kernel, is_key=is_key),
            grid=(B, G),
            in_specs=[pl.BlockSpec((1, ncb, half_k), lambda b, g, kind=kind: (kind * G + g, b, 0)),
                      pl.BlockSpec((1, 2, half_k), lambda b, g, kind=kind: (kind, 0, 0)),
                      pl.BlockSpec((1, 2 * half_k, hidden), lambda b, g, kind=kind: (kind, 0, 0)),
                      pl.BlockSpec((1, 1, hidden), lambda b, g, kind=kind: (kind, 0, 0)),
                      pl.BlockSpec((1, hidden, HD), lambda b, g, kind=kind: (kind, 0, 0)),
                      pl.BlockSpec((1, 1, HD), lambda b, g, kind=kind: (kind, 0, 0)),
                      pl.BlockSpec((1, HD), lambda b, g: (0, 0))],
            out_specs=pl.BlockSpec(o_block, lambda b, g: (b * G + g, 0, 0)),
            out_shape=jax.ShapeDtypeStruct((B * G,) + o_block[1:], BF16),
            compiler_params=_cparams(("parallel", "parallel")),
            name="nsa_compress_k" if is_key else "nsa_compress_v",
        )(raw16, pe, phi_w1.astype(BF16), phi_b1.reshape(2, 1, hidden), phi_w2.astype(BF16),
          phi_b2.reshape(2, 1, HD), kv_k_norm[0:1]))
    return outs[0], outs[1], ks, kw, vt


def _nsa_in_kernel(h_ref, g_ref, wq_ref, wgt_ref, qg_ref, q_ref, gate_ref):
    HD = NSA_HEAD_DIM
    xn = _rms_rows(h_ref[...], g_ref[...]).astype(BF16)
    y = jnp.dot(xn, wq_ref[...], preferred_element_type=F32)
    qscale = HD ** -0.5 * LOG2E
    for hd in range(q_ref.shape[0]):
        q_ref[hd] = (_rms_rows(y[:, hd * HD:(hd + 1) * HD], qg_ref[...]) * qscale).astype(BF16)
    g_t = lax.dot_general(wgt_ref[...], xn, _NT, preferred_element_type=F32)
    gate_ref[...] = jax.nn.sigmoid(g_t)


def _nsa_attn_kernel(slopes_ref, q_ref, gate_ref, kc_ref, vct_ref, ks_ref, vst_ref, kw_ref, vwt_ref, ovt_ref,
                     o_ref, qat_ref, b_ref, s_ref, p_ref, m_ref, l_ref, acc_ref, oc_ref, os_ref, *,
                     tq, hpg):
    HD = NSA_HEAD_DIM
    tk = tq
    g = pl.program_id(1)
    qi = pl.program_id(2)
    t0 = qi * tq
    ncb = kc_ref.shape[1]
    nsb = ovt_ref.shape[0]
    nsbp = qat_ref.shape[0] - HD
    slopes = [slopes_ref[g * hpg + hh] for hh in range(hpg)]

    tpos_c = lax.broadcasted_iota(jnp.int32, (ncb, tq), 1) + t0
    cpos = lax.broadcasted_iota(jnp.int32, (ncb, tq), 0) * CMP_STRIDE + (CMP_BLOCK - 1)
    dist_c = tpos_c - cpos
    mask_c = dist_c >= 0
    dist_cf = dist_c.astype(F32)
    p_sum = jnp.zeros((ncb, tq), F32)
    for hh in range(hpg):
        s = lax.dot_general(kc_ref[0], q_ref[hh], _NT, preferred_element_type=F32) - slopes[hh] * dist_cf
        s = jnp.where(mask_c, s, MASK_FILL)
        e = jnp.where(mask_c, jnp.exp2(s - jnp.max(s, axis=0, keepdims=True)), 0.0)
        p = e / jnp.maximum(jnp.sum(e, axis=0, keepdims=True), TINY)
        oc_ref[hh] = jnp.dot(vct_ref[0], p.astype(BF16), preferred_element_type=F32)
        p_sum = p_sum + p

    imp = jnp.zeros((nsb, tq), F32)
    rem = p_sum
    for _ in range(3):
        part = rem.astype(BF16)
        imp = imp + jnp.dot(ovt_ref[...], part, preferred_element_type=F32)
        rem = rem - part.astype(F32)
    jb = lax.broadcasted_iota(jnp.int32, (nsb, tq), 0)
    cur = (lax.broadcasted_iota(jnp.int32, (nsb, tq), 1) + t0) // SLC_BLOCK
    forced = jnp.where(jb == 0, 1.0, jnp.where(jb == cur, 1.0, jnp.where(jb == cur - 1, 1.0, 0.0)))
    score = jnp.where(jb <= cur, imp + FORCE_BONUS * forced, -jnp.inf)
    blocks = [score[SUBLANES * r:SUBLANES * (r + 1), :] for r in range(nsb // SUBLANES)]
    pad = [jnp.full((SUBLANES, tq), -jnp.inf, F32)] * (N_SEL - len(blocks))
    kth = _top16_sorted(blocks + pad)[N_SEL - 1][0:1, :]
    above = [jnp.where(b > kth, 1.0, 0.0) for b in blocks]
    equal = [jnp.where(b == kth, 1.0, 0.0) for b in blocks]
    n_above = above[0]
    for a in above[1:]:
        n_above = n_above + a
    need = float(N_SEL) - jnp.sum(n_above, axis=0, keepdims=True)
    sub = lax.broadcasted_iota(jnp.int32, (SUBLANES, tq), 0)
    before = jnp.zeros((1, tq), F32)
    picked = []
    for a, e in zip(above, equal):
        run = e
        for sh in (1, 2, 4):
            run = run + jnp.where(sub >= sh, pltpu.roll(run, sh, axis=0), 0.0)
        first = jnp.where((run - e) + before < need, e, 0.0)
        picked.append(jnp.where(a + first > 0.5, 0.0, MASK_FILL))
        before = before + jnp.max(run, axis=0, keepdims=True)
    neg_sel = jnp.concatenate(picked, axis=0)

    bpt = tk // SLC_BLOCK
    tile_max = neg_sel
    for sh in range(1, bpt):
        tile_max = jnp.maximum(tile_max, pltpu.roll(neg_sel, nsb - sh, axis=0))
    hit = jnp.max(tile_max, axis=1, keepdims=True) > 0.5 * MASK_FILL
    row = lax.broadcasted_iota(jnp.int32, (nsb, 1), 0)
    bit = jnp.where(hit & (lax.rem(row, bpt) == 0),
                    lax.shift_left(jnp.int32(1), lax.rem(row // bpt, TILE_WORD_BITS)), 0).astype(F32)
    in_lo = row < TILE_WORD_BITS * bpt
    word_lo = jnp.sum(jnp.where(in_lo, bit, 0.0)).astype(jnp.int32)
    word_hi = jnp.sum(jnp.where(in_lo, 0.0, bit)).astype(jnp.int32)

    if nsbp > nsb:
        neg_sel = jnp.concatenate([neg_sel, jnp.zeros((nsbp - nsb, tq), F32)], axis=0)
    neg_sel = neg_sel.astype(BF16)

    wl = hpg * tq
    rc = BF16_ROW_GROUP
    kio_c = lax.broadcasted_iota(jnp.int32, (rc, wl), 0)
    tio_c = lax.rem(lax.broadcasted_iota(jnp.int32, (rc, wl), 1), tq)
    slope_row = jnp.concatenate([jnp.full((1, tq), slopes[hh], F32) for hh in range(hpg)], axis=1)
    for hh in range(hpg):
        qat_ref[0:HD, hh * tq:(hh + 1) * tq] = q_ref[hh].astype(F32).T.astype(BF16)
        qat_ref[HD:, hh * tq:(hh + 1) * tq] = neg_sel

    @pl.when(qi == 0)
    def _():
        b_ref[...] = slope_row * lax.broadcasted_iota(jnp.int32, (tk, wl), 0).astype(F32)

    chunks = [slice(r * rc, (r + 1) * rc) for r in range(tk // rc)]

    def key_tile(k_ref, ki):
        return k_ref[0, pl.ds(pl.multiple_of(ki * tk, tk), tk), :]

    def bias_and_max(mask_kind):
        acc = None
        for r, ch in enumerate(chunks):
            t = s_ref[ch, :] + b_ref[ch, :]
            if mask_kind == "causal":
                t = t + jnp.where(kio_c + r * rc > tio_c, MASK_FILL, 0.0)
            elif mask_kind == "beyond":
                t = t + jnp.where(kio_c + r * rc <= tio_c, MASK_FILL, 0.0)
            s_ref[ch, :] = t
            acc = t if acc is None else jnp.maximum(acc, t)
        return jnp.max(acc, axis=0, keepdims=True)

    def exp_pass(shift):
        lsum = jnp.zeros((rc, wl), F32)
        for ch in chunks:
            p = jnp.exp2(s_ref[ch, :] + shift)
            p_ref[ch, :] = p.astype(BF16)
            lsum = lsum + p
        return jnp.sum(lsum, axis=0, keepdims=True)

    def first_tile(s, v_t):
        s_ref[...] = s
        m = bias_and_max("causal")
        m_ref[...] = m
        l_ref[...] = exp_pass(-m)
        acc_ref[...] = jnp.dot(v_t, p_ref[...], preferred_element_type=F32)

    def next_tile(s, back, v_t, mask_kind):
        s_ref[...] = s
        c = -slope_row * back
        m_old = m_ref[...]
        m_new = jnp.maximum(m_old, bias_and_max(mask_kind) + c)
        alpha = jnp.exp2(m_old - m_new)
        l_ref[...] = alpha * l_ref[...] + exp_pass(c - m_new)
        acc_ref[...] = alpha * acc_ref[...] + jnp.dot(v_t, p_ref[...], preferred_element_type=F32)
        m_ref[...] = m_new

    def sel_scores(ki):
        return jnp.dot(key_tile(ks_ref, ki), qat_ref[...], preferred_element_type=F32)

    def win_scores(ki):
        return jnp.dot(key_tile(kw_ref, ki), qat_ref[0:HD, :], preferred_element_type=F32)

    first_tile(sel_scores(qi), vst_ref[0, qi])

    def sel_body(step, carry):
        ki = qi - 1 - step
        word = jnp.where(ki < TILE_WORD_BITS, word_lo, word_hi)

        @pl.when((lax.shift_right_logical(word, lax.rem(ki, TILE_WORD_BITS)) & 1) != 0)
        def _():
            next_tile(sel_scores(ki), ((step + 1) * tk).astype(F32), vst_ref[0, ki], None)
        return carry

    lax.fori_loop(0, qi, sel_body, 0)
    os_ref[...] = acc_ref[...] / jnp.maximum(l_ref[...], TINY)

    first_tile(win_scores(qi), vwt_ref[0, qi])
    for back_tiles in range(1, WINDOW // tk + 1):
        @pl.when(qi >= back_tiles)
        def _(back_tiles=back_tiles):
            ki = qi - back_tiles
            next_tile(win_scores(ki), float(back_tiles * tk), vwt_ref[0, ki],
                      "beyond" if back_tiles == WINDOW // tk else None)
    o_w = acc_ref[...] / jnp.maximum(l_ref[...], TINY)

    for hh in range(hpg):
        cols = slice(hh * tq, (hh + 1) * tq)
        gates = [gate_ref[0, hh * NSA_BRANCHES + br:hh * NSA_BRANCHES + br + 1, :] for br in range(NSA_BRANCHES)]
        o_t = gates[0] * oc_ref[hh] + gates[1] * os_ref[:, cols] + gates[2] * o_w[:, cols]
        o_ref[:, hh * HD:(hh + 1) * HD] = o_t.T.astype(BF16)


def _mm_res_kernel(a_ref, w_ref, h_ref, o_ref):
    o_ref[...] = jnp.dot(a_ref[...], w_ref[...], preferred_element_type=F32) + h_ref[...]


def _nsa_layer(h, B, S, norm_g, w_in, q_norm, w_out, k_c, v_ct, ks, kw, vt, tq):
    T, D = h.shape
    G, HD = NSA_KV_GROUPS, NSA_HEAD_DIM
    width = w_out.shape[0]
    heads = width // HD
    hpg = heads // G
    n_gate = heads * NSA_BRANCHES
    w = w_in.astype(BF16)
    tm = TILES.tokens
    q, gate_t = pl.pallas_call(
        _nsa_in_kernel,
        grid=(T // tm,),
        in_specs=[pl.BlockSpec((tm, D), lambda i: (i, 0)),
                  pl.BlockSpec((1, D), lambda i: (0, 0)),
                  pl.BlockSpec((D, width), lambda i: (0, 0)),
                  pl.BlockSpec((n_gate, D), lambda i: (0, 0)),
                  pl.BlockSpec((1, HD), lambda i: (0, 0))],
        out_specs=[pl.BlockSpec((heads, tm, HD), lambda i: (0, i, 0)),
                   pl.BlockSpec((n_gate, tm), lambda i: (0, i))],
        out_shape=[jax.ShapeDtypeStruct((heads, T, HD), BF16), jax.ShapeDtypeStruct((n_gate, T), F32)],
        compiler_params=_cparams(("parallel",)),
        name="nsa_in",
    )(h, norm_g.reshape(1, D), w[:, :width], w[:, width:].T, q_norm.reshape(1, HD))

    ncb = S // CMP_STRIDE
    nsb = S // SLC_BLOCK
    n_cmp = (S - CMP_BLOCK) // CMP_STRIDE + 1
    c_start = jnp.arange(ncb) * CMP_STRIDE
    c_pos = c_start + CMP_BLOCK - 1
    s_start = jnp.arange(nsb) * SLC_BLOCK
    ov_t = ((c_start[None, :] < s_start[:, None] + SLC_BLOCK) & (c_pos[None, :] >= s_start[:, None])
            & (jnp.arange(ncb)[None, :] < n_cmp)).astype(BF16)
    h_idx = jnp.arange(1, heads + 1, dtype=F32)
    slopes = 2.0 ** (-8.0 * h_idx / heads) * LOG2E
    nq = S // tq
    assert nq <= 2 * TILE_WORD_BITS, "the key-tile flags of a query tile are kept in two words"
    w_sel = ks.shape[2]
    o = pl.pallas_call(
        functools.partial(_nsa_attn_kernel, tq=tq, hpg=hpg),
        grid_spec=pltpu.PrefetchScalarGridSpec(
            num_scalar_prefetch=1,
            grid=(B, G, nq),
            in_specs=[pl.BlockSpec((hpg, tq, HD), lambda b, g, qi, sl: (g, b * nq + qi, 0)),
                      pl.BlockSpec((1, hpg * NSA_BRANCHES, tq), lambda b, g, qi, sl: (g, 0, b * nq + qi)),
                      pl.BlockSpec((1, ncb, HD), lambda b, g, qi, sl: (b * G + g, 0, 0)),
                      pl.BlockSpec((1, HD, ncb), lambda b, g, qi, sl: (b * G + g, 0, 0)),
                      pl.BlockSpec((1, S, w_sel), lambda b, g, qi, sl: (g, b, 0)),
                      pl.BlockSpec((1, nq, HD, tq), lambda b, g, qi, sl: (g, b, 0, 0)),
                      pl.BlockSpec((1, S, HD), lambda b, g, qi, sl: (g, b, 0)),
                      pl.BlockSpec((1, nq, HD, tq), lambda b, g, qi, sl: (G + g, b, 0, 0)),
                      pl.BlockSpec((nsb, ncb), lambda b, g, qi, sl: (0, 0))],
            out_specs=pl.BlockSpec((tq, hpg * HD), lambda b, g, qi, sl: (b * nq + qi, g)),
            scratch_shapes=[pltpu.VMEM((w_sel, hpg * tq), BF16),
                            pltpu.VMEM((tq, hpg * tq), F32),
                            pltpu.VMEM((tq, hpg * tq), F32),
                            pltpu.VMEM((tq, hpg * tq), BF16),
                            pltpu.VMEM((1, hpg * tq), F32),
                            pltpu.VMEM((1, hpg * tq), F32),
                            pltpu.VMEM((HD, hpg * tq), F32),
                            pltpu.VMEM((hpg, HD, tq), F32),
                            pltpu.VMEM((HD, hpg * tq), F32)]),
        out_shape=jax.ShapeDtypeStruct((T, width), BF16),
        compiler_params=_cparams(("parallel", "parallel", "arbitrary")),
        name="nsa_attn",
    )(slopes, q, gate_t.reshape(G, hpg * NSA_BRANCHES, T), k_c, v_ct, ks, vt, kw, vt, ov_t)

    return pl.pallas_call(
        _mm_res_kernel,
        grid=(T // tm,),
        in_specs=[pl.BlockSpec((tm, width), lambda i: (i, 0)),
                  pl.BlockSpec((width, D), lambda i: (0, 0)),
                  pl.BlockSpec((tm, D), lambda i: (i, 0))],
        out_specs=pl.BlockSpec((tm, D), lambda i: (i, 0)),
        out_shape=jax.ShapeDtypeStruct((T, D), F32),
        compiler_params=_cparams(("parallel",)),
        name="nsa_out",
    )(o, w_out.astype(BF16), h)


def kernel(x, a_norm, a_w_in, a_v_norm, a_w_s, a_b_s, a_w_out, kv_norm, kv_w, kv_k_norm, phi_pe, phi_w1, phi_b1, phi_w2, phi_b2, b_norm, b_w_in, b_q_norm, b_w_out, ffn_norm, peer_w_q, peer_keys, peer_u, peer_v):
    B, S, D = x.shape
    depth = ffn_norm.shape[0]
    n_a = a_norm.shape[0]
    tq = TILES.nsa
    h = x.reshape(B * S, D)
    shared = None
    for layer in range(depth):
        if layer < n_a:
            a = layer
            h = _gmlp_layer(h, a_norm[a], a_w_in[a], a_v_norm[a], a_w_s[a], a_b_s[a], a_w_out[a])
        else:
            if shared is None:
                shared = _nsa_shared_kv(h, B, S, kv_norm, kv_w, kv_k_norm, phi_pe, phi_w1, phi_b1, phi_w2,
                                        phi_b2, tq)
            b = layer - n_a
            h = _nsa_layer(h, B, S, b_norm[b], b_w_in[b], b_q_norm[b], b_w_out[b], *shared, tq)
        h = _peer_layer(h, ffn_norm[layer], peer_w_q[layer], peer_keys[layer], peer_u[layer], peer_v[layer])
    return h.reshape(B, S, D)
```

```python
import functools
import math
from typing import NamedTuple

import jax
import jax.numpy as jnp
from jax import lax
from jax.experimental import pallas as pl
from jax.experimental.pallas import tpu as pltpu

F32 = jnp.float32
BF16 = jnp.bfloat16

GMLP_CHUNK = 128
GMLP_GROUPS = 8
NSA_HEAD_DIM = 128
NSA_KV_GROUPS = 4
NSA_BRANCHES = 3
CMP_BLOCK = 32
CMP_STRIDE = 16
SLC_BLOCK = 64
N_SEL = 16
WINDOW = 512
FORCE_BONUS = 1000.0
PEER_HEADS = 8
PEER_NKEYS = 128
PEER_TOPK = 16
PEER_KEY_DIM = 128
RMS_EPS = 1e-6
MASK_FILL = -1e30
TINY = float(jnp.finfo(jnp.float32).tiny)

V7X_VMEM_LIMIT_BYTES = 56 * 1024 * 1024
LANES = 128
SUBLANES = 8
LOG2E = 1.4426950408889634
TILE_WORD_BITS = 16
BF16_ROW_GROUP = 2 * SUBLANES


class _Tiles(NamedTuple):
    tokens: int = 512
    gmlp_in_cols: int = 1024
    gmlp_out_cols: int = 512
    peer_score_tokens: int = 256
    peer_experts: int = 512
    peer_tiles_per_step: int = 2
    peer_piece_tokens: int = 256
    nsa: int = 256


TILES = _Tiles()

_NT = (((1,), (1,)), ((), ()))


def _cparams(sem):
    return pltpu.CompilerParams(dimension_semantics=sem, vmem_limit_bytes=V7X_VMEM_LIMIT_BYTES)


GELU_C0 = math.sqrt(2.0 / math.pi)
GELU_C1 = 0.044715 * GELU_C0


def _gelu(x):
    return x * (0.5 * (1.0 + jnp.tanh(GELU_C0 * (x + 0.044715 * (x * x * x)))))


def _rms_rows(x, gain):
    r = lax.rsqrt(jnp.mean(x * x, axis=-1, keepdims=True) + RMS_EPS)
    return (x * r) * gain


def _gmlp_in_kernel(h_ref, g_ref, w_ref, z_ref, ss_ref, xn_ref, *, n_u_tiles):
    j = pl.program_id(1)

    @pl.when(j == 0)
    def _():
        xn_ref[...] = _rms_rows(h_ref[...], g_ref[...]).astype(BF16)
        ss_ref[...] = jnp.zeros_like(ss_ref)

    z = _gelu(jnp.dot(xn_ref[...], w_ref[...], preferred_element_type=F32))
    z_ref[...] = z.astype(BF16)

    @pl.when(j >= n_u_tiles)
    def _():
        ss_ref[...] += jnp.sum(z * z, axis=-1, keepdims=True)


def _gmlp_out_kernel(u_ref, v_ref, ss_ref, vg_ref, ws_ref, bs_ref, wo_ref, h_ref, o_ref, prod_ref, *,
                     n_chunks, half):
    j = pl.program_id(1)
    gdim = half // GMLP_GROUPS

    @pl.when(j == 0)
    def _():
        row = lax.broadcasted_iota(jnp.int32, (GMLP_CHUNK, GMLP_CHUNK), 0)
        col = lax.broadcasted_iota(jnp.int32, (GMLP_CHUNK, GMLP_CHUNK), 1)
        tril = row >= col
        for c in range(n_chunks):
            rows = slice(c * GMLP_CHUNK, (c + 1) * GMLP_CHUNK)
            rinv = lax.rsqrt(ss_ref[c] * (1.0 / half) + RMS_EPS)
            for g in range(GMLP_GROUPS):
                cols = slice(g * gdim, (g + 1) * gdim)
                wc = (jnp.where(tril, ws_ref[g], 0.0) * rinv).astype(BF16)
                sv = jnp.dot(wc, v_ref[rows, cols], preferred_element_type=F32)
                sv = sv * vg_ref[:, cols] + bs_ref[:, g:g + 1]
                prod_ref[rows, cols] = (u_ref[rows, cols].astype(F32) * sv).astype(BF16)

    o_ref[...] = jnp.dot(prod_ref[...], wo_ref[...], preferred_element_type=F32) + h_ref[...]


def _gmlp_layer(h, norm_g, w_in, v_norm, w_s, b_s, w_out):
    T, D = h.shape
    half = w_out.shape[0]
    tm, tn = TILES.tokens, TILES.gmlp_in_cols
    n_u = half // tn
    z, ss = pl.pallas_call(
        functools.partial(_gmlp_in_kernel, n_u_tiles=n_u),
        grid=(T // tm, 2 * half // tn),
        in_specs=[pl.BlockSpec((tm, D), lambda i, j: (i, 0)),
                  pl.BlockSpec((1, D), lambda i, j: (0, 0)),
                  pl.BlockSpec((D, tn), lambda i, j: (0, j))],
        out_specs=[pl.BlockSpec((tm, tn), lambda i, j: (i, j)),
                   pl.BlockSpec((tm, 1), lambda i, j: (i, 0))],
        out_shape=[jax.ShapeDtypeStruct((T, 2 * half), BF16), jax.ShapeDtypeStruct((T, 1), F32)],
        scratch_shapes=[pltpu.VMEM((tm, D), BF16)],
        compiler_params=_cparams(("parallel", "arbitrary")),
        name="gmlp_in",
    )(h, norm_g.reshape(1, D), w_in.astype(BF16))

    tn2 = TILES.gmlp_out_cols
    n_chunks = tm // GMLP_CHUNK
    return pl.pallas_call(
        functools.partial(_gmlp_out_kernel, n_chunks=n_chunks, half=half),
        grid=(T // tm, D // tn2),
        in_specs=[pl.BlockSpec((tm, half), lambda i, j: (i, 0)),
                  pl.BlockSpec((tm, half), lambda i, j: (i, 1)),
                  pl.BlockSpec((n_chunks, 1, GMLP_CHUNK), lambda i, j: (i, 0, 0)),
                  pl.BlockSpec((1, half), lambda i, j: (0, 0)),
                  pl.BlockSpec((GMLP_GROUPS, GMLP_CHUNK, GMLP_CHUNK), lambda i, j: (0, 0, 0)),
                  pl.BlockSpec((GMLP_CHUNK, GMLP_GROUPS), lambda i, j: (0, 0)),
                  pl.BlockSpec((half, tn2), lambda i, j: (0, j)),
                  pl.BlockSpec((tm, tn2), lambda i, j: (i, j))],
        out_specs=pl.BlockSpec((tm, tn2), lambda i, j: (i, j)),
        out_shape=jax.ShapeDtypeStruct((T, D), F32),
        scratch_shapes=[pltpu.VMEM((tm, half), BF16)],
        compiler_params=_cparams(("parallel", "arbitrary")),
        name="gmlp_out",
    )(z, z, ss.reshape(T // GMLP_CHUNK, 1, GMLP_CHUNK), v_norm.reshape(1, half), w_s, b_s.T,
      w_out.astype(BF16), h)


def _cmpx(v, i, l, desc):
    hi, lo = jnp.maximum(v[i], v[l]), jnp.minimum(v[i], v[l])
    v[i], v[l] = (hi, lo) if desc else (lo, hi)


def _bitonic_sort_desc(v):
    n = len(v)
    k = 2
    while k <= n:
        j = k // 2
        while j >= 1:
            for i in range(n):
                l = i ^ j
                if l > i:
                    _cmpx(v, i, l, (i & k) == 0)
            j //= 2
        k *= 2


def _bitonic_merge_desc(v):
    n = len(v)
    j = n // 2
    while j >= 1:
        for i in range(n):
            l = i ^ j
            if l > i:
                _cmpx(v, i, l, True)
        j //= 2


def _top16_sorted(rows):
    v = list(rows)
    _bitonic_sort_desc(v)
    for shift in (4, 2, 1):
        w = [pltpu.roll(x, shift, axis=0) for x in v]
        v = [jnp.maximum(v[i], w[PEER_TOPK - 1 - i]) for i in range(PEER_TOPK)]
        _bitonic_merge_desc(v)
    return v


def _peer_score_kernel(h_ref, g_ref, wq_ref, keys_ref, xn_ref, s_ref, st_ref, t2_ref):
    tm = h_ref.shape[0]
    xn = _rms_rows(h_ref[...], g_ref[...]).astype(BF16)
    xn_ref[...] = xn
    q = jnp.dot(xn, wq_ref[...], preferred_element_type=F32).astype(BF16)
    sub = lax.broadcasted_iota(jnp.int32, (SUBLANES, tm), 0)

    def pack(lst):
        out = lst[0]
        for s in range(1, SUBLANES):
            out = jnp.where(sub == s, lst[s], out)
        return out

    for h in range(PEER_HEADS):
        tops = []
        for p in range(2):
            hp = 2 * h + p
            qs = q[:, hp * PEER_KEY_DIM:(hp + 1) * PEER_KEY_DIM]
            s_t = lax.dot_general(keys_ref[hp], qs, _NT, preferred_element_type=F32)
            s_ref[hp] = s_t
            tops.append(_top16_sorted([s_t[SUBLANES * r:SUBLANES * (r + 1), :] for r in range(16)]))
        t1, t2 = tops
        t2lo, t2hi, t1hi = pack(t2[0:8]), pack(t2[8:16]), pack(t1[8:16])
        cands = [t1[0] + t2lo, t1[0] + t2hi] + [t1[a] + t2lo for a in range(1, 8)] + [t1hi + t2[0]]
        cands += [jnp.full((SUBLANES, tm), -jnp.inf, F32)] * (16 - len(cands))
        top = _top16_sorted(cands)
        z = jnp.ones((SUBLANES, tm), F32)
        for k in range(1, PEER_TOPK):
            z = z + jnp.exp(top[k] - top[0])
        for k, val in enumerate((top[PEER_TOPK - 1], t1[0], t2[0], z)):
            st_ref[pl.ds(k * PEER_HEADS + h, 1), :] = val[0:1, :]
        t2_ref[h, 0:SUBLANES, :] = t2lo
        t2_ref[h, SUBLANES:PEER_TOPK, :] = t2hi


def _peer_main_kernel(xnt_ref, u_ref, vt_ref, s_ref, st_ref, t2_ref, h_ref, o_ref, acc_ref, e1_ref, cnt_ref,
                      e2_ref, code_ref, hid_ref, w_ref, *, rows_per_tile):
    jj = pl.program_id(1)
    tm = xnt_ref.shape[1]
    te = hid_ref.shape[1]
    rc = BF16_ROW_GROUP

    @pl.when(jj == 0)
    def _():
        acc_ref[...] = jnp.zeros_like(acc_ref)
        for h in range(PEER_HEADS):
            thr = st_ref[h:h + 1, :]
            m1 = st_ref[PEER_HEADS + h:PEER_HEADS + h + 1, :]
            m2 = st_ref[2 * PEER_HEADS + h:2 * PEER_HEADS + h + 1, :]
            zz = st_ref[3 * PEER_HEADS + h:3 * PEER_HEADS + h + 1, :]
            s1 = s_ref[2 * h]
            s2 = s_ref[2 * h + 1]
            e1_ref[h] = 0.5 * jnp.exp(s1 - m1)
            e2_ref[h] = (jnp.exp(s2 - m2) / zz).astype(BF16)
            cnt = jnp.zeros(s1.shape, F32)
            code = jnp.zeros(s1.shape, F32)
            for b in range(PEER_TOPK):
                t2b = t2_ref[h, b:b + 1, :]
                cnt = jnp.where(s1 + t2b >= thr, float(b + 1), cnt)
                code = jnp.where(t2b > s2, float(b + 1), code)
            cnt_ref[h] = cnt
            code_ref[h] = code.astype(BF16)

    tn = TILES.peer_piece_tokens
    n_pieces = tm // tn

    def project_pieces(slot):
        def piece(n):
            hid_ref[slot, :, n * tn:(n + 1) * tn] = jnp.dot(
                u_ref[slot * te:(slot + 1) * te, :], xnt_ref[:, n * tn:(n + 1) * tn],
                preferred_element_type=F32)
        return [functools.partial(piece, n) for n in range(n_pieces)]

    def gate_chunks(slot):
        tile = hid_ref.shape[0] * jj + slot
        out = []
        for r in range(rows_per_tile):
            i = tile * rows_per_tile + r
            cnts = [cnt_ref[h, pl.ds(i, 1), :].astype(BF16) for h in range(PEER_HEADS)]
            e1s = [e1_ref[h, pl.ds(i, 1), :].astype(BF16) for h in range(PEER_HEADS)]
            for c in range(PEER_NKEYS // rc):
                def chunk(r=r, c=c, cnts=cnts, e1s=e1s):
                    rows = slice(c * rc, (c + 1) * rc)
                    gate = jnp.zeros((rc, tm), BF16)
                    for h in range(PEER_HEADS):
                        gate = gate + jnp.where(code_ref[h, rows, :] < cnts[h], e2_ref[h, rows, :],
                                                jnp.zeros((), BF16)) * e1s[h]
                    out_rows = slice(r * PEER_NKEYS + c * rc, r * PEER_NKEYS + (c + 1) * rc)
                    x = hid_ref[slot, out_rows, :]
                    xg = x * gate.astype(F32)
                    inner = x * (GELU_C0 + GELU_C1 * (x * x))
                    w_ref[slot, out_rows, :] = (xg + xg * jnp.tanh(inner)).astype(BF16)
                out.append(chunk)
        return out

    def accumulate_pieces(slot):
        def piece(n):
            acc_ref[:, n * tn:(n + 1) * tn] += jnp.dot(vt_ref[:, slot * te:(slot + 1) * te],
                                                       w_ref[slot, :, n * tn:(n + 1) * tn],
                                                       preferred_element_type=F32)
        return [functools.partial(piece, n) for n in range(n_pieces)]

    def emit_interleaved(many, few):
        stride = max(len(many) // max(len(few), 1), 1)
        for k, f in enumerate(many):
            if k % stride == 0 and k // stride < len(few):
                few[k // stride]()
            f()

    n_slots = hid_ref.shape[0]
    for f in project_pieces(0):
        f()
    for k in range(n_slots):
        matmuls = accumulate_pieces(k - 1) if k >= 1 else []
        if k + 1 < n_slots:
            matmuls = matmuls + project_pieces(k + 1)
        emit_interleaved(gate_chunks(k), matmuls)
    for f in accumulate_pieces(n_slots - 1):
        f()

    @pl.when(jj == pl.num_programs(1) - 1)
    def _():
        o_ref[...] = acc_ref[...].T + h_ref[...]


def _peer_layer(h, norm_g, w_q, keys, u_tab, v_tab):
    T, D = h.shape
    n_exp = u_tab.shape[0]
    n_hp = 2 * PEER_HEADS
    tm1 = TILES.peer_score_tokens
    xn, s_t, stats, t2 = pl.pallas_call(
        _peer_score_kernel,
        grid=(T // tm1,),
        in_specs=[pl.BlockSpec((tm1, D), lambda i: (i, 0)),
                  pl.BlockSpec((1, D), lambda i: (0, 0)),
                  pl.BlockSpec((D, n_hp * PEER_KEY_DIM), lambda i: (0, 0)),
                  pl.BlockSpec((n_hp, PEER_NKEYS, PEER_KEY_DIM), lambda i: (0, 0, 0))],
        out_specs=[pl.BlockSpec((tm1, D), lambda i: (i, 0)),
                   pl.BlockSpec((n_hp, PEER_NKEYS, tm1), lambda i: (0, 0, i)),
                   pl.BlockSpec((4 * PEER_HEADS, tm1), lambda i: (0, i)),
                   pl.BlockSpec((PEER_HEADS, PEER_TOPK, tm1), lambda i: (0, 0, i))],
        out_shape=[jax.ShapeDtypeStruct((T, D), BF16),
                   jax.ShapeDtypeStruct((n_hp, PEER_NKEYS, T), F32),
                   jax.ShapeDtypeStruct((4 * PEER_HEADS, T), F32),
                   jax.ShapeDtypeStruct((PEER_HEADS, PEER_TOPK, T), F32)],
        compiler_params=_cparams(("parallel",)),
        name="peer_score",
    )(h, norm_g.reshape(1, D), w_q.astype(BF16),
      keys.reshape(n_hp, PEER_NKEYS, PEER_KEY_DIM).astype(BF16))

    tm, te, tps = TILES.tokens, TILES.peer_experts, TILES.peer_tiles_per_step
    n_tiles = n_exp // te
    n_pairs = n_tiles // tps
    once = pl.Buffered(1)
    return pl.pallas_call(
        functools.partial(_peer_main_kernel, rows_per_tile=te // PEER_NKEYS),
        grid=(T // tm, n_pairs),
        in_specs=[pl.BlockSpec((D, tm), lambda i, j: (0, i), pipeline_mode=once),
                  pl.BlockSpec((tps * te, D), lambda i, j: (j, 0)),
                  pl.BlockSpec((D, tps * te), lambda i, j: (0, j)),
                  pl.BlockSpec((n_hp, PEER_NKEYS, tm), lambda i, j: (0, 0, i), pipeline_mode=once),
                  pl.BlockSpec((4 * PEER_HEADS, tm), lambda i, j: (0, i), pipeline_mode=once),
                  pl.BlockSpec((PEER_HEADS, PEER_TOPK, tm), lambda i, j: (0, 0, i), pipeline_mode=once),
                  pl.BlockSpec((tm, D), lambda i, j: (i, 0), pipeline_mode=once)],
        out_specs=pl.BlockSpec((tm, D), lambda i, j: (i, 0)),
        out_shape=jax.ShapeDtypeStruct((T, D), F32),
        scratch_shapes=[pltpu.VMEM((D, tm), F32),
                        pltpu.VMEM((PEER_HEADS, PEER_NKEYS, tm), F32),
                        pltpu.VMEM((PEER_HEADS, PEER_NKEYS, tm), F32),
                        pltpu.VMEM((PEER_HEADS, PEER_NKEYS, tm), BF16),
                        pltpu.VMEM((PEER_HEADS, PEER_NKEYS, tm), BF16),
                        pltpu.VMEM((tps, te, tm), F32),
                        pltpu.VMEM((tps, te, tm), BF16)],
        compiler_params=_cparams(("parallel", "arbitrary")),
        name="peer_main",
    )(xn.T, u_tab.astype(BF16), v_tab.T.astype(BF16), s_t, stats, t2, h)


def _kv_proj_kernel(h_ref, g_ref, w_ref, wvt_ref, kg_ref, raw_ref, ks_ref, kw_ref, vt_ref, *, tiles_per_seq):
    G, HD = NSA_KV_GROUPS, NSA_HEAD_DIM
    tk = h_ref.shape[0]
    xn = _rms_rows(h_ref[...], g_ref[...]).astype(BF16)
    y = jnp.dot(xn, w_ref[...], preferred_element_type=F32)
    col = lambda six, c: y[:, (six * G + c) * HD:(six * G + c + 1) * HD]
    nsbp = ks_ref.shape[2] - HD
    blk = lax.rem(pl.program_id(0), tiles_per_seq) * (tk // SLC_BLOCK) \
        + lax.broadcasted_iota(jnp.int32, (tk, nsbp), 0) // SLC_BLOCK
    onehot = jnp.where(lax.broadcasted_iota(jnp.int32, (tk, nsbp), 1) == blk, 1.0, 0.0).astype(BF16)
    for c in range(G):
        raw_ref[c] = col(0, c)
        raw_ref[G + c] = col(1, c)
        ks_ref[c, :, 0:HD] = _rms_rows(col(2, c), kg_ref[1:2, :]).astype(BF16)
        ks_ref[c, :, HD:] = onehot
        kw_ref[c] = _rms_rows(col(3, c), kg_ref[2:3, :]).astype(BF16)
    y_t = lax.dot_general(wvt_ref[...], xn, _NT, preferred_element_type=F32)
    for c in range(2 * G):
        vt_ref[c, 0] = y_t[c * HD:(c + 1) * HD, :].astype(BF16)


def _compress_kernel(x_ref, pe_ref, w1_ref, b1_ref, w2_ref, b2_ref, gain_ref, o_ref, *, is_key):
    x = x_ref[0]
    n, half_k = x.shape
    xa = (x + pe_ref[0, 0:1, :]).astype(BF16)
    xb = (x + pe_ref[0, 1:2, :]).astype(BF16)
    a = jnp.dot(xa, w1_ref[0, :half_k, :], preferred_element_type=F32)
    b = jnp.dot(xb, w1_ref[0, half_k:, :], preferred_element_type=F32)
    hid = _gelu(a + pltpu.roll(b, n - 1, axis=0) + b1_ref[0])
    y = jnp.dot(hid.astype(BF16), w2_ref[0], preferred_element_type=F32) + b2_ref[0]
    if is_key:
        o_ref[0] = _rms_rows(y, gain_ref[...]).astype(BF16)
    else:
        o_ref[0] = y.T.astype(BF16)


def _nsa_shared_kv(h, B, S, kv_norm, kv_w, kv_k_norm, phi_pe, phi_w1, phi_b1, phi_w2, phi_b2, tk):
    T, D = h.shape
    G, HD = NSA_KV_GROUPS, NSA_HEAD_DIM
    GW = G * HD
    w = kv_w.astype(BF16)
    sl = lambda six: w[:, six * GW:(six + 1) * GW]
    wk = jnp.concatenate([sl(0), sl(1), sl(2), sl(4)], axis=1)
    wvt = jnp.concatenate([sl(3), sl(5)], axis=1).T
    nsbp = -(-(S // SLC_BLOCK) // LANES) * LANES
    raw, ks, kw, vt = pl.pallas_call(
        functools.partial(_kv_proj_kernel, tiles_per_seq=S // tk),
        grid=(T // tk,),
        in_specs=[pl.BlockSpec((tk, D), lambda i: (i, 0)),
                  pl.BlockSpec((1, D), lambda i: (0, 0)),
                  pl.BlockSpec((D, 4 * GW), lambda i: (0, 0)),
                  pl.BlockSpec((2 * GW, D), lambda i: (0, 0)),
                  pl.BlockSpec((NSA_BRANCHES, HD), lambda i: (0, 0))],
        out_specs=[pl.BlockSpec((2 * G, tk, HD), lambda i: (0, i, 0)),
                   pl.BlockSpec((G, tk, HD + nsbp), lambda i: (0, i, 0)),
                   pl.BlockSpec((G, tk, HD), lambda i: (0, i, 0)),
                   pl.BlockSpec((2 * G, 1, HD, tk), lambda i: (0, i, 0, 0))],
        out_shape=[jax.ShapeDtypeStruct((2 * G, T, HD), F32),
                   jax.ShapeDtypeStruct((G, T, HD + nsbp), BF16),
                   jax.ShapeDtypeStruct((G, T, HD), BF16),
                   jax.ShapeDtypeStruct((2 * G, T // tk, HD, tk), BF16)],
        compiler_params=_cparams(("parallel",)),
        name="nsa_kv_proj",
    )(h, kv_norm.reshape(1, D), wk, wvt, kv_k_norm)

    ncb = S // CMP_STRIDE
    half_k = CMP_STRIDE * HD
    hidden = phi_w1.shape[-1]
    raw16 = raw.reshape(2 * G, T // CMP_STRIDE, half_k)
    pe = phi_pe.reshape(2, CMP_BLOCK // CMP_STRIDE, half_k)
    outs = []
    for kind in range(2):
        is_key = kind == 0
        o_block = (1, ncb, HD) if is_key else (1, HD, ncb)
        outs.append(pl.pallas_call(
            functools.partial(_compress_kernel, is_key=is_key),
            grid=(B, G),
            in_specs=[pl.BlockSpec((1, ncb, half_k), lambda b, g, kind=kind: (kind * G + g, b, 0)),
                      pl.BlockSpec((1, 2, half_k), lambda b, g, kind=kind: (kind, 0, 0)),
                      pl.BlockSpec((1, 2 * half_k, hidden), lambda b, g, kind=kind: (kind, 0, 0)),
                      pl.BlockSpec((1, 1, hidden), lambda b, g, kind=kind: (kind, 0, 0)),
                      pl.BlockSpec((1, hidden, HD), lambda b, g, kind=kind: (kind, 0, 0)),
                      pl.BlockSpec((1, 1, HD), lambda b, g, kind=kind: (kind, 0, 0)),
                      pl.BlockSpec((1, HD), lambda b, g: (0, 0))],
            out_specs=pl.BlockSpec(o_block, lambda b, g: (b * G + g, 0, 0)),
            out_shape=jax.ShapeDtypeStruct((B * G,) + o_block[1:], BF16),
            compiler_params=_cparams(("parallel", "parallel")),
            name="nsa_compress_k" if is_key else "nsa_compress_v",
        )(raw16, pe, phi_w1.astype(BF16), phi_b1.reshape(2, 1, hidden), phi_w2.astype(BF16),
          phi_b2.reshape(2, 1, HD), kv_k_norm[0:1]))
    return outs[0], outs[1], ks, kw, vt


def _nsa_in_kernel(h_ref, g_ref, wq_ref, wgt_ref, qg_ref, q_ref, gate_ref):
    HD = NSA_HEAD_DIM
    xn = _rms_rows(h_ref[...], g_ref[...]).astype(BF16)
    y = jnp.dot(xn, wq_ref[...], preferred_element_type=F32)
    qscale = HD ** -0.5 * LOG2E
    for hd in range(q_ref.shape[0]):
        q_ref[hd] = (_rms_rows(y[:, hd * HD:(hd + 1) * HD], qg_ref[...]) * qscale).astype(BF16)
    g_t = lax.dot_general(wgt_ref[...], xn, _NT, preferred_element_type=F32)
    gate_ref[...] = jax.nn.sigmoid(g_t)


def _nsa_attn_kernel(slopes_ref, q_ref, gate_ref, kc_ref, vct_ref, ks_ref, vst_ref, kw_ref, vwt_ref, ovt_ref,
                     o_ref, qat_ref, b_ref, s_ref, p_ref, m_ref, l_ref, acc_ref, oc_ref, os_ref, *,
                     tq, hpg):
    HD = NSA_HEAD_DIM
    tk = tq
    g = pl.program_id(1)
    qi = pl.program_id(2)
    t0 = qi * tq
    ncb = kc_ref.shape[1]
    nsb = ovt_ref.shape[0]
    nsbp = qat_ref.shape[0] - HD
    slopes = [slopes_ref[g * hpg + hh] for hh in range(hpg)]

    tpos_c = lax.broadcasted_iota(jnp.int32, (ncb, tq), 1) + t0
    cpos = lax.broadcasted_iota(jnp.int32, (ncb, tq), 0) * CMP_STRIDE + (CMP_BLOCK - 1)
    dist_c = tpos_c - cpos
    mask_c = dist_c >= 0
    dist_cf = dist_c.astype(F32)
    p_sum = jnp.zeros((ncb, tq), F32)
    for hh in range(hpg):
        s = lax.dot_general(kc_ref[0], q_ref[hh], _NT, preferred_element_type=F32) - slopes[hh] * dist_cf
        s = jnp.where(mask_c, s, MASK_FILL)
        e = jnp.where(mask_c, jnp.exp2(s - jnp.max(s, axis=0, keepdims=True)), 0.0)
        p = e / jnp.maximum(jnp.sum(e, axis=0, keepdims=True), TINY)
        oc_ref[hh] = jnp.dot(vct_ref[0], p.astype(BF16), preferred_element_type=F32)
        p_sum = p_sum + p

    imp = jnp.zeros((nsb, tq), F32)
    rem = p_sum
    for _ in range(3):
        part = rem.astype(BF16)
        imp = imp + jnp.dot(ovt_ref[...], part, preferred_element_type=F32)
        rem = rem - part.astype(F32)
    jb = lax.broadcasted_iota(jnp.int32, (nsb, tq), 0)
    cur = (lax.broadcasted_iota(jnp.int32, (nsb, tq), 1) + t0) // SLC_BLOCK
    forced = jnp.where(jb == 0, 1.0, jnp.where(jb == cur, 1.0, jnp.where(jb == cur - 1, 1.0, 0.0)))
    score = jnp.where(jb <= cur, imp + FORCE_BONUS * forced, -jnp.inf)
    blocks = [score[SUBLANES * r:SUBLANES * (r + 1), :] for r in range(nsb // SUBLANES)]
    pad = [jnp.full((SUBLANES, tq), -jnp.inf, F32)] * (N_SEL - len(blocks))
    kth = _top16_sorted(blocks + pad)[N_SEL - 1][0:1, :]
    above = [jnp.where(b > kth, 1.0, 0.0) for b in blocks]
    equal = [jnp.where(b == kth, 1.0, 0.0) for b in blocks]
    n_above = above[0]
    for a in above[1:]:
        n_above = n_above + a
    need = float(N_SEL) - jnp.sum(n_above, axis=0, keepdims=True)
    sub = lax.broadcasted_iota(jnp.int32, (SUBLANES, tq), 0)
    before = jnp.zeros((1, tq), F32)
    picked = []
    for a, e in zip(above, equal):
        run = e
        for sh in (1, 2, 4):
            run = run + jnp.where(sub >= sh, pltpu.roll(run, sh, axis=0), 0.0)
        first = jnp.where((run - e) + before < need, e, 0.0)
        picked.append(jnp.where(a + first > 0.5, 0.0, MASK_FILL))
        before = before + jnp.max(run, axis=0, keepdims=True)
    neg_sel = jnp.concatenate(picked, axis=0)

    bpt = tk // SLC_BLOCK
    tile_max = neg_sel
    for sh in range(1, bpt):
        tile_max = jnp.maximum(tile_max, pltpu.roll(neg_sel, nsb - sh, axis=0))
    hit = jnp.max(tile_max, axis=1, keepdims=True) > 0.5 * MASK_FILL
    row = lax.broadcasted_iota(jnp.int32, (nsb, 1), 0)
    bit = jnp.where(hit & (lax.rem(row, bpt) == 0),
                    lax.shift_left(jnp.int32(1), lax.rem(row // bpt, TILE_WORD_BITS)), 0).astype(F32)
    in_lo = row < TILE_WORD_BITS * bpt
    word_lo = jnp.sum(jnp.where(in_lo, bit, 0.0)).astype(jnp.int32)
    word_hi = jnp.sum(jnp.where(in_lo, 0.0, bit)).astype(jnp.int32)

    if nsbp > nsb:
        neg_sel = jnp.concatenate([neg_sel, jnp.zeros((nsbp - nsb, tq), F32)], axis=0)
    neg_sel = neg_sel.astype(BF16)

    wl = hpg * tq
    rc = BF16_ROW_GROUP
    kio_c = lax.broadcasted_iota(jnp.int32, (rc, wl), 0)
    tio_c = lax.rem(lax.broadcasted_iota(jnp.int32, (rc, wl), 1), tq)
    slope_row = jnp.concatenate([jnp.full((1, tq), slopes[hh], F32) for hh in range(hpg)], axis=1)
    for hh in range(hpg):
        qat_ref[0:HD, hh * tq:(hh + 1) * tq] = q_ref[hh].astype(F32).T.astype(BF16)
        qat_ref[HD:, hh * tq:(hh + 1) * tq] = neg_sel
    b_ref[...] = slope_row * lax.broadcasted_iota(jnp.int32, (tk, wl), 0).astype(F32)

    chunks = [slice(r * rc, (r + 1) * rc) for r in range(tk // rc)]

    def key_tile(k_ref, ki):
        return k_ref[0, pl.ds(pl.multiple_of(ki * tk, tk), tk), :]

    def bias_and_max(mask_kind):
        acc = None
        for r, ch in enumerate(chunks):
            t = s_ref[ch, :] + b_ref[ch, :]
            if mask_kind == "causal":
                t = t + jnp.where(kio_c + r * rc > tio_c, MASK_FILL, 0.0)
            elif mask_kind == "beyond":
                t = t + jnp.where(kio_c + r * rc <= tio_c, MASK_FILL, 0.0)
            s_ref[ch, :] = t
            acc = t if acc is None else jnp.maximum(acc, t)
        return jnp.max(acc, axis=0, keepdims=True)

    def exp_pass(shift):
        lsum = jnp.zeros((rc, wl), F32)
        for ch in chunks:
            p = jnp.exp2(s_ref[ch, :] + shift)
            p_ref[ch, :] = p.astype(BF16)
            lsum = lsum + p
        return jnp.sum(lsum, axis=0, keepdims=True)

    def first_tile(s, v_t):
        s_ref[...] = s
        m = bias_and_max("causal")
        m_ref[...] = m
        l_ref[...] = exp_pass(-m)
        acc_ref[...] = jnp.dot(v_t, p_ref[...], preferred_element_type=F32)

    def next_tile(s, back, v_t, mask_kind):
        s_ref[...] = s
        c = -slope_row * back
        m_old = m_ref[...]
        m_new = jnp.maximum(m_old, bias_and_max(mask_kind) + c)
        alpha = jnp.exp2(m_old - m_new)
        l_ref[...] = alpha * l_ref[...] + exp_pass(c - m_new)
        acc_ref[...] = alpha * acc_ref[...] + jnp.dot(v_t, p_ref[...], preferred_element_type=F32)
        m_ref[...] = m_new

    def sel_scores(ki):
        return jnp.dot(key_tile(ks_ref, ki), qat_ref[...], preferred_element_type=F32)

    def win_scores(ki):
        return jnp.dot(key_tile(kw_ref, ki), qat_ref[0:HD, :], preferred_element_type=F32)

    first_tile(sel_scores(qi), vst_ref[0, qi])

    def sel_body(step, carry):
        ki = qi - 1 - step
        word = jnp.where(ki < TILE_WORD_BITS, word_lo, word_hi)

        @pl.when((lax.shift_right_logical(word, lax.rem(ki, TILE_WORD_BITS)) & 1) != 0)
        def _():
            next_tile(sel_scores(ki), ((step + 1) * tk).astype(F32), vst_ref[0, ki], None)
        return carry

    lax.fori_loop(0, qi, sel_body, 0)
    os_ref[...] = acc_ref[...] / jnp.maximum(l_ref[...], TINY)

    first_tile(win_scores(qi), vwt_ref[0, qi])
    for back_tiles in range(1, WINDOW // tk + 1):
        @pl.when(qi >= back_tiles)
        def _(back_tiles=back_tiles):
            ki = qi - back_tiles
            next_tile(win_scores(ki), float(back_tiles * tk), vwt_ref[0, ki],
                      "beyond" if back_tiles == WINDOW // tk else None)
    o_w = acc_ref[...] / jnp.maximum(l_ref[...], TINY)

    for hh in range(hpg):
        cols = slice(hh * tq, (hh + 1) * tq)
        gates = [gate_ref[0, hh * NSA_BRANCHES + br:hh * NSA_BRANCHES + br + 1, :] for br in range(NSA_BRANCHES)]
        o_t = gates[0] * oc_ref[hh] + gates[1] * os_ref[:, cols] + gates[2] * o_w[:, cols]
        o_ref[:, hh * HD:(hh + 1) * HD] = o_t.T.astype(BF16)


def _mm_res_kernel(a_ref, w_ref, h_ref, o_ref):
    o_ref[...] = jnp.dot(a_ref[...], w_ref[...], preferred_element_type=F32) + h_ref[...]


def _nsa_layer(h, B, S, norm_g, w_in, q_norm, w_out, k_c, v_ct, ks, kw, vt, tq):
    T, D = h.shape
    G, HD = NSA_KV_GROUPS, NSA_HEAD_DIM
    width = w_out.shape[0]
    heads = width // HD
    hpg = heads // G
    n_gate = heads * NSA_BRANCHES
    w = w_in.astype(BF16)
    tm = TILES.tokens
    q, gate_t = pl.pallas_call(
        _nsa_in_kernel,
        grid=(T // tm,),
        in_specs=[pl.BlockSpec((tm, D), lambda i: (i, 0)),
                  pl.BlockSpec((1, D), lambda i: (0, 0)),
                  pl.BlockSpec((D, width), lambda i: (0, 0)),
                  pl.BlockSpec((n_gate, D), lambda i: (0, 0)),
                  pl.BlockSpec((1, HD), lambda i: (0, 0))],
        out_specs=[pl.BlockSpec((heads, tm, HD), lambda i: (0, i, 0)),
                   pl.BlockSpec((n_gate, tm), lambda i: (0, i))],
        out_shape=[jax.ShapeDtypeStruct((heads, T, HD), BF16), jax.ShapeDtypeStruct((n_gate, T), F32)],
        compiler_params=_cparams(("parallel",)),
        name="nsa_in",
    )(h, norm_g.reshape(1, D), w[:, :width], w[:, width:].T, q_norm.reshape(1, HD))

    ncb = S // CMP_STRIDE
    nsb = S // SLC_BLOCK
    n_cmp = (S - CMP_BLOCK) // CMP_STRIDE + 1
    c_start = jnp.arange(ncb) * CMP_STRIDE
    c_pos = c_start + CMP_BLOCK - 1
    s_start = jnp.arange(nsb) * SLC_BLOCK
    ov_t = ((c_start[None, :] < s_start[:, None] + SLC_BLOCK) & (c_pos[None, :] >= s_start[:, None])
            & (jnp.arange(ncb)[None, :] < n_cmp)).astype(BF16)
    h_idx = jnp.arange(1, heads + 1, dtype=F32)
    slopes = 2.0 ** (-8.0 * h_idx / heads) * LOG2E
    nq = S // tq
    assert nq <= 2 * TILE_WORD_BITS, "the key-tile flags of a query tile are kept in two words"
    w_sel = ks.shape[2]
    o = pl.pallas_call(
        functools.partial(_nsa_attn_kernel, tq=tq, hpg=hpg),
        grid_spec=pltpu.PrefetchScalarGridSpec(
            num_scalar_prefetch=1,
            grid=(B, G, nq),
            in_specs=[pl.BlockSpec((hpg, tq, HD), lambda b, g, qi, sl: (g, b * nq + qi, 0)),
                      pl.BlockSpec((1, hpg * NSA_BRANCHES, tq), lambda b, g, qi, sl: (g, 0, b * nq + qi)),
                      pl.BlockSpec((1, ncb, HD), lambda b, g, qi, sl: (b * G + g, 0, 0)),
                      pl.BlockSpec((1, HD, ncb), lambda b, g, qi, sl: (b * G + g, 0, 0)),
                      pl.BlockSpec((1, S, w_sel), lambda b, g, qi, sl: (g, b, 0)),
                      pl.BlockSpec((1, nq, HD, tq), lambda b, g, qi, sl: (g, b, 0, 0)),
                      pl.BlockSpec((1, S, HD), lambda b, g, qi, sl: (g, b, 0)),
                      pl.BlockSpec((1, nq, HD, tq), lambda b, g, qi, sl: (G + g, b, 0, 0)),
                      pl.BlockSpec((nsb, ncb), lambda b, g, qi, sl: (0, 0))],
            out_specs=pl.BlockSpec((tq, hpg * HD), lambda b, g, qi, sl: (b * nq + qi, g)),
            scratch_shapes=[pltpu.VMEM((w_sel, hpg * tq), BF16),
                            pltpu.VMEM((tq, hpg * tq), F32),
                            pltpu.VMEM((tq, hpg * tq), F32),
                            pltpu.VMEM((tq, hpg * tq), BF16),
                            pltpu.VMEM((1, hpg * tq), F32),
                            pltpu.VMEM((1, hpg * tq), F32),
                            pltpu.VMEM((HD, hpg * tq), F32),
                            pltpu.VMEM((hpg, HD, tq), F32),
                            pltpu.VMEM((HD, hpg * tq), F32)]),
        out_shape=jax.ShapeDtypeStruct((T, width), BF16),
        compiler_params=_cparams(("parallel", "parallel", "arbitrary")),
        name="nsa_attn",
    )(slopes, q, gate_t.reshape(G, hpg * NSA_BRANCHES, T), k_c, v_ct, ks, vt, kw, vt, ov_t)

    return pl.pallas_call(
        _mm_res_kernel,
        grid=(T // tm,),
        in_specs=[pl.BlockSpec((tm, width), lambda i: (i, 0)),
                  pl.BlockSpec((width, D), lambda i: (0, 0)),
                  pl.BlockSpec((tm, D), lambda i: (i, 0))],
        out_specs=pl.BlockSpec((tm, D), lambda i: (i, 0)),
        out_shape=jax.ShapeDtypeStruct((T, D), F32),
        compiler_params=_cparams(("parallel",)),
        name="nsa_out",
    )(o, w_out.astype(BF16), h)


def kernel(x, a_norm, a_w_in, a_v_norm, a_w_s, a_b_s, a_w_out, kv_norm, kv_w, kv_k_norm, phi_pe, phi_w1, phi_b1, phi_w2, phi_b2, b_norm, b_w_in, b_q_norm, b_w_out, ffn_norm, peer_w_q, peer_keys, peer_u, peer_v):
    B, S, D = x.shape
    depth = ffn_norm.shape[0]
    n_a = a_norm.shape[0]
    tq = TILES.nsa
    h = x.reshape(B * S, D)
    shared = None
    for layer in range(depth):
        if layer < n_a:
            a = layer
            h = _gmlp_layer(h, a_norm[a], a_w_in[a], a_v_norm[a], a_w_s[a], a_b_s[a], a_w_out[a])
        else:
            if shared is None:
                shared = _nsa_shared_kv(h, B, S, kv_norm, kv_w, kv_k_norm, phi_pe, phi_w1, phi_b1, phi_w2,
                                        phi_b2, tq)
            b = layer - n_a
            h = _nsa_layer(h, B, S, b_norm[b], b_w_in[b], b_q_norm[b], b_w_out[b], *shared, tq)
        h = _peer_layer(h, ffn_norm[layer], peer_w_q[layer], peer_keys[layer], peer_u[layer], peer_v[layer])
    return h.reshape(B, S, D)
```

```python
import functools
import math
from typing import NamedTuple

import jax
import jax.numpy as jnp
from jax import lax
from jax.experimental import pallas as pl
from jax.experimental.pallas import tpu as pltpu

F32 = jnp.float32
BF16 = jnp.bfloat16

GMLP_CHUNK = 128
GMLP_GROUPS = 8
NSA_HEAD_DIM = 128
NSA_KV_GROUPS = 4
NSA_BRANCHES = 3
CMP_BLOCK = 32
CMP_STRIDE = 16
SLC_BLOCK = 64
N_SEL = 16
WINDOW = 512
FORCE_BONUS = 1000.0
PEER_HEADS = 8
PEER_NKEYS = 128
PEER_TOPK = 16
PEER_KEY_DIM = 128
RMS_EPS = 1e-6
MASK_FILL = -1e30
TINY = float(jnp.finfo(jnp.float32).tiny)

V7X_VMEM_LIMIT_BYTES = 56 * 1024 * 1024
LANES = 128
SUBLANES = 8
LOG2E = 1.4426950408889634
TILE_WORD_BITS = 16
BF16_ROW_GROUP = 2 * SUBLANES


class _Tiles(NamedTuple):
    tokens: int = 512
    gmlp_in_tokens: int = 1024
    gmlp_in_cols: int = 2048
    gmlp_out_cols: int = 512
    peer_score_tokens: int = 256
    peer_experts: int = 512
    peer_tiles_per_step: int = 2
    peer_piece_tokens: int = 256
    nsa: int = 256


TILES = _Tiles()

_NT = (((1,), (1,)), ((), ()))


def _cparams(sem):
    return pltpu.CompilerParams(dimension_semantics=sem, vmem_limit_bytes=V7X_VMEM_LIMIT_BYTES)


GELU_C0 = math.sqrt(2.0 / math.pi)
GELU_C1 = 0.044715 * GELU_C0


def _gelu(x):
    return x * (0.5 * (1.0 + jnp.tanh(GELU_C0 * (x + 0.044715 * (x * x * x)))))


def _rms_rows(x, gain):
    r = lax.rsqrt(jnp.mean(x * x, axis=-1, keepdims=True) + RMS_EPS)
    return (x * r) * gain


def _gmlp_in_kernel(h_ref, g_ref, w_ref, z_ref, ss_ref, xn_ref, *, n_u_tiles):
    j = pl.program_id(1)

    @pl.when(j == 0)
    def _():
        xn_ref[...] = _rms_rows(h_ref[...], g_ref[...]).astype(BF16)
        ss_ref[...] = jnp.zeros_like(ss_ref)

    z = _gelu(jnp.dot(xn_ref[...], w_ref[...], preferred_element_type=F32))
    z_ref[...] = z.astype(BF16)

    @pl.when(j >= n_u_tiles)
    def _():
        ss_ref[...] += jnp.sum(z * z, axis=-1, keepdims=True)


def _gmlp_out_kernel(u_ref, v_ref, ss_ref, vg_ref, ws_ref, bs_ref, wo_ref, h_ref, o_ref, prod_ref, *,
                     n_chunks, half):
    j = pl.program_id(1)
    gdim = half // GMLP_GROUPS

    @pl.when(j == 0)
    def _():
        row = lax.broadcasted_iota(jnp.int32, (GMLP_CHUNK, GMLP_CHUNK), 0)
        col = lax.broadcasted_iota(jnp.int32, (GMLP_CHUNK, GMLP_CHUNK), 1)
        tril = row >= col
        for c in range(n_chunks):
            rows = slice(c * GMLP_CHUNK, (c + 1) * GMLP_CHUNK)
            rinv = lax.rsqrt(ss_ref[c] * (1.0 / half) + RMS_EPS)
            for g in range(GMLP_GROUPS):
                cols = slice(g * gdim, (g + 1) * gdim)
                wc = (jnp.where(tril, ws_ref[g], 0.0) * rinv).astype(BF16)
                sv = jnp.dot(wc, v_ref[rows, cols], preferred_element_type=F32)
                sv = sv * vg_ref[:, cols] + bs_ref[:, g:g + 1]
                prod_ref[rows, cols] = (u_ref[rows, cols].astype(F32) * sv).astype(BF16)

    o_ref[...] = jnp.dot(prod_ref[...], wo_ref[...], preferred_element_type=F32) + h_ref[...]


def _gmlp_layer(h, norm_g, w_in, v_norm, w_s, b_s, w_out):
    T, D = h.shape
    half = w_out.shape[0]
    tm, tm_in, tn = TILES.tokens, TILES.gmlp_in_tokens, TILES.gmlp_in_cols
    n_u = half // tn
    z, ss = pl.pallas_call(
        functools.partial(_gmlp_in_kernel, n_u_tiles=n_u),
        grid=(T // tm_in, 2 * half // tn),
        in_specs=[pl.BlockSpec((tm_in, D), lambda i, j: (i, 0)),
                  pl.BlockSpec((1, D), lambda i, j: (0, 0)),
                  pl.BlockSpec((D, tn), lambda i, j: (0, j))],
        out_specs=[pl.BlockSpec((tm_in, tn), lambda i, j: (i, j)),
                   pl.BlockSpec((tm_in, 1), lambda i, j: (i, 0))],
        out_shape=[jax.ShapeDtypeStruct((T, 2 * half), BF16), jax.ShapeDtypeStruct((T, 1), F32)],
        scratch_shapes=[pltpu.VMEM((tm_in, D), BF16)],
        compiler_params=_cparams(("parallel", "arbitrary")),
        name="gmlp_in",
    )(h, norm_g.reshape(1, D), w_in.astype(BF16))

    tn2 = TILES.gmlp_out_cols
    n_chunks = tm // GMLP_CHUNK
    return pl.pallas_call(
        functools.partial(_gmlp_out_kernel, n_chunks=n_chunks, half=half),
        grid=(T // tm, D // tn2),
        in_specs=[pl.BlockSpec((tm, half), lambda i, j: (i, 0)),
                  pl.BlockSpec((tm, half), lambda i, j: (i, 1)),
                  pl.BlockSpec((n_chunks, 1, GMLP_CHUNK), lambda i, j: (i, 0, 0)),
                  pl.BlockSpec((1, half), lambda i, j: (0, 0)),
                  pl.BlockSpec((GMLP_GROUPS, GMLP_CHUNK, GMLP_CHUNK), lambda i, j: (0, 0, 0)),
                  pl.BlockSpec((GMLP_CHUNK, GMLP_GROUPS), lambda i, j: (0, 0)),
                  pl.BlockSpec((half, tn2), lambda i, j: (0, j)),
                  pl.BlockSpec((tm, tn2), lambda i, j: (i, j))],
        out_specs=pl.BlockSpec((tm, tn2), lambda i, j: (i, j)),
        out_shape=jax.ShapeDtypeStruct((T, D), F32),
        scratch_shapes=[pltpu.VMEM((tm, half), BF16)],
        compiler_params=_cparams(("parallel", "arbitrary")),
        name="gmlp_out",
    )(z, z, ss.reshape(T // GMLP_CHUNK, 1, GMLP_CHUNK), v_norm.reshape(1, half), w_s, b_s.T,
      w_out.astype(BF16), h)


def _cmpx(v, i, l, desc):
    hi, lo = jnp.maximum(v[i], v[l]), jnp.minimum(v[i], v[l])
    v[i], v[l] = (hi, lo) if desc else (lo, hi)


def _bitonic_sort_desc(v):
    n = len(v)
    k = 2
    while k <= n:
        j = k // 2
        while j >= 1:
            for i in range(n):
                l = i ^ j
                if l > i:
                    _cmpx(v, i, l, (i & k) == 0)
            j //= 2
        k *= 2


def _bitonic_merge_desc(v):
    n = len(v)
    j = n // 2
    while j >= 1:
        for i in range(n):
            l = i ^ j
            if l > i:
                _cmpx(v, i, l, True)
        j //= 2


def _top16_sorted(rows):
    v = list(rows)
    _bitonic_sort_desc(v)
    for shift in (4, 2, 1):
        w = [pltpu.roll(x, shift, axis=0) for x in v]
        v = [jnp.maximum(v[i], w[PEER_TOPK - 1 - i]) for i in range(PEER_TOPK)]
        _bitonic_merge_desc(v)
    return v


def _peer_score_kernel(h_ref, g_ref, wq_ref, keys_ref, xn_ref, s_ref, st_ref, t2_ref):
    tm = h_ref.shape[0]
    xn = _rms_rows(h_ref[...], g_ref[...]).astype(BF16)
    xn_ref[...] = xn
    q = jnp.dot(xn, wq_ref[...], preferred_element_type=F32).astype(BF16)
    sub = lax.broadcasted_iota(jnp.int32, (SUBLANES, tm), 0)

    def pack(lst):
        out = lst[0]
        for s in range(1, SUBLANES):
            out = jnp.where(sub == s, lst[s], out)
        return out

    for h in range(PEER_HEADS):
        tops = []
        for p in range(2):
            hp = 2 * h + p
            qs = q[:, hp * PEER_KEY_DIM:(hp + 1) * PEER_KEY_DIM]
            s_t = lax.dot_general(keys_ref[hp], qs, _NT, preferred_element_type=F32)
            s_ref[hp] = s_t
            tops.append(_top16_sorted([s_t[SUBLANES * r:SUBLANES * (r + 1), :] for r in range(16)]))
        t1, t2 = tops
        t2lo, t2hi, t1hi = pack(t2[0:8]), pack(t2[8:16]), pack(t1[8:16])
        cands = [t1[0] + t2lo, t1[0] + t2hi] + [t1[a] + t2lo for a in range(1, 8)] + [t1hi + t2[0]]
        cands += [jnp.full((SUBLANES, tm), -jnp.inf, F32)] * (16 - len(cands))
        top = _top16_sorted(cands)
        z = jnp.ones((SUBLANES, tm), F32)
        for k in range(1, PEER_TOPK):
            z = z + jnp.exp(top[k] - top[0])
        for k, val in enumerate((top[PEER_TOPK - 1], t1[0], t2[0], z)):
            st_ref[pl.ds(k * PEER_HEADS + h, 1), :] = val[0:1, :]
        t2_ref[h, 0:SUBLANES, :] = t2lo
        t2_ref[h, SUBLANES:PEER_TOPK, :] = t2hi


def _peer_main_kernel(xnt_ref, u_ref, vt_ref, s_ref, st_ref, t2_ref, h_ref, o_ref, acc_ref, e1_ref, cnt_ref,
                      e2_ref, code_ref, hid_ref, w_ref, *, rows_per_tile):
    jj = pl.program_id(1)
    tm = xnt_ref.shape[1]
    te = hid_ref.shape[1]
    rc = BF16_ROW_GROUP

    @pl.when(jj == 0)
    def _():
        acc_ref[...] = jnp.zeros_like(acc_ref)
        for h in range(PEER_HEADS):
            thr = st_ref[h:h + 1, :]
            m1 = st_ref[PEER_HEADS + h:PEER_HEADS + h + 1, :]
            m2 = st_ref[2 * PEER_HEADS + h:2 * PEER_HEADS + h + 1, :]
            zz = st_ref[3 * PEER_HEADS + h:3 * PEER_HEADS + h + 1, :]
            s1 = s_ref[2 * h]
            s2 = s_ref[2 * h + 1]
            e1_ref[h] = 0.5 * jnp.exp(s1 - m1)
            e2_ref[h] = (jnp.exp(s2 - m2) / zz).astype(BF16)
            cnt = jnp.zeros(s1.shape, F32)
            code = jnp.zeros(s1.shape, F32)
            for b in range(PEER_TOPK):
                t2b = t2_ref[h, b:b + 1, :]
                cnt = jnp.where(s1 + t2b >= thr, float(b + 1), cnt)
                code = jnp.where(t2b > s2, float(b + 1), code)
            cnt_ref[h] = cnt
            code_ref[h] = code.astype(BF16)

    tn = TILES.peer_piece_tokens
    n_pieces = tm // tn

    def project_pieces(slot):
        def piece(n):
            hid_ref[slot, :, n * tn:(n + 1) * tn] = jnp.dot(
                u_ref[slot * te:(slot + 1) * te, :], xnt_ref[:, n * tn:(n + 1) * tn],
                preferred_element_type=F32)
        return [functools.partial(piece, n) for n in range(n_pieces)]

    def gate_chunks(slot):
        tile = hid_ref.shape[0] * jj + slot
        out = []
        for r in range(rows_per_tile):
            i = tile * rows_per_tile + r
            cnts = [cnt_ref[h, pl.ds(i, 1), :].astype(BF16) for h in range(PEER_HEADS)]
            e1s = [e1_ref[h, pl.ds(i, 1), :].astype(BF16) for h in range(PEER_HEADS)]
            for c in range(PEER_NKEYS // rc):
                def chunk(r=r, c=c, cnts=cnts, e1s=e1s):
                    rows = slice(c * rc, (c + 1) * rc)
                    gate = jnp.zeros((rc, tm), BF16)
                    for h in range(PEER_HEADS):
                        gate = gate + jnp.where(code_ref[h, rows, :] < cnts[h], e2_ref[h, rows, :],
                                                jnp.zeros((), BF16)) * e1s[h]
                    out_rows = slice(r * PEER_NKEYS + c * rc, r * PEER_NKEYS + (c + 1) * rc)
                    x = hid_ref[slot, out_rows, :]
                    xg = x * gate.astype(F32)
                    inner = x * (GELU_C0 + GELU_C1 * (x * x))
                    w_ref[slot, out_rows, :] = (xg + xg * jnp.tanh(inner)).astype(BF16)
                out.append(chunk)
        return out

    def accumulate_pieces(slot):
        def piece(n):
            acc_ref[:, n * tn:(n + 1) * tn] += jnp.dot(vt_ref[:, slot * te:(slot + 1) * te],
                                                       w_ref[slot, :, n * tn:(n + 1) * tn],
                                                       preferred_element_type=F32)
        return [functools.partial(piece, n) for n in range(n_pieces)]

    def emit_interleaved(many, few):
        stride = max(len(many) // max(len(few), 1), 1)
        for k, f in enumerate(many):
            if k % stride == 0 and k // stride < len(few):
                few[k // stride]()
            f()

    n_slots = hid_ref.shape[0]
    for f in project_pieces(0):
        f()
    for k in range(n_slots):
        matmuls = accumulate_pieces(k - 1) if k >= 1 else []
        if k + 1 < n_slots:
            matmuls = matmuls + project_pieces(k + 1)
        emit_interleaved(gate_chunks(k), matmuls)
    for f in accumulate_pieces(n_slots - 1):
        f()

    @pl.when(jj == pl.num_programs(1) - 1)
    def _():
        o_ref[...] = acc_ref[...].T + h_ref[...]


def _peer_layer(h, norm_g, w_q, keys, u_tab, v_tab):
    T, D = h.shape
    n_exp = u_tab.shape[0]
    n_hp = 2 * PEER_HEADS
    tm1 = TILES.peer_score_tokens
    xn, s_t, stats, t2 = pl.pallas_call(
        _peer_score_kernel,
        grid=(T // tm1,),
        in_specs=[pl.BlockSpec((tm1, D), lambda i: (i, 0)),
                  pl.BlockSpec((1, D), lambda i: (0, 0)),
                  pl.BlockSpec((D, n_hp * PEER_KEY_DIM), lambda i: (0, 0)),
                  pl.BlockSpec((n_hp, PEER_NKEYS, PEER_KEY_DIM), lambda i: (0, 0, 0))],
        out_specs=[pl.BlockSpec((tm1, D), lambda i: (i, 0)),
                   pl.BlockSpec((n_hp, PEER_NKEYS, tm1), lambda i: (0, 0, i)),
                   pl.BlockSpec((4 * PEER_HEADS, tm1), lambda i: (0, i)),
                   pl.BlockSpec((PEER_HEADS, PEER_TOPK, tm1), lambda i: (0, 0, i))],
        out_shape=[jax.ShapeDtypeStruct((T, D), BF16),
                   jax.ShapeDtypeStruct((n_hp, PEER_NKEYS, T), F32),
                   jax.ShapeDtypeStruct((4 * PEER_HEADS, T), F32),
                   jax.ShapeDtypeStruct((PEER_HEADS, PEER_TOPK, T), F32)],
        compiler_params=_cparams(("parallel",)),
        name="peer_score",
    )(h, norm_g.reshape(1, D), w_q.astype(BF16),
      keys.reshape(n_hp, PEER_NKEYS, PEER_KEY_DIM).astype(BF16))

    tm, te, tps = TILES.tokens, TILES.peer_experts, TILES.peer_tiles_per_step
    n_tiles = n_exp // te
    n_pairs = n_tiles // tps
    once = pl.Buffered(1)
    return pl.pallas_call(
        functools.partial(_peer_main_kernel, rows_per_tile=te // PEER_NKEYS),
        grid=(T // tm, n_pairs),
        in_specs=[pl.BlockSpec((D, tm), lambda i, j: (0, i), pipeline_mode=once),
                  pl.BlockSpec((tps * te, D), lambda i, j: (j, 0)),
                  pl.BlockSpec((D, tps * te), lambda i, j: (0, j)),
                  pl.BlockSpec((n_hp, PEER_NKEYS, tm), lambda i, j: (0, 0, i), pipeline_mode=once),
                  pl.BlockSpec((4 * PEER_HEADS, tm), lambda i, j: (0, i), pipeline_mode=once),
                  pl.BlockSpec((PEER_HEADS, PEER_TOPK, tm), lambda i, j: (0, 0, i), pipeline_mode=once),
                  pl.BlockSpec((tm, D), lambda i, j: (i, 0), pipeline_mode=once)],
        out_specs=pl.BlockSpec((tm, D), lambda i, j: (i, 0)),
        out_shape=jax.ShapeDtypeStruct((T, D), F32),
        scratch_shapes=[pltpu.VMEM((D, tm), F32),
                        pltpu.VMEM((PEER_HEADS, PEER_NKEYS, tm), F32),
                        pltpu.VMEM((PEER_HEADS, PEER_NKEYS, tm), F32),
                        pltpu.VMEM((PEER_HEADS, PEER_NKEYS, tm), BF16),
                        pltpu.VMEM((PEER_HEADS, PEER_NKEYS, tm), BF16),
                        pltpu.VMEM((tps, te, tm), F32),
                        pltpu.VMEM((tps, te, tm), BF16)],
        compiler_params=_cparams(("parallel", "arbitrary")),
        name="peer_main",
    )(xn.T, u_tab.astype(BF16), v_tab.T.astype(BF16), s_t, stats, t2, h)


def _kv_proj_kernel(h_ref, g_ref, w_ref, wvt_ref, kg_ref, raw_ref, ks_ref, kw_ref, vt_ref, *, tiles_per_seq):
    G, HD = NSA_KV_GROUPS, NSA_HEAD_DIM
    tk = h_ref.shape[0]
    xn = _rms_rows(h_ref[...], g_ref[...]).astype(BF16)
    y = jnp.dot(xn, w_ref[...], preferred_element_type=F32)
    col = lambda six, c: y[:, (six * G + c) * HD:(six * G + c + 1) * HD]
    nsbp = ks_ref.shape[2] - HD
    blk = lax.rem(pl.program_id(0), tiles_per_seq) * (tk // SLC_BLOCK) \
        + lax.broadcasted_iota(jnp.int32, (tk, nsbp), 0) // SLC_BLOCK
    onehot = jnp.where(lax.broadcasted_iota(jnp.int32, (tk, nsbp), 1) == blk, 1.0, 0.0).astype(BF16)
    for c in range(G):
        raw_ref[c] = col(0, c)
        raw_ref[G + c] = col(1, c)
        ks_ref[c, :, 0:HD] = _rms_rows(col(2, c), kg_ref[1:2, :]).astype(BF16)
        ks_ref[c, :, HD:] = onehot
        kw_ref[c] = _rms_rows(col(3, c), kg_ref[2:3, :]).astype(BF16)
    y_t = lax.dot_general(wvt_ref[...], xn, _NT, preferred_element_type=F32)
    for c in range(2 * G):
        vt_ref[c, 0] = y_t[c * HD:(c + 1) * HD, :].astype(BF16)


def _compress_kernel(x_ref, pe_ref, w1_ref, b1_ref, w2_ref, b2_ref, gain_ref, o_ref, *, is_key):
    x = x_ref[0]
    n, half_k = x.shape
    xa = (x + pe_ref[0, 0:1, :]).astype(BF16)
    xb = (x + pe_ref[0, 1:2, :]).astype(BF16)
    a = jnp.dot(xa, w1_ref[0, :half_k, :], preferred_element_type=F32)
    b = jnp.dot(xb, w1_ref[0, half_k:, :], preferred_element_type=F32)
    hid = _gelu(a + pltpu.roll(b, n - 1, axis=0) + b1_ref[0])
    y = jnp.dot(hid.astype(BF16), w2_ref[0], preferred_element_type=F32) + b2_ref[0]
    if is_key:
        o_ref[0] = _rms_rows(y, gain_ref[...]).astype(BF16)
    else:
        o_ref[0] = y.T.astype(BF16)


def _nsa_shared_kv(h, B, S, kv_norm, kv_w, kv_k_norm, phi_pe, phi_w1, phi_b1, phi_w2, phi_b2, tk):
    T, D = h.shape
    G, HD = NSA_KV_GROUPS, NSA_HEAD_DIM
    GW = G * HD
    w = kv_w.astype(BF16)
    sl = lambda six: w[:, six * GW:(six + 1) * GW]
    wk = jnp.concatenate([sl(0), sl(1), sl(2), sl(4)], axis=1)
    wvt = jnp.concatenate([sl(3), sl(5)], axis=1).T
    nsbp = -(-(S // SLC_BLOCK) // LANES) * LANES
    raw, ks, kw, vt = pl.pallas_call(
        functools.partial(_kv_proj_kernel, tiles_per_seq=S // tk),
        grid=(T // tk,),
        in_specs=[pl.BlockSpec((tk, D), lambda i: (i, 0)),
                  pl.BlockSpec((1, D), lambda i: (0, 0)),
                  pl.BlockSpec((D, 4 * GW), lambda i: (0, 0)),
                  pl.BlockSpec((2 * GW, D), lambda i: (0, 0)),
                  pl.BlockSpec((NSA_BRANCHES, HD), lambda i: (0, 0))],
        out_specs=[pl.BlockSpec((2 * G, tk, HD), lambda i: (0, i, 0)),
                   pl.BlockSpec((G, tk, HD + nsbp), lambda i: (0, i, 0)),
                   pl.BlockSpec((G, tk, HD), lambda i: (0, i, 0)),
                   pl.BlockSpec((2 * G, 1, HD, tk), lambda i: (0, i, 0, 0))],
        out_shape=[jax.ShapeDtypeStruct((2 * G, T, HD), F32),
                   jax.ShapeDtypeStruct((G, T, HD + nsbp), BF16),
                   jax.ShapeDtypeStruct((G, T, HD), BF16),
                   jax.ShapeDtypeStruct((2 * G, T // tk, HD, tk), BF16)],
        compiler_params=_cparams(("parallel",)),
        name="nsa_kv_proj",
    )(h, kv_norm.reshape(1, D), wk, wvt, kv_k_norm)

    ncb = S // CMP_STRIDE
    half_k = CMP_STRIDE * HD
    hidden = phi_w1.shape[-1]
    raw16 = raw.reshape(2 * G, T // CMP_STRIDE, half_k)
    pe = phi_pe.reshape(2, CMP_BLOCK // CMP_STRIDE, half_k)
    outs = []
    for kind in range(2):
        is_key = kind == 0
        o_block = (1, ncb, HD) if is_key else (1, HD, ncb)
        outs.append(pl.pallas_call(
            functools.partial(_compress_kernel, is_key=is_key),
            grid=(B, G),
            in_specs=[pl.BlockSpec((1, ncb, half_k), lambda b, g, kind=kind: (kind * G + g, b, 0)),
                      pl.BlockSpec((1, 2, half_k), lambda b, g, kind=kind: (kind, 0, 0)),
                      pl.BlockSpec((1, 2 * half_k, hidden), lambda b, g, kind=kind: (kind, 0, 0)),
                      pl.BlockSpec((1, 1, hidden), lambda b, g, kind=kind: (kind, 0, 0)),
                      pl.BlockSpec((1, hidden, HD), lambda b, g, kind=kind: (kind, 0, 0)),
                      pl.BlockSpec((1, 1, HD), lambda b, g, kind=kind: (kind, 0, 0)),
                      pl.BlockSpec((1, HD), lambda b, g: (0, 0))],
            out_specs=pl.BlockSpec(o_block, lambda b, g: (b * G + g, 0, 0)),
            out_shape=jax.ShapeDtypeStruct((B * G,) + o_block[1:], BF16),
            compiler_params=_cparams(("parallel", "parallel")),
            name="nsa_compress_k" if is_key else "nsa_compress_v",
        )(raw16, pe, phi_w1.astype(BF16), phi_b1.reshape(2, 1, hidden), phi_w2.astype(BF16),
          phi_b2.reshape(2, 1, HD), kv_k_norm[0:1]))
    return outs[0], outs[1], ks, kw, vt


def _nsa_in_kernel(h_ref, g_ref, wq_ref, wgt_ref, qg_ref, q_ref, gate_ref):
    HD = NSA_HEAD_DIM
    xn = _rms_rows(h_ref[...], g_ref[...]).astype(BF16)
    y = jnp.dot(xn, wq_ref[...], preferred_element_type=F32)
    qscale = HD ** -0.5 * LOG2E
    for hd in range(q_ref.shape[0]):
        q_ref[hd] = (_rms_rows(y[:, hd * HD:(hd + 1) * HD], qg_ref[...]) * qscale).astype(BF16)
    g_t = lax.dot_general(wgt_ref[...], xn, _NT, preferred_element_type=F32)
    gate_ref[...] = jax.nn.sigmoid(g_t)


def _nsa_attn_kernel(slopes_ref, q_ref, gate_ref, kc_ref, vct_ref, ks_ref, vst_ref, kw_ref, vwt_ref, ovt_ref,
                     o_ref, qat_ref, b_ref, s_ref, p_ref, m_ref, l_ref, acc_ref, oc_ref, os_ref, *,
                     tq, hpg):
    HD = NSA_HEAD_DIM
    tk = tq
    g = pl.program_id(1)
    qi = pl.program_id(2)
    t0 = qi * tq
    ncb = kc_ref.shape[1]
    nsb = ovt_ref.shape[0]
    nsbp = qat_ref.shape[0] - HD
    slopes = [slopes_ref[g * hpg + hh] for hh in range(hpg)]

    tpos_c = lax.broadcasted_iota(jnp.int32, (ncb, tq), 1) + t0
    cpos = lax.broadcasted_iota(jnp.int32, (ncb, tq), 0) * CMP_STRIDE + (CMP_BLOCK - 1)
    dist_c = tpos_c - cpos
    mask_c = dist_c >= 0
    dist_cf = dist_c.astype(F32)
    p_sum = jnp.zeros((ncb, tq), F32)
    for hh in range(hpg):
        s = lax.dot_general(kc_ref[0], q_ref[hh], _NT, preferred_element_type=F32) - slopes[hh] * dist_cf
        s = jnp.where(mask_c, s, MASK_FILL)
        e = jnp.where(mask_c, jnp.exp2(s - jnp.max(s, axis=0, keepdims=True)), 0.0)
        p = e / jnp.maximum(jnp.sum(e, axis=0, keepdims=True), TINY)
        oc_ref[hh] = jnp.dot(vct_ref[0], p.astype(BF16), preferred_element_type=F32)
        p_sum = p_sum + p

    imp = jnp.zeros((nsb, tq), F32)
    rem = p_sum
    for _ in range(3):
        part = rem.astype(BF16)
        imp = imp + jnp.dot(ovt_ref[...], part, preferred_element_type=F32)
        rem = rem - part.astype(F32)
    jb = lax.broadcasted_iota(jnp.int32, (nsb, tq), 0)
    cur = (lax.broadcasted_iota(jnp.int32, (nsb, tq), 1) + t0) // SLC_BLOCK
    forced = jnp.where(jb == 0, 1.0, jnp.where(jb == cur, 1.0, jnp.where(jb == cur - 1, 1.0, 0.0)))
    score = jnp.where(jb <= cur, imp + FORCE_BONUS * forced, -jnp.inf)
    blocks = [score[SUBLANES * r:SUBLANES * (r + 1), :] for r in range(nsb // SUBLANES)]
    pad = [jnp.full((SUBLANES, tq), -jnp.inf, F32)] * (N_SEL - len(blocks))
    kth = _top16_sorted(blocks + pad)[N_SEL - 1][0:1, :]
    above = [jnp.where(b > kth, 1.0, 0.0) for b in blocks]
    equal = [jnp.where(b == kth, 1.0, 0.0) for b in blocks]
    n_above = above[0]
    for a in above[1:]:
        n_above = n_above + a
    need = float(N_SEL) - jnp.sum(n_above, axis=0, keepdims=True)
    sub = lax.broadcasted_iota(jnp.int32, (SUBLANES, tq), 0)
    before = jnp.zeros((1, tq), F32)
    picked = []
    for a, e in zip(above, equal):
        run = e
        for sh in (1, 2, 4):
            run = run + jnp.where(sub >= sh, pltpu.roll(run, sh, axis=0), 0.0)
        first = jnp.where((run - e) + before < need, e, 0.0)
        picked.append(jnp.where(a + first > 0.5, 0.0, MASK_FILL))
        before = before + jnp.max(run, axis=0, keepdims=True)
    neg_sel = jnp.concatenate(picked, axis=0)

    bpt = tk // SLC_BLOCK
    tile_max = neg_sel
    for sh in range(1, bpt):
        tile_max = jnp.maximum(tile_max, pltpu.roll(neg_sel, nsb - sh, axis=0))
    hit = jnp.max(tile_max, axis=1, keepdims=True) > 0.5 * MASK_FILL
    row = lax.broadcasted_iota(jnp.int32, (nsb, 1), 0)
    bit = jnp.where(hit & (lax.rem(row, bpt) == 0),
                    lax.shift_left(jnp.int32(1), lax.rem(row // bpt, TILE_WORD_BITS)), 0).astype(F32)
    in_lo = row < TILE_WORD_BITS * bpt
    word_lo = jnp.sum(jnp.where(in_lo, bit, 0.0)).astype(jnp.int32)
    word_hi = jnp.sum(jnp.where(in_lo, 0.0, bit)).astype(jnp.int32)

    if nsbp > nsb:
        neg_sel = jnp.concatenate([neg_sel, jnp.zeros((nsbp - nsb, tq), F32)], axis=0)
    neg_sel = neg_sel.astype(BF16)

    wl = hpg * tq
    rc = BF16_ROW_GROUP
    kio_c = lax.broadcasted_iota(jnp.int32, (rc, wl), 0)
    tio_c = lax.rem(lax.broadcasted_iota(jnp.int32, (rc, wl), 1), tq)
    slope_row = jnp.concatenate([jnp.full((1, tq), slopes[hh], F32) for hh in range(hpg)], axis=1)
    for hh in range(hpg):
        qat_ref[0:HD, hh * tq:(hh + 1) * tq] = q_ref[hh].astype(F32).T.astype(BF16)
        qat_ref[HD:, hh * tq:(hh + 1) * tq] = neg_sel
    b_ref[...] = slope_row * lax.broadcasted_iota(jnp.int32, (tk, wl), 0).astype(F32)

    chunks = [slice(r * rc, (r + 1) * rc) for r in range(tk // rc)]

    def key_tile(k_ref, ki):
        return k_ref[0, pl.ds(pl.multiple_of(ki * tk, tk), tk), :]

    def bias_and_max(mask_kind):
        acc = None
        for r, ch in enumerate(chunks):
            t = s_ref[ch, :] + b_ref[ch, :]
            if mask_kind == "causal":
                t = t + jnp.where(kio_c + r * rc > tio_c, MASK_FILL, 0.0)
            elif mask_kind == "beyond":
                t = t + jnp.where(kio_c + r * rc <= tio_c, MASK_FILL, 0.0)
            s_ref[ch, :] = t
            acc = t if acc is None else jnp.maximum(acc, t)
        return jnp.max(acc, axis=0, keepdims=True)

    def exp_pass(shift):
        lsum = jnp.zeros((rc, wl), F32)
        for ch in chunks:
            p = jnp.exp2(s_ref[ch, :] + shift)
            p_ref[ch, :] = p.astype(BF16)
            lsum = lsum + p
        return jnp.sum(lsum, axis=0, keepdims=True)

    def first_tile(s, v_t):
        s_ref[...] = s
        m = bias_and_max("causal")
        m_ref[...] = m
        l_ref[...] = exp_pass(-m)
        acc_ref[...] = jnp.dot(v_t, p_ref[...], preferred_element_type=F32)

    def next_tile(s, back, v_t, mask_kind):
        s_ref[...] = s
        c = -slope_row * back
        m_old = m_ref[...]
        m_new = jnp.maximum(m_old, bias_and_max(mask_kind) + c)
        alpha = jnp.exp2(m_old - m_new)
        l_ref[...] = alpha * l_ref[...] + exp_pass(c - m_new)
        acc_ref[...] = alpha * acc_ref[...] + jnp.dot(v_t, p_ref[...], preferred_element_type=F32)
        m_ref[...] = m_new

    def sel_scores(ki):
        return jnp.dot(key_tile(ks_ref, ki), qat_ref[...], preferred_element_type=F32)

    def win_scores(ki):
        return jnp.dot(key_tile(kw_ref, ki), qat_ref[0:HD, :], preferred_element_type=F32)

    first_tile(sel_scores(qi), vst_ref[0, qi])

    def sel_body(step, carry):
        ki = qi - 1 - step
        word = jnp.where(ki < TILE_WORD_BITS, word_lo, word_hi)

        @pl.when((lax.shift_right_logical(word, lax.rem(ki, TILE_WORD_BITS)) & 1) != 0)
        def _():
            next_tile(sel_scores(ki), ((step + 1) * tk).astype(F32), vst_ref[0, ki], None)
        return carry

    lax.fori_loop(0, qi, sel_body, 0)
    os_ref[...] = acc_ref[...] / jnp.maximum(l_ref[...], TINY)

    first_tile(win_scores(qi), vwt_ref[0, qi])
    for back_tiles in range(1, WINDOW // tk + 1):
        @pl.when(qi >= back_tiles)
        def _(back_tiles=back_tiles):
            ki = qi - back_tiles
            next_tile(win_scores(ki), float(back_tiles * tk), vwt_ref[0, ki],
                      "beyond" if back_tiles == WINDOW // tk else None)
    o_w = acc_ref[...] / jnp.maximum(l_ref[...], TINY)

    for hh in range(hpg):
        cols = slice(hh * tq, (hh + 1) * tq)
        gates = [gate_ref[0, hh * NSA_BRANCHES + br:hh * NSA_BRANCHES + br + 1, :] for br in range(NSA_BRANCHES)]
        o_t = gates[0] * oc_ref[hh] + gates[1] * os_ref[:, cols] + gates[2] * o_w[:, cols]
        o_ref[:, hh * HD:(hh + 1) * HD] = o_t.T.astype(BF16)


def _mm_res_kernel(a_ref, w_ref, h_ref, o_ref):
    o_ref[...] = jnp.dot(a_ref[...], w_ref[...], preferred_element_type=F32) + h_ref[...]


def _nsa_layer(h, B, S, norm_g, w_in, q_norm, w_out, k_c, v_ct, ks, kw, vt, tq):
    T, D = h.shape
    G, HD = NSA_KV_GROUPS, NSA_HEAD_DIM
    width = w_out.shape[0]
    heads = width // HD
    hpg = heads // G
    n_gate = heads * NSA_BRANCHES
    w = w_in.astype(BF16)
    tm = TILES.tokens
    q, gate_t = pl.pallas_call(
        _nsa_in_kernel,
        grid=(T // tm,),
        in_specs=[pl.BlockSpec((tm, D), lambda i: (i, 0)),
                  pl.BlockSpec((1, D), lambda i: (0, 0)),
                  pl.BlockSpec((D, width), lambda i: (0, 0)),
                  pl.BlockSpec((n_gate, D), lambda i: (0, 0)),
                  pl.BlockSpec((1, HD), lambda i: (0, 0))],
        out_specs=[pl.BlockSpec((heads, tm, HD), lambda i: (0, i, 0)),
                   pl.BlockSpec((n_gate, tm), lambda i: (0, i))],
        out_shape=[jax.ShapeDtypeStruct((heads, T, HD), BF16), jax.ShapeDtypeStruct((n_gate, T), F32)],
        compiler_params=_cparams(("parallel",)),
        name="nsa_in",
    )(h, norm_g.reshape(1, D), w[:, :width], w[:, width:].T, q_norm.reshape(1, HD))

    ncb = S // CMP_STRIDE
    nsb = S // SLC_BLOCK
    n_cmp = (S - CMP_BLOCK) // CMP_STRIDE + 1
    c_start = jnp.arange(ncb) * CMP_STRIDE
    c_pos = c_start + CMP_BLOCK - 1
    s_start = jnp.arange(nsb) * SLC_BLOCK
    ov_t = ((c_start[None, :] < s_start[:, None] + SLC_BLOCK) & (c_pos[None, :] >= s_start[:, None])
            & (jnp.arange(ncb)[None, :] < n_cmp)).astype(BF16)
    h_idx = jnp.arange(1, heads + 1, dtype=F32)
    slopes = 2.0 ** (-8.0 * h_idx / heads) * LOG2E
    nq = S // tq
    assert nq <= 2 * TILE_WORD_BITS, "the key-tile flags of a query tile are kept in two words"
    w_sel = ks.shape[2]
    o = pl.pallas_call(
        functools.partial(_nsa_attn_kernel, tq=tq, hpg=hpg),
        grid_spec=pltpu.PrefetchScalarGridSpec(
            num_scalar_prefetch=1,
            grid=(B, G, nq),
            in_specs=[pl.BlockSpec((hpg, tq, HD), lambda b, g, qi, sl: (g, b * nq + qi, 0)),
                      pl.BlockSpec((1, hpg * NSA_BRANCHES, tq), lambda b, g, qi, sl: (g, 0, b * nq + qi)),
                      pl.BlockSpec((1, ncb, HD), lambda b, g, qi, sl: (b * G + g, 0, 0)),
                      pl.BlockSpec((1, HD, ncb), lambda b, g, qi, sl: (b * G + g, 0, 0)),
                      pl.BlockSpec((1, S, w_sel), lambda b, g, qi, sl: (g, b, 0)),
                      pl.BlockSpec((1, nq, HD, tq), lambda b, g, qi, sl: (g, b, 0, 0)),
                      pl.BlockSpec((1, S, HD), lambda b, g, qi, sl: (g, b, 0)),
                      pl.BlockSpec((1, nq, HD, tq), lambda b, g, qi, sl: (G + g, b, 0, 0)),
                      pl.BlockSpec((nsb, ncb), lambda b, g, qi, sl: (0, 0))],
            out_specs=pl.BlockSpec((tq, hpg * HD), lambda b, g, qi, sl: (b * nq + qi, g)),
            scratch_shapes=[pltpu.VMEM((w_sel, hpg * tq), BF16),
                            pltpu.VMEM((tq, hpg * tq), F32),
                            pltpu.VMEM((tq, hpg * tq), F32),
                            pltpu.VMEM((tq, hpg * tq), BF16),
                            pltpu.VMEM((1, hpg * tq), F32),
                            pltpu.VMEM((1, hpg * tq), F32),
                            pltpu.VMEM((HD, hpg * tq), F32),
                            pltpu.VMEM((hpg, HD, tq), F32),
                            pltpu.VMEM((HD, hpg * tq), F32)]),
        out_shape=jax.ShapeDtypeStruct((T, width), BF16),
        compiler_params=_cparams(("parallel", "parallel", "arbitrary")),
        name="nsa_attn",
    )(slopes, q, gate_t.reshape(G, hpg * NSA_BRANCHES, T), k_c, v_ct, ks, vt, kw, vt, ov_t)

    return pl.pallas_call(
        _mm_res_kernel,
        grid=(T // tm,),
        in_specs=[pl.BlockSpec((tm, width), lambda i: (i, 0)),
                  pl.BlockSpec((width, D), lambda i: (0, 0)),
                  pl.BlockSpec((tm, D), lambda i: (i, 0))],
        out_specs=pl.BlockSpec((tm, D), lambda i: (i, 0)),
        out_shape=jax.ShapeDtypeStruct((T, D), F32),
        compiler_params=_cparams(("parallel",)),
        name="nsa_out",
    )(o, w_out.astype(BF16), h)


def kernel(x, a_norm, a_w_in, a_v_norm, a_w_s, a_b_s, a_w_out, kv_norm, kv_w, kv_k_norm, phi_pe, phi_w1, phi_b1, phi_w2, phi_b2, b_norm, b_w_in, b_q_norm, b_w_out, ffn_norm, peer_w_q, peer_keys, peer_u, peer_v):
    B, S, D = x.shape
    depth = ffn_norm.shape[0]
    n_a = a_norm.shape[0]
    tq = TILES.nsa
    h = x.reshape(B * S, D)
    shared = None
    for layer in range(depth):
        if layer < n_a:
            a = layer
            h = _gmlp_layer(h, a_norm[a], a_w_in[a], a_v_norm[a], a_w_s[a], a_b_s[a], a_w_out[a])
        else:
            if shared is None:
                shared = _nsa_shared_kv(h, B, S, kv_norm, kv_w, kv_k_norm, phi_pe, phi_w1, phi_b1, phi_w2,
                                        phi_b2, tq)
            b = layer - n_a
            h = _nsa_layer(h, B, S, b_norm[b], b_w_in[b], b_q_norm[b], b_w_out[b], *shared, tq)
        h = _peer_layer(h, ffn_norm[layer], peer_w_q[layer], peer_keys[layer], peer_u[layer], peer_v[layer])
    return h.reshape(B, S, D)
```
